```python
import jax, jax.numpy as jnp
from jax import lax
import numpy as np

D_MODEL = 2048
BATCH = 2
SEQ = 8192
DEPTH = 2

GRID_W = 64
CTX_LEN = 256
N_MOD = 9
D_FF = 5632
CONV_DIM = D_MODEL // 2
CONV_WIDTH = 3
FOURIER_DIM = D_MODEL // 2
FOURIER_GROUPS = 8
FOURIER_GROUP_DIM = FOURIER_DIM // FOURIER_GROUPS
HEAD_DIM = 64
N_Q_HEADS = D_MODEL // HEAD_DIM
N_KV_HEADS = 4
KV_REP = N_Q_HEADS // N_KV_HEADS
WINDOW = 128
BLOCK = 128
ROPE_BASE = 10000.0
LN_EPS = 1e-5
NEG_INF = -1e30
ALPHA = (2 * DEPTH) ** 0.25
BETA = (8 * DEPTH) ** -0.25
N_EVEN = (DEPTH + 1) // 2
N_ODD = DEPTH // 2

kernel_name = "hybrid_conv_fourier_swa_dit_prefix"


def layer_norm(x, g, b):
    xf = x.astype(jnp.float32)
    mu = xf.mean(-1, keepdims=True)
    var = jnp.square(xf - mu).mean(-1, keepdims=True)
    y = (xf - mu) * lax.rsqrt(var + LN_EPS)
    return (y * g.astype(jnp.float32) + b.astype(jnp.float32)).astype(x.dtype)


def modulate(x, shift, scale):
    return x * (1 + scale) + shift


def post_norm_residual(x, y, gate, g, b):
    return layer_norm(ALPHA * x + gate * y, g, b)


def swiglu(h, wg, wu, wd):
    return (jax.nn.silu(h @ wg) * (h @ wu)) @ wd


def ffn_sublayer(x, shift, scale, gate, wg, wu, wd, g, b):
    y = swiglu(modulate(x, shift, scale), wg, wu, wd)
    return post_norm_residual(x, 0.5 * y, gate, g, b)


def conv_fourier_mixer(h, w_in, w_conv, w_out):
    bsz, seq_len, _ = h.shape
    u = h @ w_in
    g_b, g_c, x_in, u_f = jnp.split(u, [CONV_DIM, 2 * CONV_DIM, 3 * CONV_DIM], axis=-1)
    v = g_c * x_in
    pad = CONV_WIDTH // 2
    vp = jnp.pad(v, ((0, 0), (pad, pad), (0, 0)))
    conv = sum(w_conv[k] * vp[:, k:k + seq_len] for k in range(CONV_WIDTH))
    y_a = g_b * conv
    ug = u_f.astype(jnp.float32).reshape(bsz, seq_len, FOURIER_GROUPS, FOURIER_GROUP_DIM)
    y_b = jnp.fft.fft2(ug, axes=(1, 3), norm="ortho").real
    y_b = y_b.reshape(bsz, seq_len, FOURIER_DIM).astype(h.dtype)
    return jnp.concatenate([y_a, y_b], axis=-1) @ w_out


def axial_rope_tables(seq_len):
    rows = seq_len // GRID_W
    row = jnp.repeat(jnp.arange(rows, dtype=jnp.float32), GRID_W)
    col = jnp.tile(jnp.arange(GRID_W, dtype=jnp.float32), rows)
    n_freq = HEAD_DIM // 4
    inv_freq = jnp.power(ROPE_BASE, -jnp.arange(n_freq, dtype=jnp.float32) / n_freq)
    ang = jnp.concatenate([row[:, None] * inv_freq, col[:, None] * inv_freq], axis=-1)
    return jnp.cos(ang), jnp.sin(ang)


def apply_axial_rope(x, cos, sin):
    b, l, h, d = x.shape
    xa = x.reshape(b, l, h, 2, 2, d // 4)
    x1, x2 = xa[..., 0, :], xa[..., 1, :]
    c = cos.reshape(l, 1, 2, d // 4).astype(x.dtype)
    s = sin.reshape(l, 1, 2, d // 4).astype(x.dtype)
    out = jnp.stack([x1 * c - x2 * s, x2 * c + x1 * s], axis=-2)
    return out.reshape(b, l, h, d)


def attend(sink_g, scores, values):
    sink_col = jnp.broadcast_to(sink_g[None, :, :, None, None], scores[0].shape[:-1] + (1,))
    p = jax.nn.softmax(jnp.concatenate([sink_col] + scores, axis=-1), axis=-1)
    out = 0.0
    start = 1
    for s, v in zip(scores, values):
        n = s.shape[-1]
        out = out + jnp.einsum('bgrqk,bkgd->bqgrd', p[..., start:start + n].astype(v.dtype), v)
        start += n
    return out


def window_attention(h_lat, h_ctx, w_in, sink, w_out, cos, sin, need_ctx_out):
    bsz, seq_len, _ = h_lat.shape
    n_blk = seq_len // BLOCK
    scale = HEAD_DIM ** -0.5
    sink_g = sink.astype(jnp.float32).reshape(N_KV_HEADS, KV_REP)

    def project(h):
        b, n, _ = h.shape
        u = h @ w_in
        q, k, v = jnp.split(u, [N_Q_HEADS * HEAD_DIM, (N_Q_HEADS + N_KV_HEADS) * HEAD_DIM], axis=-1)
        return (q.reshape(b, n, N_Q_HEADS, HEAD_DIM), k.reshape(b, n, N_KV_HEADS, HEAD_DIM),
                v.reshape(b, n, N_KV_HEADS, HEAD_DIM))

    q_c, k_c, v_c = project(h_ctx)
    q_l, k_l, v_l = project(h_lat)
    q_l = apply_axial_rope(q_l, cos, sin)
    k_l = apply_axial_rope(k_l, cos, sin)

    qb = q_l.reshape(bsz, n_blk, BLOCK, N_KV_HEADS, KV_REP, HEAD_DIM).transpose(1, 0, 2, 3, 4, 5)

    def band(t):
        tp = jnp.pad(t, ((0, 0), (BLOCK, BLOCK), (0, 0), (0, 0)))
        tb = tp.reshape(bsz, n_blk + 2, BLOCK, N_KV_HEADS, HEAD_DIM)
        tband = jnp.concatenate([tb[:, :-2], tb[:, 1:-1], tb[:, 2:]], axis=2)
        return tband.transpose(1, 0, 2, 3, 4)

    k_band, v_band = band(k_l), band(v_l)
    blk = jnp.arange(n_blk)[:, None, None]
    a = jnp.arange(BLOCK)[None, :, None]
    j = jnp.arange(3 * BLOCK)[None, None, :]
    k_pos = blk * BLOCK + j - BLOCK
    mask = (jnp.abs(j - BLOCK - a) <= WINDOW) & (k_pos >= 0) & (k_pos < seq_len)

    def block_step(args):
        qi, ki, vi, mi = args
        s_ctx = jnp.einsum('bqgrd,bcgd->bgrqc', qi, k_c).astype(jnp.float32) * scale
        s_loc = jnp.einsum('bqgrd,bkgd->bgrqk', qi, ki).astype(jnp.float32) * scale
        s_loc = jnp.where(mi, s_loc, NEG_INF)
        return attend(sink_g, [s_ctx, s_loc], [v_c, vi])

    o = lax.map(block_step, (qb, k_band, v_band, mask))
    o = o.transpose(1, 0, 2, 3, 4, 5).reshape(bsz, seq_len, N_Q_HEADS * HEAD_DIM)
    y_lat = o @ w_out

    y_ctx = None
    if need_ctx_out:
        n_ctx = h_ctx.shape[1]
        qc = q_c.reshape(bsz, n_ctx, N_KV_HEADS, KV_REP, HEAD_DIM)
        s_cc = jnp.einsum('bqgrd,bcgd->bgrqc', qc, k_c).astype(jnp.float32) * scale
        oc = attend(sink_g, [s_cc], [v_c]).reshape(bsz, n_ctx, N_Q_HEADS * HEAD_DIM)
        y_ctx = oc @ w_out
    return y_lat, y_ctx


def setup_inputs(seed: int = 0) -> dict:
    key = jax.random.key(seed)
    ks = jax.random.split(key, 20)
    f32 = jnp.float32

    def nrm(k, shape, fan_in, gain=1.0):
        return jax.random.normal(k, shape, f32) * (gain * fan_in ** -0.5)

    qkv_w = (N_Q_HEADS + 2 * N_KV_HEADS) * HEAD_DIM
    return {
        "x": jax.random.normal(ks[0], (BATCH, SEQ, D_MODEL), f32),
        "c": jax.random.normal(ks[1], (BATCH, D_MODEL), f32),
        "ctx": jax.random.normal(ks[2], (BATCH, CTX_LEN, D_MODEL), f32),
        "c_ctx": jax.random.normal(ks[3], (D_MODEL,), f32),
        "w_mod": nrm(ks[4], (DEPTH, D_MODEL, N_MOD * D_MODEL), D_MODEL),
        "b_mod": 0.01 * jax.random.normal(ks[5], (DEPTH, N_MOD * D_MODEL), f32),
        "ln_g": 1.0 + 0.02 * jax.random.normal(ks[6], (DEPTH, 3, D_MODEL), f32),
        "ln_b": 0.02 * jax.random.normal(ks[7], (DEPTH, 3, D_MODEL), f32),
        "ffn_w_gate": nrm(ks[8], (DEPTH, 2, D_MODEL, D_FF), D_MODEL),
        "ffn_w_up": nrm(ks[9], (DEPTH, 2, D_MODEL, D_FF), D_MODEL),
        "ffn_w_down": nrm(ks[10], (DEPTH, 2, D_FF, D_MODEL), D_FF, BETA),
        "ab_w_in": nrm(ks[11], (N_EVEN, D_MODEL, 3 * CONV_DIM + FOURIER_DIM), D_MODEL),
        "ab_conv": nrm(ks[12], (N_EVEN, CONV_WIDTH, CONV_DIM), CONV_WIDTH),
        "ab_w_out": nrm(ks[13], (N_EVEN, CONV_DIM + FOURIER_DIM, D_MODEL), CONV_DIM + FOURIER_DIM, BETA),
        "attn_w_in": nrm(ks[14], (N_ODD, D_MODEL, qkv_w), D_MODEL),
        "attn_sink": jax.random.normal(ks[15], (N_ODD, N_Q_HEADS), f32),
        "attn_w_out": nrm(ks[16], (N_ODD, N_Q_HEADS * HEAD_DIM, D_MODEL), N_Q_HEADS * HEAD_DIM, BETA),
    }


def reference(x, c, ctx, c_ctx, w_mod, b_mod, ln_g, ln_b, ffn_w_gate, ffn_w_up, ffn_w_down,
              ab_w_in, ab_conv, ab_w_out, attn_w_in, attn_sink, attn_w_out):
    xl, xc = x, ctx
    cos, sin = axial_rope_tables(x.shape[1])
    for layer in range(DEPTH):
        last = layer == DEPTH - 1
        even = layer % 2 == 0
        idx = layer // 2
        ml = jnp.split((jax.nn.silu(c) @ w_mod[layer] + b_mod[layer])[:, None, :], N_MOD, axis=-1)
        mc = jnp.split((jax.nn.silu(c_ctx) @ w_mod[layer] + b_mod[layer])[None, None, :], N_MOD, axis=-1)
        ctx_ffn1 = not (last and even)
        ctx_rest = not last

        xl = ffn_sublayer(xl, ml[0], ml[1], ml[2], ffn_w_gate[layer, 0], ffn_w_up[layer, 0],
                          ffn_w_down[layer, 0], ln_g[layer, 0], ln_b[layer, 0])
        if ctx_ffn1:
            xc = ffn_sublayer(xc, mc[0], mc[1], mc[2], ffn_w_gate[layer, 0], ffn_w_up[layer, 0],
                              ffn_w_down[layer, 0], ln_g[layer, 0], ln_b[layer, 0])

        if even:
            yl = conv_fourier_mixer(modulate(xl, ml[3], ml[4]), ab_w_in[idx], ab_conv[idx], ab_w_out[idx])
            xl = post_norm_residual(xl, yl, ml[5], ln_g[layer, 1], ln_b[layer, 1])
            if ctx_rest:
                yc = conv_fourier_mixer(modulate(xc, mc[3], mc[4]), ab_w_in[idx], ab_conv[idx], ab_w_out[idx])
                xc = post_norm_residual(xc, yc, mc[5], ln_g[layer, 1], ln_b[layer, 1])
        else:
            yl, yc = window_attention(modulate(xl, ml[3], ml[4]), modulate(xc, mc[3], mc[4]),
                                      attn_w_in[idx], attn_sink[idx], attn_w_out[idx], cos, sin, ctx_rest)
            xl = post_norm_residual(xl, yl, ml[5], ln_g[layer, 1], ln_b[layer, 1])
            if ctx_rest:
                xc = post_norm_residual(xc, yc, mc[5], ln_g[layer, 1], ln_b[layer, 1])

        xl = ffn_sublayer(xl, ml[6], ml[7], ml[8], ffn_w_gate[layer, 1], ffn_w_up[layer, 1],
                          ffn_w_down[layer, 1], ln_g[layer, 2], ln_b[layer, 2])
        if ctx_rest:
            xc = ffn_sublayer(xc, mc[6], mc[7], mc[8], ffn_w_gate[layer, 1], ffn_w_up[layer, 1],
                              ffn_w_down[layer, 1], ln_g[layer, 2], ln_b[layer, 2])
    return xl
```

```python
import functools
import math

import jax
import jax.numpy as jnp
from jax import lax
from jax.experimental import pallas as pl
from jax.experimental.pallas import tpu as pltpu

F32 = jnp.float32
BF16 = jnp.bfloat16

D_MODEL = 2048
GRID_W = 64
N_MOD = 9
D_FF = 5632
CONV_DIM = 1024
FOURIER_DIM = 1024
FOURIER_GROUPS = 8
FOURIER_GROUP_DIM = 128
HEAD_DIM = 64
N_Q_HEADS = 32
N_KV_HEADS = 4
KV_REP = 8
WINDOW = 128
BLOCK = 128
ROPE_BASE = 10000.0
LN_EPS = 1e-5
NEG_INF = -1e30
DEPTH = 2
ALPHA = (2 * DEPTH) ** 0.25

LANES = 128
V7X_VMEM_BYTES = 64 * 1024 * 1024
MIB = 1024 * 1024

N_COND = 8
CTX_COND = 2
FFN_TM = 512
FFN_TF = 512
ROW_CHUNK = 128
PROJ_TM = 512
FFT_N1 = 64
FFT_TN = 8192


def _cparams(semantics, vmem_mib):
    return pltpu.CompilerParams(dimension_semantics=semantics,
                                vmem_limit_bytes=min(vmem_mib * MIB, V7X_VMEM_BYTES))


def _dot(a, b):
    return jnp.dot(a, b, preferred_element_type=F32)


def _layer_norm_rows(z, g, b):
    mu = jnp.mean(z, axis=-1, keepdims=True)
    zc = z - mu
    var = jnp.mean(zc * zc, axis=-1, keepdims=True)
    return zc * lax.rsqrt(var + LN_EPS) * g + b


def _mod_kernel(c_ref, w_ref, b_ref, o_ref):
    c = c_ref[...]
    a = (c * jax.nn.sigmoid(c)).astype(BF16)
    o_ref[...] = _dot(a, w_ref[...].astype(BF16)) + b_ref[...]


def _modulation(cond, w_mod, b_mod):
    tn = 1024
    n = N_MOD * D_MODEL
    return pl.pallas_call(
        _mod_kernel,
        out_shape=jax.ShapeDtypeStruct((DEPTH, N_COND, n), F32),
        grid=(DEPTH, n // tn),
        in_specs=[
            pl.BlockSpec((N_COND, D_MODEL), lambda l, j: (0, 0)),
            pl.BlockSpec((None, D_MODEL, tn), lambda l, j: (l, 0, j)),
            pl.BlockSpec((None, 1, tn), lambda l, j: (l, 0, j)),
        ],
        out_specs=pl.BlockSpec((None, N_COND, tn), lambda l, j: (l, 0, j)),
        compiler_params=_cparams(("parallel", "parallel"), 40),
        name="modulation",
    )(cond, w_mod, b_mod.reshape(DEPTH, 1, n))


def _mod_spec(layer, k, cond_of_tile):
    return pl.BlockSpec((None, None, None, 1, D_MODEL),
                        lambda i, *_: (layer, cond_of_tile(i), k, 0, 0))


def _ln_spec(layer, k):
    return pl.BlockSpec((None, None, 1, D_MODEL), lambda i, *_: (layer, k, 0, 0))


def _ffn_kernel(x_ref, sh_ref, sc_ref, gt_ref, wg_ref, wu_ref, wd_ref, lg_ref, lb_ref,
                o_ref, xm_ref):
    f = pl.program_id(1)
    n_chunks = x_ref.shape[0] // ROW_CHUNK

    @pl.when(f == 0)
    def _():
        scale1 = 1.0 + sc_ref[...]
        shift = sh_ref[...]

        def body(c, carry):
            rows = pl.ds(pl.multiple_of(c * ROW_CHUNK, ROW_CHUNK), ROW_CHUNK)
            xm_ref[rows, :] = (x_ref[rows, :] * scale1 + shift).astype(BF16)
            o_ref[rows, :] = jnp.zeros((ROW_CHUNK, D_MODEL), F32)
            return carry

        lax.fori_loop(0, n_chunks, body, 0)

    xm = xm_ref[...]
    h_gate = _dot(xm, wg_ref[...])
    h_up = _dot(xm, wu_ref[...])
    act = (h_gate * jax.nn.sigmoid(h_gate) * h_up).astype(BF16)
    o_ref[...] += _dot(act, wd_ref[...])

    @pl.when(f == pl.num_programs(1) - 1)
    def _():
        half_gate = 0.5 * gt_ref[...]
        g = lg_ref[...]
        b = lb_ref[...]

        def body(c, carry):
            rows = pl.ds(pl.multiple_of(c * ROW_CHUNK, ROW_CHUNK), ROW_CHUNK)
            z = ALPHA * x_ref[rows, :] + half_gate * o_ref[rows, :]
            o_ref[rows, :] = _layer_norm_rows(z, g, b)
            return carry

        lax.fori_loop(0, n_chunks, body, 0)


def _ffn(x, mod, layer, sub, k0, cond_of_tile, wg, wu, wd, ln_g, ln_b, tm):
    rows = x.shape[0]
    w_in_spec = pl.BlockSpec((None, None, D_MODEL, FFN_TF), lambda i, f: (layer, sub, 0, f))
    return pl.pallas_call(
        _ffn_kernel,
        out_shape=jax.ShapeDtypeStruct((rows, D_MODEL), F32),
        grid=(rows // tm, D_FF // FFN_TF),
        in_specs=[
            pl.BlockSpec((tm, D_MODEL), lambda i, f: (i, 0)),
            _mod_spec(layer, k0, cond_of_tile),
            _mod_spec(layer, k0 + 1, cond_of_tile),
            _mod_spec(layer, k0 + 2, cond_of_tile),
            w_in_spec,
            w_in_spec,
            pl.BlockSpec((None, None, FFN_TF, D_MODEL), lambda i, f: (layer, sub, f, 0)),
            _ln_spec(layer, 2 * sub),
            _ln_spec(layer, 2 * sub),
        ],
        out_specs=pl.BlockSpec((tm, D_MODEL), lambda i, f: (i, 0)),
        scratch_shapes=[pltpu.VMEM((tm, D_MODEL), BF16)],
        compiler_params=_cparams(("parallel", "arbitrary"), 48),
        name=f"ffn_l{layer}_s{sub}_{rows}",
    )(x, mod, mod, mod, wg, wu, wd, ln_g, ln_b)


def _mix_in_kernel(x_ref, sh_ref, sc_ref, w_ref, cs_ref, gb_ref, v_ref, f_ref):
    xm = (x_ref[...] * (1.0 + sc_ref[...]) + sh_ref[...]).astype(BF16)
    gb_ref[...] = _dot(xm, w_ref[:, 0:CONV_DIM])
    g_c = _dot(xm, w_ref[:, CONV_DIM:2 * CONV_DIM])
    x_in = _dot(xm, w_ref[:, 2 * CONV_DIM:3 * CONV_DIM])
    v_ref[...] = g_c * x_in
    u_f = _dot(xm, w_ref[:, 3 * CONV_DIM:]).astype(BF16)
    cs = cs_ref[...]
    for g in range(FOURIER_GROUPS):
        lanes = slice(g * FOURIER_GROUP_DIM, (g + 1) * FOURIER_GROUP_DIM)
        res = _dot(u_f[:, lanes], cs)
        f_ref[0, :, lanes] = res[:, :FOURIER_GROUP_DIM]
        f_ref[1, :, lanes] = res[:, FOURIER_GROUP_DIM:]


def _mix_in(x, mod, cond_of_tile, w_in, cs, tm):
    rows = x.shape[0]
    n_in = 3 * CONV_DIM + FOURIER_DIM
    return pl.pallas_call(
        _mix_in_kernel,
        out_shape=(jax.ShapeDtypeStruct((rows, CONV_DIM), F32),
                   jax.ShapeDtypeStruct((rows, CONV_DIM), F32),
                   jax.ShapeDtypeStruct((2, rows, FOURIER_DIM), F32)),
        grid=(rows // tm,),
        in_specs=[
            pl.BlockSpec((tm, D_MODEL), lambda i: (i, 0)),
            _mod_spec(0, 3, cond_of_tile),
            _mod_spec(0, 4, cond_of_tile),
            pl.BlockSpec((None, D_MODEL, n_in), lambda i: (0, 0, 0), pipeline_mode=pl.Buffered(1)),
            pl.BlockSpec((FOURIER_GROUP_DIM, 2 * FOURIER_GROUP_DIM), lambda i: (0, 0)),
        ],
        out_specs=(pl.BlockSpec((tm, CONV_DIM), lambda i: (i, 0)),
                   pl.BlockSpec((tm, CONV_DIM), lambda i: (i, 0)),
                   pl.BlockSpec((2, tm, FOURIER_DIM), lambda i: (0, i, 0))),
        compiler_params=_cparams(("parallel",), 56),
        name=f"mix_in_{rows}",
    )(x, mod, mod, w_in, cs)


def _fft1_kernel(m_ref, v_ref, o_ref):
    n1 = v_ref.shape[1]
    tn = v_ref.shape[2]
    v = v_ref[...].reshape(2 * n1, tn).astype(BF16)
    o_ref[...] = _dot(m_ref[...], v).reshape(2, n1, tn)


def _fft1(v, m1, batch, n1, n2):
    cols = n2 * FOURIER_DIM
    v4 = v.reshape(2, batch, n1, cols)
    out = pl.pallas_call(
        _fft1_kernel,
        out_shape=jax.ShapeDtypeStruct((2, batch, n1, cols), F32),
        grid=(batch, cols // FFT_TN),
        in_specs=[
            pl.BlockSpec((2 * n1, 2 * n1), lambda b, j: (0, 0)),
            pl.BlockSpec((2, None, n1, FFT_TN), lambda b, j: (0, b, 0, j)),
        ],
        out_specs=pl.BlockSpec((2, None, n1, FFT_TN), lambda b, j: (0, b, 0, j)),
        compiler_params=_cparams(("parallel", "parallel"), 40),
        name="fft_stage1",
    )(m1, v4)
    return out


def _fft2_kernel(t_ref, twc_ref, tws_ref, f_ref, o_ref, *, scale):
    reps = FOURIER_DIM // LANES
    c = jnp.concatenate([twc_ref[...]] * reps, axis=1)
    s = jnp.concatenate([tws_ref[...]] * reps, axis=1)
    tr = t_ref[0]
    ti = t_ref[1]
    pr = (tr * c + ti * s).astype(BF16)
    pi = (ti * c - tr * s).astype(BF16)
    rhs = jnp.concatenate([pr, pi], axis=0)
    o_ref[...] = _dot(f_ref[...], rhs) * scale


def _fft2(t, twc, tws, f2, batch, n1, n2):
    t5 = t.reshape(2, batch, n1, n2, FOURIER_DIM)
    scale = 1.0 / math.sqrt(n1 * n2 * FOURIER_GROUP_DIM)
    out = pl.pallas_call(
        functools.partial(_fft2_kernel, scale=scale),
        out_shape=jax.ShapeDtypeStruct((batch, n2, n1 * FOURIER_DIM), F32),
        grid=(batch, n1),
        in_specs=[
            pl.BlockSpec((2, None, None, n2, FOURIER_DIM), lambda b, k: (0, b, k, 0, 0)),
            pl.BlockSpec((None, n2, LANES), lambda b, k: (k, 0, 0)),
            pl.BlockSpec((None, n2, LANES), lambda b, k: (k, 0, 0)),
            pl.BlockSpec((n2, 2 * n2), lambda b, k: (0, 0)),
        ],
        out_specs=pl.BlockSpec((None, n2, FOURIER_DIM), lambda b, k: (b, 0, k)),
        compiler_params=_cparams(("parallel", "parallel"), 32),
        name=f"fft_stage2_{n1}x{n2}",
    )(t5, twc, tws, f2)
    return out.reshape(batch * n2 * n1, FOURIER_DIM)


def _mix_out_kernel(x_ref, gt_ref, gb_ref, v_ref, vp_ref, vn_ref, yb_ref, wc_ref, wo_ref,
                    lg_ref, lb_ref, o_ref, *, tiles_per_seq):
    i = pl.program_id(0)
    tm = v_ref.shape[0]
    pos = lax.rem(i, tiles_per_seq)
    v = v_ref[...]
    prev_row = jnp.where(pos == 0, 0.0, vp_ref[7:8, :])
    next_row = jnp.where(pos == tiles_per_seq - 1, 0.0, vn_ref[0:1, :])
    row = lax.broadcasted_iota(jnp.int32, v.shape, 0)
    v_prev = jnp.where(row == 0, prev_row, pltpu.roll(v, 1, 0))
    v_next = jnp.where(row == tm - 1, next_row, pltpu.roll(v, tm - 1, 0))
    conv = wc_ref[0:1, :] * v_prev + wc_ref[1:2, :] * v + wc_ref[2:3, :] * v_next
    y_a = (gb_ref[...] * conv).astype(BF16)
    y = _dot(y_a, wo_ref[0:CONV_DIM, :]) + _dot(yb_ref[...].astype(BF16), wo_ref[CONV_DIM:, :])
    z = ALPHA * x_ref[...] + gt_ref[...] * y
    o_ref[...] = _layer_norm_rows(z, lg_ref[...], lb_ref[...])


def _mix_out(x, mod, cond_of_tile, gb, v, yb, w_conv, w_out, ln_g, ln_b, tm, seq_len):
    rows = x.shape[0]
    sub = 8
    v3 = v.reshape(rows // sub, sub, CONV_DIM)
    step = tm // sub
    last = rows // sub - 1
    return pl.pallas_call(
        functools.partial(_mix_out_kernel, tiles_per_seq=seq_len // tm),
        out_shape=jax.ShapeDtypeStruct((rows, D_MODEL), F32),
        grid=(rows // tm,),
        in_specs=[
            pl.BlockSpec((tm, D_MODEL), lambda i: (i, 0)),
            _mod_spec(0, 5, cond_of_tile),
            pl.BlockSpec((tm, CONV_DIM), lambda i: (i, 0)),
            pl.BlockSpec((tm, CONV_DIM), lambda i: (i, 0)),
            pl.BlockSpec((None, sub, CONV_DIM), lambda i: (jnp.maximum(i * step - 1, 0), 0, 0)),
            pl.BlockSpec((None, sub, CONV_DIM), lambda i: (jnp.minimum((i + 1) * step, last), 0, 0)),
            pl.BlockSpec((tm, FOURIER_DIM), lambda i: (i, 0)),
            pl.BlockSpec((None, 3, CONV_DIM), lambda i: (0, 0, 0)),
            pl.BlockSpec((None, D_MODEL, D_MODEL), lambda i: (0, 0, 0), pipeline_mode=pl.Buffered(1)),
            _ln_spec(0, 1),
            _ln_spec(0, 1),
        ],
        out_specs=pl.BlockSpec((tm, D_MODEL), lambda i: (i, 0)),
        compiler_params=_cparams(("parallel",), 56),
        name=f"mix_out_{rows}",
    )(x, mod, gb, v, v3, v3, yb, w_conv, w_out, ln_g, ln_b)


def _rope_block(blk, cos, sin, first_half):
    rot = jnp.where(first_half, pltpu.roll(blk, LANES - 16, 1), pltpu.roll(blk, 16, 1))
    return blk * cos + rot * sin


def _dup_heads(blk, low_half):
    swapped = pltpu.roll(blk, HEAD_DIM, 1)
    return jnp.where(low_half, blk, swapped), jnp.where(low_half, swapped, blk)


def _qkv_kernel(x_ref, sh_ref, sc_ref, w_ref, cos_ref, sin_ref, q_ref, k_ref, v_ref, *, rope):
    tm = x_ref.shape[0]
    xm = (x_ref[...] * (1.0 + sc_ref[...]) + sh_ref[...]).astype(BF16)
    lane = lax.broadcasted_iota(jnp.int32, (tm, LANES), 1)
    first_half = jnp.bitwise_and(lane, 31) < 16
    low_half = lane < HEAD_DIM
    n_q = N_Q_HEADS * HEAD_DIM
    n_kv = N_KV_HEADS * HEAD_DIM
    if rope:
        cos = cos_ref[...]
        sin = sin_ref[...]
    if q_ref is not None:
        q = _dot(xm, w_ref[:, 0:n_q])
        for j in range(n_q // LANES):
            lanes = slice(j * LANES, (j + 1) * LANES)
            blk = q[:, lanes]
            if rope:
                blk = _rope_block(blk, cos, sin, first_half)
            q_ref[:, lanes] = (blk * (HEAD_DIM ** -0.5)).astype(BF16)
    k = _dot(xm, w_ref[:, n_q:n_q + n_kv])
    v = _dot(xm, w_ref[:, n_q + n_kv:])
    for j in range(n_kv // LANES):
        lanes = slice(j * LANES, (j + 1) * LANES)
        kb = k[:, lanes]
        if rope:
            kb = _rope_block(kb, cos, sin, first_half)
        k0, k1 = _dup_heads(kb, low_half)
        v0, v1 = _dup_heads(v[:, lanes], low_half)
        k_ref[:, 2 * j * LANES:(2 * j + 1) * LANES] = k0.astype(BF16)
        k_ref[:, (2 * j + 1) * LANES:(2 * j + 2) * LANES] = k1.astype(BF16)
        v_ref[:, 2 * j * LANES:(2 * j + 1) * LANES] = v0.astype(BF16)
        v_ref[:, (2 * j + 1) * LANES:(2 * j + 2) * LANES] = v1.astype(BF16)


def _qkv_latent_kernel(x_ref, sh_ref, sc_ref, w_ref, cos_ref, sin_ref, q_ref, k_ref, v_ref):
    _qkv_kernel(x_ref, sh_ref, sc_ref, w_ref, cos_ref, sin_ref, q_ref, k_ref, v_ref, rope=True)


def _kv_context_kernel(x_ref, sh_ref, sc_ref, w_ref, k_ref, v_ref):
    _qkv_kernel(x_ref, sh_ref, sc_ref, w_ref, None, None, None, k_ref, v_ref, rope=False)


def _qkv_latent(x, mod, cond_of_tile, w_in, cos_t, sin_t, tm, seq_len):
    rows = x.shape[0]
    n_in = (N_Q_HEADS + 2 * N_KV_HEADS) * HEAD_DIM
    tiles_per_seq = seq_len // tm
    dup = N_KV_HEADS * LANES
    return pl.pallas_call(
        _qkv_latent_kernel,
        out_shape=(jax.ShapeDtypeStruct((rows, N_Q_HEADS * HEAD_DIM), BF16),
                   jax.ShapeDtypeStruct((rows, dup), BF16),
                   jax.ShapeDtypeStruct((rows, dup), BF16)),
        grid=(rows // tm,),
        in_specs=[
            pl.BlockSpec((tm, D_MODEL), lambda i: (i, 0)),
            _mod_spec(1, 3, cond_of_tile),
            _mod_spec(1, 4, cond_of_tile),
            pl.BlockSpec((None, D_MODEL, n_in), lambda i: (0, 0, 0), pipeline_mode=pl.Buffered(1)),
            pl.BlockSpec((tm, LANES), lambda i: (lax.rem(i, tiles_per_seq), 0)),
            pl.BlockSpec((tm, LANES), lambda i: (lax.rem(i, tiles_per_seq), 0)),
        ],
        out_specs=(pl.BlockSpec((tm, N_Q_HEADS * HEAD_DIM), lambda i: (i, 0)),
                   pl.BlockSpec((tm, dup), lambda i: (i, 0)),
                   pl.BlockSpec((tm, dup), lambda i: (i, 0))),
        compiler_params=_cparams(("parallel",), 48),
        name="qkv_latent",
    )(x, mod, mod, w_in, cos_t, sin_t)


def _kv_context(x, mod, cond_of_tile, w_in, tm):
    rows = x.shape[0]
    n_in = (N_Q_HEADS + 2 * N_KV_HEADS) * HEAD_DIM
    dup = N_KV_HEADS * LANES
    return pl.pallas_call(
        _kv_context_kernel,
        out_shape=(jax.ShapeDtypeStruct((rows, dup), BF16),
                   jax.ShapeDtypeStruct((rows, dup), BF16)),
        grid=(rows // tm,),
        in_specs=[
            pl.BlockSpec((tm, D_MODEL), lambda i: (i, 0)),
            _mod_spec(1, 3, cond_of_tile),
            _mod_spec(1, 4, cond_of_tile),
            pl.BlockSpec((None, D_MODEL, n_in), lambda i: (0, 0, 0), pipeline_mode=pl.Buffered(1)),
        ],
        out_specs=(pl.BlockSpec((tm, dup), lambda i: (i, 0)),
                   pl.BlockSpec((tm, dup), lambda i: (i, 0))),
        compiler_params=_cparams(("parallel",), 40),
        name="kv_context",
    )(x, mod, mod, w_in)


def _attn_kernel(sink_ref, q_ref, kl_ref, vl_ref, kc_ref, vc_ref, o_ref, *, seq_len):
    i = pl.program_id(1)
    band = 3 * BLOCK
    n_ctx = kc_ref.shape[0]
    pairs = KV_REP // 2
    rows = 2 * pairs * BLOCK
    start = jnp.clip((i - 1) * BLOCK, 0, seq_len - band)
    start = pl.multiple_of(start, BLOCK)
    delta = start - i * BLOCK
    a = jnp.bitwise_and(lax.broadcasted_iota(jnp.int32, (rows, band), 0), BLOCK - 1)
    j = lax.broadcasted_iota(jnp.int32, (rows, band), 1)
    in_window = jnp.abs(j + delta - a) <= WINDOW
    low_half = lax.broadcasted_iota(jnp.int32, (BLOCK, LANES), 1) < HEAD_DIM

    for g in range(N_KV_HEADS):
        kv_lanes = slice(g * LANES, (g + 1) * LANES)
        q_parts = []
        sink_parts = []
        for p in range(pairs):
            blk = q_ref[:, (g * pairs + p) * LANES:(g * pairs + p + 1) * LANES]
            q_parts.append(jnp.where(low_half, blk, jnp.zeros_like(blk)))
            q_parts.append(jnp.where(low_half, jnp.zeros_like(blk), blk))
            for r in range(2):
                sink_parts.append(jnp.full((BLOCK, 1), sink_ref[g * KV_REP + 2 * p + r], F32))
        q_g = jnp.concatenate(q_parts, axis=0)
        sink = jnp.concatenate(sink_parts, axis=0)
        k_all = jnp.concatenate([kc_ref[:, kv_lanes], kl_ref[pl.ds(start, band), kv_lanes]], axis=0)
        v_all = jnp.concatenate([vc_ref[:, kv_lanes], vl_ref[pl.ds(start, band), kv_lanes]], axis=0)
        s = lax.dot_general(q_g, k_all, (((1,), (1,)), ((), ())), preferred_element_type=F32)
        s_ctx = s[:, :n_ctx]
        s_loc = jnp.where(in_window, s[:, n_ctx:], NEG_INF)
        m = jnp.maximum(jnp.maximum(jnp.max(s_ctx, axis=1, keepdims=True),
                                    jnp.max(s_loc, axis=1, keepdims=True)), sink)
        p_ctx = jnp.exp(s_ctx - m)
        p_loc = jnp.exp(s_loc - m)
        denom = (jnp.sum(p_ctx, axis=1, keepdims=True) + jnp.sum(p_loc, axis=1, keepdims=True)
                 + jnp.exp(sink - m))
        probs = jnp.concatenate([p_ctx, p_loc], axis=1).astype(BF16)
        o = _dot(probs, v_all) / denom
        for p in range(pairs):
            lo = o[2 * p * BLOCK:(2 * p + 1) * BLOCK]
            hi = o[(2 * p + 1) * BLOCK:(2 * p + 2) * BLOCK]
            o_ref[:, (g * pairs + p) * LANES:(g * pairs + p + 1) * LANES] = (
                jnp.where(low_half, lo, hi).astype(BF16))


def _attention(sink, q, k_l, v_l, k_c, v_c, batch, seq_len, n_ctx):
    n_blk = seq_len // BLOCK
    dup = N_KV_HEADS * LANES
    n_q = N_Q_HEADS * HEAD_DIM
    return pl.pallas_call(
        functools.partial(_attn_kernel, seq_len=seq_len),
        out_shape=jax.ShapeDtypeStruct((batch * seq_len, n_q), BF16),
        grid=(batch, n_blk),
        in_specs=[
            pl.BlockSpec(memory_space=pltpu.SMEM),
            pl.BlockSpec((BLOCK, n_q), lambda b, i: (b * n_blk + i, 0)),
            pl.BlockSpec((None, seq_len, dup), lambda b, i: (b, 0, 0)),
            pl.BlockSpec((None, seq_len, dup), lambda b, i: (b, 0, 0)),
            pl.BlockSpec((None, n_ctx, dup), lambda b, i: (b, 0, 0)),
            pl.BlockSpec((None, n_ctx, dup), lambda b, i: (b, 0, 0)),
        ],
        out_specs=pl.BlockSpec((BLOCK, n_q), lambda b, i: (b * n_blk + i, 0)),
        compiler_params=_cparams(("parallel", "arbitrary"), 56),
        name="window_attention",
    )(sink, q, k_l.reshape(batch, seq_len, dup), v_l.reshape(batch, seq_len, dup),
      k_c.reshape(batch, n_ctx, dup), v_c.reshape(batch, n_ctx, dup))


def _attn_out_kernel(x_ref, gt_ref, a_ref, wo_ref, lg_ref, lb_ref, o_ref):
    y = _dot(a_ref[...], wo_ref[...])
    z = ALPHA * x_ref[...] + gt_ref[...] * y
    o_ref[...] = _layer_norm_rows(z, lg_ref[...], lb_ref[...])


def _attn_out(x, mod, cond_of_tile, a, w_out, ln_g, ln_b, tm):
    rows = x.shape[0]
    return pl.pallas_call(
        _attn_out_kernel,
        out_shape=jax.ShapeDtypeStruct((rows, D_MODEL), F32),
        grid=(rows // tm,),
        in_specs=[
            pl.BlockSpec((tm, D_MODEL), lambda i: (i, 0)),
            _mod_spec(1, 5, cond_of_tile),
            pl.BlockSpec((tm, D_MODEL), lambda i: (i, 0)),
            pl.BlockSpec((None, D_MODEL, D_MODEL), lambda i: (0, 0, 0), pipeline_mode=pl.Buffered(1)),
            _ln_spec(1, 1),
            _ln_spec(1, 1),
        ],
        out_specs=pl.BlockSpec((tm, D_MODEL), lambda i: (i, 0)),
        compiler_params=_cparams(("parallel",), 48),
        name="attn_out",
    )(x, mod, a, w_out, ln_g, ln_b)


def _dft_cos_sin(n):
    k = jnp.arange(n, dtype=jnp.int32)
    ang = ((k[:, None] * k[None, :]) % n).astype(F32) * (2.0 * math.pi / n)
    return jnp.cos(ang), jnp.sin(ang)


def _twiddles(n1, n2):
    k1 = jnp.arange(n1, dtype=jnp.int32)[:, None]
    m2 = jnp.arange(n2, dtype=jnp.int32)[None, :]
    n = n1 * n2
    ang = ((k1 * m2) % n).astype(F32) * (2.0 * math.pi / n)
    shape = (n1, n2, LANES)
    return (jnp.broadcast_to(jnp.cos(ang)[:, :, None], shape),
            jnp.broadcast_to(jnp.sin(ang)[:, :, None], shape))


def _rope_lane_tables(seq_len):
    rows = seq_len // GRID_W
    row = jnp.repeat(jnp.arange(rows, dtype=F32), GRID_W)
    col = jnp.tile(jnp.arange(GRID_W, dtype=F32), rows)
    n_freq = HEAD_DIM // 4
    inv_freq = jnp.power(ROPE_BASE, -jnp.arange(n_freq, dtype=F32) / n_freq)
    ang_r = row[:, None] * inv_freq
    ang_c = col[:, None] * inv_freq
    cos_h = jnp.concatenate([jnp.cos(ang_r), jnp.cos(ang_r), jnp.cos(ang_c), jnp.cos(ang_c)], axis=-1)
    sin_h = jnp.concatenate([-jnp.sin(ang_r), jnp.sin(ang_r), -jnp.sin(ang_c), jnp.sin(ang_c)], axis=-1)
    reps = LANES // HEAD_DIM
    return jnp.tile(cos_h, (1, reps)), jnp.tile(sin_h, (1, reps))


def kernel(x, c, ctx, c_ctx, w_mod, b_mod, ln_g, ln_b, ffn_w_gate, ffn_w_up, ffn_w_down,
           ab_w_in, ab_conv, ab_w_out, attn_w_in, attn_sink, attn_w_out):
    batch, seq_len, _ = x.shape
    n_ctx = ctx.shape[1]
    assert batch == 2 and seq_len % FFN_TM == 0 and seq_len % (FFT_N1 * 8) == 0

    xl = x.reshape(batch * seq_len, D_MODEL)
    xc = ctx.reshape(batch * n_ctx, D_MODEL)
    ctx_rows = batch * n_ctx

    cond = jnp.concatenate([c, c_ctx[None, :], jnp.zeros((N_COND - batch - 1, D_MODEL), F32)], axis=0)
    mod = _modulation(cond, w_mod, b_mod).reshape(DEPTH, N_COND, N_MOD, 1, D_MODEL)
    ln_g4 = ln_g.reshape(DEPTH, 3, 1, D_MODEL)
    ln_b4 = ln_b.reshape(DEPTH, 3, 1, D_MODEL)

    wg = ffn_w_gate.astype(BF16)
    wu = ffn_w_up.astype(BF16)
    wd = ffn_w_down.astype(BF16)
    ab_in = ab_w_in.astype(BF16)
    ab_out = ab_w_out.astype(BF16)
    at_in = attn_w_in.astype(BF16)
    at_out = attn_w_out.astype(BF16)

    def lat_cond(tm):
        tiles = seq_len // tm
        return lambda i: i // tiles

    ctx_cond = lambda i: CTX_COND

    def ffn_both(xl, xc, layer, sub, with_ctx):
        k0 = 6 * sub
        xl = _ffn(xl, mod, layer, sub, k0, lat_cond(FFN_TM), wg, wu, wd, ln_g4, ln_b4, FFN_TM)
        if with_ctx:
            xc = _ffn(xc, mod, layer, sub, k0, ctx_cond, wg, wu, wd, ln_g4, ln_b4, ctx_rows)
        return xl, xc

    xl, xc = ffn_both(xl, xc, 0, 0, True)

    cc, sc_ = _dft_cos_sin(FOURIER_GROUP_DIM)
    cs_chan = jnp.concatenate([cc, -sc_], axis=1).astype(BF16)
    n1, n2 = FFT_N1, seq_len // FFT_N1
    c1, s1 = _dft_cos_sin(n1)
    m1 = jnp.concatenate([jnp.concatenate([c1, s1], axis=1),
                          jnp.concatenate([-s1, c1], axis=1)], axis=0).astype(BF16)
    c2, s2 = _dft_cos_sin(n2)
    f2 = jnp.concatenate([c2, s2], axis=1).astype(BF16)
    twc, tws = _twiddles(n1, n2)
    cn, sn = _dft_cos_sin(n_ctx)
    f2_ctx = jnp.concatenate([cn, sn], axis=1).astype(BF16)
    twc_ctx, tws_ctx = _twiddles(1, n_ctx)

    gb, v, vf = _mix_in(xl, mod, lat_cond(PROJ_TM), ab_in, cs_chan, PROJ_TM)
    t = _fft1(vf, m1, batch, n1, n2)
    yb = _fft2(t, twc, tws, f2, batch, n1, n2)
    xl = _mix_out(xl, mod, lat_cond(PROJ_TM), gb, v, yb, ab_conv, ab_out, ln_g4, ln_b4, PROJ_TM, seq_len)

    gb_c, v_c, vf_c = _mix_in(xc, mod, ctx_cond, ab_in, cs_chan, n_ctx)
    yb_c = _fft2(vf_c, twc_ctx, tws_ctx, f2_ctx, batch, 1, n_ctx)
    xc = _mix_out(xc, mod, ctx_cond, gb_c, v_c, yb_c, ab_conv, ab_out, ln_g4, ln_b4, n_ctx, n_ctx)

    xl, xc = ffn_both(xl, xc, 0, 1, True)

    xl, xc = ffn_both(xl, xc, 1, 0, True)

    cos_t, sin_t = _rope_lane_tables(seq_len)
    q, k_l, v_l = _qkv_latent(xl, mod, lat_cond(PROJ_TM), at_in, cos_t, sin_t, PROJ_TM, seq_len)
    k_c, v_c2 = _kv_context(xc, mod, ctx_cond, at_in, n_ctx)
    att = _attention(attn_sink.reshape(N_Q_HEADS), q, k_l, v_l, k_c, v_c2, batch, seq_len, n_ctx)
    xl = _attn_out(xl, mod, lat_cond(PROJ_TM), att, at_out, ln_g4, ln_b4, PROJ_TM)

    xl, _ = ffn_both(xl, xc, 1, 1, False)
    return xl.reshape(batch, seq_len, D_MODEL)
```

```python
import functools
import math

import jax
import jax.numpy as jnp
from jax import lax
from jax.experimental import pallas as pl
from jax.experimental.pallas import tpu as pltpu

F32 = jnp.float32
BF16 = jnp.bfloat16

D_MODEL = 2048
GRID_W = 64
N_MOD = 9
D_FF = 5632
CONV_DIM = 1024
FOURIER_DIM = 1024
FOURIER_GROUPS = 8
FOURIER_GROUP_DIM = 128
HEAD_DIM = 64
N_Q_HEADS = 32
N_KV_HEADS = 4
KV_REP = 8
WINDOW = 128
BLOCK = 128
ROPE_BASE = 10000.0
LN_EPS = 1e-5
NEG_INF = -1e30
DEPTH = 2
ALPHA = (2 * DEPTH) ** 0.25
LOG2_E = math.log2(math.e)

LANES = 128
SUBLANES = 8
V7X_VMEM_BYTES = 64 * 1024 * 1024
MIB = 1024 * 1024

N_COND = 8
CTX_COND = 2
FFN_TM = 1024
FFN_TF = 512
ROW_CHUNK = 128
PROJ_TM = 512
FFT_N1 = 64


def _cparams(semantics, vmem_mib):
    return pltpu.CompilerParams(dimension_semantics=semantics,
                                vmem_limit_bytes=min(vmem_mib * MIB, V7X_VMEM_BYTES))


def _dot(a, b):
    return jnp.dot(a, b, preferred_element_type=F32)


def _layer_norm_rows(z, g, b):
    mu = jnp.mean(z, axis=-1, keepdims=True)
    zc = z - mu
    var = jnp.mean(zc * zc, axis=-1, keepdims=True)
    return zc * lax.rsqrt(var + LN_EPS) * g + b


def _mod_kernel(c_ref, w_ref, b_ref, o_ref):
    c = c_ref[...]
    a = (c * jax.nn.sigmoid(c)).astype(BF16)
    o_ref[...] = _dot(a, w_ref[...].astype(BF16)) + b_ref[...]


def _modulation(cond, w_mod, b_mod):
    tn = 1024
    n = N_MOD * D_MODEL
    return pl.pallas_call(
        _mod_kernel,
        out_shape=jax.ShapeDtypeStruct((DEPTH, N_COND, n), F32),
        grid=(DEPTH, n // tn),
        in_specs=[
            pl.BlockSpec((N_COND, D_MODEL), lambda l, j: (0, 0)),
            pl.BlockSpec((None, D_MODEL, tn), lambda l, j: (l, 0, j)),
            pl.BlockSpec((None, 1, tn), lambda l, j: (l, 0, j)),
        ],
        out_specs=pl.BlockSpec((None, N_COND, tn), lambda l, j: (l, 0, j)),
        compiler_params=_cparams(("parallel", "parallel"), 40),
        name="modulation",
    )(cond, w_mod, b_mod.reshape(DEPTH, 1, n))


def _mod_spec(layer, k, cond_of_tile):
    return pl.BlockSpec((None, None, None, 1, D_MODEL),
                        lambda i, *_: (layer, cond_of_tile(i), k, 0, 0))


def _ln_spec(layer, k):
    return pl.BlockSpec((None, None, 1, D_MODEL), lambda i, *_: (layer, k, 0, 0))


def _ffn_kernel(x_ref, sh_ref, sc_ref, gt_ref, wg_ref, wu_ref, wd_ref, lg_ref, lb_ref,
                o_ref, xm_ref):
    f = pl.program_id(1)
    n_chunks = x_ref.shape[0] // ROW_CHUNK

    @pl.when(f == 0)
    def _():
        scale1 = 1.0 + sc_ref[...]
        shift = sh_ref[...]

        def body(c, carry):
            rows = pl.ds(pl.multiple_of(c * ROW_CHUNK, ROW_CHUNK), ROW_CHUNK)
            xm_ref[rows, :] = (x_ref[rows, :] * scale1 + shift).astype(BF16)
            o_ref[rows, :] = jnp.zeros((ROW_CHUNK, D_MODEL), F32)
            return carry

        lax.fori_loop(0, n_chunks, body, 0)

    xm = xm_ref[...]
    h_gate = _dot(xm, wg_ref[...])
    h_up = _dot(xm, wu_ref[...])
    act = (h_gate * jax.nn.sigmoid(h_gate) * h_up).astype(BF16)
    o_ref[...] += _dot(act, wd_ref[...])

    @pl.when(f == pl.num_programs(1) - 1)
    def _():
        half_gate = 0.5 * gt_ref[...]
        g = lg_ref[...]
        b = lb_ref[...]

        def body(c, carry):
            rows = pl.ds(pl.multiple_of(c * ROW_CHUNK, ROW_CHUNK), ROW_CHUNK)
            z = ALPHA * x_ref[rows, :] + half_gate * o_ref[rows, :]
            o_ref[rows, :] = _layer_norm_rows(z, g, b)
            return carry

        lax.fori_loop(0, n_chunks, body, 0)


def _ffn(x, mod, layer, sub, k0, cond_of_tile, wg, wu, wd, ln_g, ln_b, tm):
    rows = x.shape[0]
    w_in_spec = pl.BlockSpec((None, None, D_MODEL, FFN_TF), lambda i, f: (layer, sub, 0, f))
    return pl.pallas_call(
        _ffn_kernel,
        out_shape=jax.ShapeDtypeStruct((rows, D_MODEL), F32),
        grid=(rows // tm, D_FF // FFN_TF),
        in_specs=[
            pl.BlockSpec((tm, D_MODEL), lambda i, f: (i, 0)),
            _mod_spec(layer, k0, cond_of_tile),
            _mod_spec(layer, k0 + 1, cond_of_tile),
            _mod_spec(layer, k0 + 2, cond_of_tile),
            w_in_spec,
            w_in_spec,
            pl.BlockSpec((None, None, FFN_TF, D_MODEL), lambda i, f: (layer, sub, f, 0)),
            _ln_spec(layer, 2 * sub),
            _ln_spec(layer, 2 * sub),
        ],
        out_specs=pl.BlockSpec((tm, D_MODEL), lambda i, f: (i, 0)),
        scratch_shapes=[pltpu.VMEM((tm, D_MODEL), BF16)],
        compiler_params=_cparams(("parallel", "arbitrary"), 60),
        name=f"ffn_l{layer}_s{sub}_{rows}",
    )(x, mod, mod, mod, wg, wu, wd, ln_g, ln_b)


def _mix_in_kernel(x_ref, sh_ref, sc_ref, w_ref, cs_ref, gb_ref, v_ref, f_ref):
    xm = (x_ref[...] * (1.0 + sc_ref[...]) + sh_ref[...]).astype(BF16)
    gb_ref[...] = _dot(xm, w_ref[:, 0:CONV_DIM])
    g_c = _dot(xm, w_ref[:, CONV_DIM:2 * CONV_DIM])
    x_in = _dot(xm, w_ref[:, 2 * CONV_DIM:3 * CONV_DIM])
    v_ref[...] = g_c * x_in
    u_f = _dot(xm, w_ref[:, 3 * CONV_DIM:]).astype(BF16)
    cs = cs_ref[...]
    for g in range(FOURIER_GROUPS):
        lanes = slice(g * FOURIER_GROUP_DIM, (g + 1) * FOURIER_GROUP_DIM)
        res = _dot(u_f[:, lanes], cs)
        f_ref[0, :, lanes] = res[:, :FOURIER_GROUP_DIM]
        f_ref[1, :, lanes] = res[:, FOURIER_GROUP_DIM:]


def _mix_in(x, mod, cond_of_tile, w_in, cs, tm):
    rows = x.shape[0]
    n_in = 3 * CONV_DIM + FOURIER_DIM
    return pl.pallas_call(
        _mix_in_kernel,
        out_shape=(jax.ShapeDtypeStruct((rows, CONV_DIM), F32),
                   jax.ShapeDtypeStruct((rows, CONV_DIM), F32),
                   jax.ShapeDtypeStruct((2, rows, FOURIER_DIM), F32)),
        grid=(rows // tm,),
        in_specs=[
            pl.BlockSpec((tm, D_MODEL), lambda i: (i, 0)),
            _mod_spec(0, 3, cond_of_tile),
            _mod_spec(0, 4, cond_of_tile),
            pl.BlockSpec((None, D_MODEL, n_in), lambda i: (0, 0, 0), pipeline_mode=pl.Buffered(1)),
            pl.BlockSpec((FOURIER_GROUP_DIM, 2 * FOURIER_GROUP_DIM), lambda i: (0, 0)),
        ],
        out_specs=(pl.BlockSpec((tm, CONV_DIM), lambda i: (i, 0)),
                   pl.BlockSpec((tm, CONV_DIM), lambda i: (i, 0)),
                   pl.BlockSpec((2, tm, FOURIER_DIM), lambda i: (0, i, 0))),
        compiler_params=_cparams(("parallel",), 56),
        name=f"mix_in_{rows}",
    )(x, mod, mod, w_in, cs)


def _fft1_kernel(m_ref, v_ref, o_ref):
    n1 = v_ref.shape[1]
    ch = v_ref.shape[3]
    m = m_ref[...]
    for j in range(SUBLANES):
        v = jnp.concatenate([v_ref[0, :, j, :], v_ref[1, :, j, :]], axis=0).astype(BF16)
        t = _dot(m, v)
        o_ref[0, :, j, :] = t[:n1]
        o_ref[1, :, j, :] = t[n1:]


def _fft1(v, m1, batch, n1, n2):
    shape = (2, batch, n1, n2 // SUBLANES, SUBLANES, FOURIER_DIM)
    spec = pl.BlockSpec((2, None, n1, None, SUBLANES, FOURIER_DIM), lambda b, j: (0, b, 0, j, 0, 0))
    return pl.pallas_call(
        _fft1_kernel,
        out_shape=jax.ShapeDtypeStruct(shape, F32),
        grid=(batch, n2 // SUBLANES),
        in_specs=[pl.BlockSpec((2 * n1, 2 * n1), lambda b, j: (0, 0)), spec],
        out_specs=spec,
        compiler_params=_cparams(("parallel", "parallel"), 40),
        name="fft_stage1",
    )(m1, v.reshape(shape))


def _fft2_one(tr, ti, twc, tws, f, scale):
    reps = FOURIER_DIM // LANES
    c = jnp.concatenate([twc] * reps, axis=1)
    s = jnp.concatenate([tws] * reps, axis=1)
    pr = (tr * c + ti * s).astype(BF16)
    pi = (ti * c - tr * s).astype(BF16)
    return _dot(f, jnp.concatenate([pr, pi], axis=0)) * scale


def _fft2_kernel(t_ref, twc_ref, tws_ref, f_ref, o_ref, *, scale):
    f = f_ref[...]
    for j in range(SUBLANES):
        o_ref[:, j, :] = _fft2_one(t_ref[0, j], t_ref[1, j], twc_ref[j], tws_ref[j], f, scale)


def _fft2_single_kernel(t_ref, twc_ref, tws_ref, f_ref, o_ref, *, scale):
    o_ref[...] = _fft2_one(t_ref[0], t_ref[1], twc_ref[...], tws_ref[...], f_ref[...], scale)


def _fft2(t, twc, tws, f2, batch, n1, n2):
    t5 = t.reshape(2, batch, n1, n2, FOURIER_DIM)
    scale = 1.0 / math.sqrt(n1 * n2 * FOURIER_GROUP_DIM)
    f_spec = pl.BlockSpec((n2, 2 * n2), lambda b, k: (0, 0))
    if n1 == 1:
        out = pl.pallas_call(
            functools.partial(_fft2_single_kernel, scale=scale),
            out_shape=jax.ShapeDtypeStruct((batch, n2, FOURIER_DIM), F32),
            grid=(batch, 1),
            in_specs=[
                pl.BlockSpec((2, None, None, n2, FOURIER_DIM), lambda b, k: (0, b, 0, 0, 0)),
                pl.BlockSpec((None, n2, LANES), lambda b, k: (0, 0, 0)),
                pl.BlockSpec((None, n2, LANES), lambda b, k: (0, 0, 0)),
                f_spec,
            ],
            out_specs=pl.BlockSpec((None, n2, FOURIER_DIM), lambda b, k: (b, 0, 0)),
            compiler_params=_cparams(("parallel", "parallel"), 32),
            name=f"fft_stage2_{n1}x{n2}",
        )(t5, twc, tws, f2)
    else:
        out = pl.pallas_call(
            functools.partial(_fft2_kernel, scale=scale),
            out_shape=jax.ShapeDtypeStruct((batch, n2, n1 // SUBLANES, SUBLANES, FOURIER_DIM), F32),
            grid=(batch, n1 // SUBLANES),
            in_specs=[
                pl.BlockSpec((2, None, SUBLANES, n2, FOURIER_DIM), lambda b, k: (0, b, k, 0, 0)),
                pl.BlockSpec((SUBLANES, n2, LANES), lambda b, k: (k, 0, 0)),
                pl.BlockSpec((SUBLANES, n2, LANES), lambda b, k: (k, 0, 0)),
                f_spec,
            ],
            out_specs=pl.BlockSpec((None, n2, None, SUBLANES, FOURIER_DIM), lambda b, k: (b, 0, k, 0, 0)),
            compiler_params=_cparams(("parallel", "parallel"), 48),
            name=f"fft_stage2_{n1}x{n2}",
        )(t5, twc, tws, f2)
    return out.reshape(batch * n2 * n1, FOURIER_DIM)


def _mix_out_kernel(x_ref, gt_ref, gb_ref, v_ref, vp_ref, vn_ref, yb_ref, wc_ref, wo_ref,
                    lg_ref, lb_ref, o_ref, *, tiles_per_seq):
    i = pl.program_id(0)
    tm = v_ref.shape[0]
    pos = lax.rem(i, tiles_per_seq)
    v = v_ref[...]
    prev_row = jnp.where(pos == 0, 0.0, vp_ref[7:8, :])
    next_row = jnp.where(pos == tiles_per_seq - 1, 0.0, vn_ref[0:1, :])
    row = lax.broadcasted_iota(jnp.int32, v.shape, 0)
    v_prev = jnp.where(row == 0, prev_row, pltpu.roll(v, 1, 0))
    v_next = jnp.where(row == tm - 1, next_row, pltpu.roll(v, tm - 1, 0))
    conv = wc_ref[0:1, :] * v_prev + wc_ref[1:2, :] * v + wc_ref[2:3, :] * v_next
    y_a = (gb_ref[...] * conv).astype(BF16)
    y = _dot(y_a, wo_ref[0:CONV_DIM, :]) + _dot(yb_ref[...].astype(BF16), wo_ref[CONV_DIM:, :])
    z = ALPHA * x_ref[...] + gt_ref[...] * y
    o_ref[...] = _layer_norm_rows(z, lg_ref[...], lb_ref[...])


def _mix_out(x, mod, cond_of_tile, gb, v, yb, w_conv, w_out, ln_g, ln_b, tm, seq_len):
    rows = x.shape[0]
    sub = 8
    v3 = v.reshape(rows // sub, sub, CONV_DIM)
    step = tm // sub
    last = rows // sub - 1
    return pl.pallas_call(
        functools.partial(_mix_out_kernel, tiles_per_seq=seq_len // tm),
        out_shape=jax.ShapeDtypeStruct((rows, D_MODEL), F32),
        grid=(rows // tm,),
        in_specs=[
            pl.BlockSpec((tm, D_MODEL), lambda i: (i, 0)),
            _mod_spec(0, 5, cond_of_tile),
            pl.BlockSpec((tm, CONV_DIM), lambda i: (i, 0)),
            pl.BlockSpec((tm, CONV_DIM), lambda i: (i, 0)),
            pl.BlockSpec((None, sub, CONV_DIM), lambda i: (jnp.maximum(i * step - 1, 0), 0, 0)),
            pl.BlockSpec((None, sub, CONV_DIM), lambda i: (jnp.minimum((i + 1) * step, last), 0, 0)),
            pl.BlockSpec((tm, FOURIER_DIM), lambda i: (i, 0)),
            pl.BlockSpec((None, 3, CONV_DIM), lambda i: (0, 0, 0)),
            pl.BlockSpec((None, D_MODEL, D_MODEL), lambda i: (0, 0, 0), pipeline_mode=pl.Buffered(1)),
            _ln_spec(0, 1),
            _ln_spec(0, 1),
        ],
        out_specs=pl.BlockSpec((tm, D_MODEL), lambda i: (i, 0)),
        compiler_params=_cparams(("parallel",), 56),
        name=f"mix_out_{rows}",
    )(x, mod, gb, v, v3, v3, yb, w_conv, w_out, ln_g, ln_b)


def _rope_block(blk, cos, sin, first_half):
    rot = jnp.where(first_half, pltpu.roll(blk, LANES - 16, 1), pltpu.roll(blk, 16, 1))
    return blk * cos + rot * sin


def _dup_heads(blk, low_half):
    swapped = pltpu.roll(blk, HEAD_DIM, 1)
    return jnp.where(low_half, blk, swapped), jnp.where(low_half, swapped, blk)


def _qkv_kernel(x_ref, sh_ref, sc_ref, w_ref, wvt_ref, cos_ref, sin_ref, q_ref, k_ref, vt_ref, *, rope):
    tm = x_ref.shape[0]
    xm = (x_ref[...] * (1.0 + sc_ref[...]) + sh_ref[...]).astype(BF16)
    lane = lax.broadcasted_iota(jnp.int32, (tm, LANES), 1)
    first_half = jnp.bitwise_and(lane, 31) < 16
    low_half = lane < HEAD_DIM
    n_q = N_Q_HEADS * HEAD_DIM
    n_kv = N_KV_HEADS * HEAD_DIM
    if rope:
        cos = cos_ref[...]
        sin = sin_ref[...]
    if q_ref is not None:
        q = _dot(xm, w_ref[:, 0:n_q])
        for j in range(n_q // LANES):
            lanes = slice(j * LANES, (j + 1) * LANES)
            blk = q[:, lanes]
            if rope:
                blk = _rope_block(blk, cos, sin, first_half)
            q_ref[:, lanes] = (blk * (LOG2_E * HEAD_DIM ** -0.5)).astype(BF16)
    k = _dot(xm, w_ref[:, n_q:n_q + n_kv])
    v_t = lax.dot_general(wvt_ref[...], xm, (((1,), (1,)), ((), ())),
                          preferred_element_type=F32).astype(BF16)
    for j in range(n_kv // LANES):
        kb = k[:, j * LANES:(j + 1) * LANES]
        if rope:
            kb = _rope_block(kb, cos, sin, first_half)
        k0, k1 = _dup_heads(kb, low_half)
        k_ref[:, 2 * j * LANES:(2 * j + 1) * LANES] = k0.astype(BF16)
        k_ref[:, (2 * j + 1) * LANES:(2 * j + 2) * LANES] = k1.astype(BF16)
    for g in range(N_KV_HEADS):
        head = v_t[g * HEAD_DIM:(g + 1) * HEAD_DIM, :]
        vt_ref[g * LANES:g * LANES + HEAD_DIM, :] = head
        vt_ref[g * LANES + HEAD_DIM:(g + 1) * LANES, :] = head


def _qkv_latent_kernel(x_ref, sh_ref, sc_ref, w_ref, wvt_ref, cos_ref, sin_ref, q_ref, k_ref, vt_ref):
    _qkv_kernel(x_ref, sh_ref, sc_ref, w_ref, wvt_ref, cos_ref, sin_ref, q_ref, k_ref, vt_ref, rope=True)


def _kv_context_kernel(x_ref, sh_ref, sc_ref, w_ref, wvt_ref, k_ref, vt_ref):
    _qkv_kernel(x_ref, sh_ref, sc_ref, w_ref, wvt_ref, None, None, None, k_ref, vt_ref, rope=False)


def _qkv_latent(x, mod, cond_of_tile, w_in, wv_t, cos_t, sin_t, tm, batch, seq_len):
    rows = x.shape[0]
    n_in = (N_Q_HEADS + 2 * N_KV_HEADS) * HEAD_DIM
    tps = seq_len // tm
    dup = N_KV_HEADS * LANES
    return pl.pallas_call(
        _qkv_latent_kernel,
        out_shape=(jax.ShapeDtypeStruct((rows, N_Q_HEADS * HEAD_DIM), BF16),
                   jax.ShapeDtypeStruct((rows, dup), BF16),
                   jax.ShapeDtypeStruct((batch, dup, seq_len), BF16)),
        grid=(rows // tm,),
        in_specs=[
            pl.BlockSpec((tm, D_MODEL), lambda i: (i, 0)),
            _mod_spec(1, 3, cond_of_tile),
            _mod_spec(1, 4, cond_of_tile),
            pl.BlockSpec((None, D_MODEL, n_in), lambda i: (0, 0, 0), pipeline_mode=pl.Buffered(1)),
            pl.BlockSpec((N_KV_HEADS * HEAD_DIM, D_MODEL), lambda i: (0, 0)),
            pl.BlockSpec((tm, LANES), lambda i: (lax.rem(i, tps), 0)),
            pl.BlockSpec((tm, LANES), lambda i: (lax.rem(i, tps), 0)),
        ],
        out_specs=(pl.BlockSpec((tm, N_Q_HEADS * HEAD_DIM), lambda i: (i, 0)),
                   pl.BlockSpec((tm, dup), lambda i: (i, 0)),
                   pl.BlockSpec((None, dup, tm), lambda i: (i // tps, 0, lax.rem(i, tps)))),
        compiler_params=_cparams(("parallel",), 48),
        name="qkv_latent",
    )(x, mod, mod, w_in, wv_t, cos_t, sin_t)


def _kv_context(x, mod, cond_of_tile, w_in, wv_t, tm, batch):
    rows = x.shape[0]
    n_in = (N_Q_HEADS + 2 * N_KV_HEADS) * HEAD_DIM
    dup = N_KV_HEADS * LANES
    return pl.pallas_call(
        _kv_context_kernel,
        out_shape=(jax.ShapeDtypeStruct((rows, dup), BF16),
                   jax.ShapeDtypeStruct((batch, dup, tm), BF16)),
        grid=(rows // tm,),
        in_specs=[
            pl.BlockSpec((tm, D_MODEL), lambda i: (i, 0)),
            _mod_spec(1, 3, cond_of_tile),
            _mod_spec(1, 4, cond_of_tile),
            pl.BlockSpec((None, D_MODEL, n_in), lambda i: (0, 0, 0), pipeline_mode=pl.Buffered(1)),
            pl.BlockSpec((N_KV_HEADS * HEAD_DIM, D_MODEL), lambda i: (0, 0)),
        ],
        out_specs=(pl.BlockSpec((tm, dup), lambda i: (i, 0)),
                   pl.BlockSpec((None, dup, tm), lambda i: (i, 0, 0))),
        compiler_params=_cparams(("parallel",), 40),
        name="kv_context",
    )(x, mod, mod, w_in, wv_t)


def _band_start(i, seq_len):
    return jnp.clip((i - 1) * BLOCK, 0, seq_len - 3 * BLOCK)


def _attn_kernel(sink_ref, q_ref, kl_ref, vlt_ref, kc_ref, vct_ref, bias_ref, o_ref, *, seq_len):
    i = pl.program_id(1)
    band = 3 * BLOCK
    n_ctx = kc_ref.shape[0]
    pairs = KV_REP // 2
    start = pl.multiple_of(_band_start(i, seq_len), BLOCK)
    low_half = lax.broadcasted_iota(jnp.int32, (BLOCK, LANES), 1) < HEAD_DIM
    top_half = lax.broadcasted_iota(jnp.int32, (LANES, BLOCK), 0) < HEAD_DIM

    for g in range(N_KV_HEADS):
        kv = slice(g * LANES, (g + 1) * LANES)
        q_parts = []
        sink_parts = []
        for p in range(pairs):
            blk = q_ref[:, (g * pairs + p) * LANES:(g * pairs + p + 1) * LANES]
            q_parts.append(jnp.where(low_half, blk, jnp.zeros_like(blk)))
            q_parts.append(jnp.where(low_half, jnp.zeros_like(blk), blk))
            for r in range(2):
                sink_parts.append(jnp.full((1, BLOCK), sink_ref[g * KV_REP + 2 * p + r] * LOG2_E, F32))
        q_g = jnp.concatenate(q_parts, axis=0)
        sink = jnp.concatenate(sink_parts, axis=1)
        k_all = jnp.concatenate([kc_ref[:, kv], kl_ref[pl.ds(start, band), kv]], axis=0)
        s = lax.dot_general(k_all, q_g, (((1,), (1,)), ((), ())), preferred_element_type=F32)
        s_ctx = s[:n_ctx]
        s_loc = s[n_ctx:] + bias_ref[...]
        m = jnp.maximum(jnp.maximum(jnp.max(s_ctx, axis=0, keepdims=True),
                                    jnp.max(s_loc, axis=0, keepdims=True)), sink)
        p_ctx = jnp.exp2(s_ctx - m)
        p_loc = jnp.exp2(s_loc - m)
        denom = (jnp.sum(p_ctx, axis=0, keepdims=True) + jnp.sum(p_loc, axis=0, keepdims=True)
                 + jnp.exp2(sink - m))
        probs_t = jnp.concatenate([p_ctx, p_loc], axis=0).astype(BF16)
        vt_all = jnp.concatenate([vct_ref[kv, :], vlt_ref[kv, pl.ds(start, band)]], axis=1)
        o_t = _dot(vt_all, probs_t) / denom
        for p in range(pairs):
            lo = o_t[:, 2 * p * BLOCK:(2 * p + 1) * BLOCK]
            hi = o_t[:, (2 * p + 1) * BLOCK:(2 * p + 2) * BLOCK]
            pair_t = jnp.where(top_half, lo, hi)
            o_ref[:, (g * pairs + p) * LANES:(g * pairs + p + 1) * LANES] = pair_t.T.astype(BF16)


def _window_bias(seq_len):
    band = 3 * BLOCK
    cols = KV_REP * BLOCK
    n_blk = seq_len // BLOCK
    blocks = jnp.array([0, 1, n_blk - 1], jnp.int32)
    delta = (_band_start(blocks, seq_len) - blocks * BLOCK)[:, None, None]
    j = jnp.arange(band, dtype=jnp.int32)[None, :, None]
    a = jnp.bitwise_and(jnp.arange(cols, dtype=jnp.int32), BLOCK - 1)[None, None, :]
    return jnp.where(jnp.abs(j + delta - a) <= WINDOW, 0.0, NEG_INF).astype(F32)


def _attention(sink, q, k_l, vt_l, k_c, vt_c, batch, seq_len, n_ctx):
    n_blk = seq_len // BLOCK
    assert n_blk >= 3
    dup = N_KV_HEADS * LANES
    n_q = N_Q_HEADS * HEAD_DIM
    band = 3 * BLOCK

    def bias_case(b, i):
        return (jnp.where(i == 0, 0, jnp.where(i == n_blk - 1, 2, 1)), 0, 0)

    return pl.pallas_call(
        functools.partial(_attn_kernel, seq_len=seq_len),
        out_shape=jax.ShapeDtypeStruct((batch * seq_len, n_q), BF16),
        grid=(batch, n_blk),
        in_specs=[
            pl.BlockSpec(memory_space=pltpu.SMEM),
            pl.BlockSpec((BLOCK, n_q), lambda b, i: (b * n_blk + i, 0)),
            pl.BlockSpec((None, seq_len, dup), lambda b, i: (b, 0, 0)),
            pl.BlockSpec((None, dup, seq_len), lambda b, i: (b, 0, 0)),
            pl.BlockSpec((None, n_ctx, dup), lambda b, i: (b, 0, 0)),
            pl.BlockSpec((None, dup, n_ctx), lambda b, i: (b, 0, 0)),
            pl.BlockSpec((None, band, KV_REP * BLOCK), bias_case),
        ],
        out_specs=pl.BlockSpec((BLOCK, n_q), lambda b, i: (b * n_blk + i, 0)),
        compiler_params=_cparams(("parallel", "arbitrary"), 56),
        name="window_attention",
    )(sink, q, k_l.reshape(batch, seq_len, dup), vt_l, k_c.reshape(batch, n_ctx, dup), vt_c,
      _window_bias(seq_len))


def _attn_out_kernel(x_ref, gt_ref, a_ref, wo_ref, lg_ref, lb_ref, o_ref):
    y = _dot(a_ref[...], wo_ref[...])
    z = ALPHA * x_ref[...] + gt_ref[...] * y
    o_ref[...] = _layer_norm_rows(z, lg_ref[...], lb_ref[...])


def _attn_out(x, mod, cond_of_tile, a, w_out, ln_g, ln_b, tm):
    rows = x.shape[0]
    return pl.pallas_call(
        _attn_out_kernel,
        out_shape=jax.ShapeDtypeStruct((rows, D_MODEL), F32),
        grid=(rows // tm,),
        in_specs=[
            pl.BlockSpec((tm, D_MODEL), lambda i: (i, 0)),
            _mod_spec(1, 5, cond_of_tile),
            pl.BlockSpec((tm, D_MODEL), lambda i: (i, 0)),
            pl.BlockSpec((None, D_MODEL, D_MODEL), lambda i: (0, 0, 0), pipeline_mode=pl.Buffered(1)),
            _ln_spec(1, 1),
            _ln_spec(1, 1),
        ],
        out_specs=pl.BlockSpec((tm, D_MODEL), lambda i: (i, 0)),
        compiler_params=_cparams(("parallel",), 48),
        name="attn_out",
    )(x, mod, a, w_out, ln_g, ln_b)


def _dft_cos_sin(n):
    k = jnp.arange(n, dtype=jnp.int32)
    ang = ((k[:, None] * k[None, :]) % n).astype(F32) * (2.0 * math.pi / n)
    return jnp.cos(ang), jnp.sin(ang)


def _twiddles(n1, n2):
    k1 = jnp.arange(n1, dtype=jnp.int32)[:, None]
    m2 = jnp.arange(n2, dtype=jnp.int32)[None, :]
    n = n1 * n2
    ang = ((k1 * m2) % n).astype(F32) * (2.0 * math.pi / n)
    shape = (n1, n2, LANES)
    return (jnp.broadcast_to(jnp.cos(ang)[:, :, None], shape),
            jnp.broadcast_to(jnp.sin(ang)[:, :, None], shape))


def _rope_lane_tables(seq_len):
    rows = seq_len // GRID_W
    row = jnp.repeat(jnp.arange(rows, dtype=F32), GRID_W)
    col = jnp.tile(jnp.arange(GRID_W, dtype=F32), rows)
    n_freq = HEAD_DIM // 4
    inv_freq = jnp.power(ROPE_BASE, -jnp.arange(n_freq, dtype=F32) / n_freq)
    ang_r = row[:, None] * inv_freq
    ang_c = col[:, None] * inv_freq
    cos_h = jnp.concatenate([jnp.cos(ang_r), jnp.cos(ang_r), jnp.cos(ang_c), jnp.cos(ang_c)], axis=-1)
    sin_h = jnp.concatenate([-jnp.sin(ang_r), jnp.sin(ang_r), -jnp.sin(ang_c), jnp.sin(ang_c)], axis=-1)
    reps = LANES // HEAD_DIM
    return jnp.tile(cos_h, (1, reps)), jnp.tile(sin_h, (1, reps))


def kernel(x, c, ctx, c_ctx, w_mod, b_mod, ln_g, ln_b, ffn_w_gate, ffn_w_up, ffn_w_down,
           ab_w_in, ab_conv, ab_w_out, attn_w_in, attn_sink, attn_w_out):
    batch, seq_len, _ = x.shape
    n_ctx = ctx.shape[1]
    assert batch == 2 and seq_len % FFN_TM == 0 and seq_len % (FFT_N1 * 8) == 0

    xl = x.reshape(batch * seq_len, D_MODEL)
    xc = ctx.reshape(batch * n_ctx, D_MODEL)
    ctx_rows = batch * n_ctx

    cond = jnp.concatenate([c, c_ctx[None, :], jnp.zeros((N_COND - batch - 1, D_MODEL), F32)], axis=0)
    mod = _modulation(cond, w_mod, b_mod).reshape(DEPTH, N_COND, N_MOD, 1, D_MODEL)
    ln_g4 = ln_g.reshape(DEPTH, 3, 1, D_MODEL)
    ln_b4 = ln_b.reshape(DEPTH, 3, 1, D_MODEL)

    wg = ffn_w_gate.astype(BF16)
    wu = ffn_w_up.astype(BF16)
    wd = ffn_w_down.astype(BF16)
    ab_in = ab_w_in.astype(BF16)
    ab_out = ab_w_out.astype(BF16)
    at_in = attn_w_in.astype(BF16)
    at_out = attn_w_out.astype(BF16)

    def lat_cond(tm):
        tiles = seq_len // tm
        return lambda i: i // tiles

    ctx_cond = lambda i: CTX_COND

    def ffn_both(xl, xc, layer, sub, with_ctx):
        k0 = 6 * sub
        xl = _ffn(xl, mod, layer, sub, k0, lat_cond(FFN_TM), wg, wu, wd, ln_g4, ln_b4, FFN_TM)
        if with_ctx:
            xc = _ffn(xc, mod, layer, sub, k0, ctx_cond, wg, wu, wd, ln_g4, ln_b4, ctx_rows)
        return xl, xc

    xl, xc = ffn_both(xl, xc, 0, 0, True)

    cc, sc_ = _dft_cos_sin(FOURIER_GROUP_DIM)
    cs_chan = jnp.concatenate([cc, -sc_], axis=1).astype(BF16)
    n1, n2 = FFT_N1, seq_len // FFT_N1
    c1, s1 = _dft_cos_sin(n1)
    m1 = jnp.concatenate([jnp.concatenate([c1, s1], axis=1),
                          jnp.concatenate([-s1, c1], axis=1)], axis=0).astype(BF16)
    c2, s2 = _dft_cos_sin(n2)
    f2 = jnp.concatenate([c2, s2], axis=1).astype(BF16)
    twc, tws = _twiddles(n1, n2)
    cn, sn = _dft_cos_sin(n_ctx)
    f2_ctx = jnp.concatenate([cn, sn], axis=1).astype(BF16)
    twc_ctx, tws_ctx = _twiddles(1, n_ctx)

    gb, v, vf = _mix_in(xl, mod, lat_cond(PROJ_TM), ab_in, cs_chan, PROJ_TM)
    t = _fft1(vf, m1, batch, n1, n2)
    yb = _fft2(t, twc, tws, f2, batch, n1, n2)
    xl = _mix_out(xl, mod, lat_cond(PROJ_TM), gb, v, yb, ab_conv, ab_out, ln_g4, ln_b4, PROJ_TM, seq_len)

    gb_c, v_c, vf_c = _mix_in(xc, mod, ctx_cond, ab_in, cs_chan, n_ctx)
    yb_c = _fft2(vf_c, twc_ctx, tws_ctx, f2_ctx, batch, 1, n_ctx)
    xc = _mix_out(xc, mod, ctx_cond, gb_c, v_c, yb_c, ab_conv, ab_out, ln_g4, ln_b4, n_ctx, n_ctx)

    xl, xc = ffn_both(xl, xc, 0, 1, True)

    xl, xc = ffn_both(xl, xc, 1, 0, True)

    cos_t, sin_t = _rope_lane_tables(seq_len)
    n_qk = (N_Q_HEADS + N_KV_HEADS) * HEAD_DIM
    wv_t = at_in[0, :, n_qk:].T
    q, k_l, vt_l = _qkv_latent(xl, mod, lat_cond(PROJ_TM), at_in, wv_t, cos_t, sin_t, PROJ_TM, batch, seq_len)
    k_c, vt_c = _kv_context(xc, mod, ctx_cond, at_in, wv_t, n_ctx, batch)
    att = _attention(attn_sink.reshape(N_Q_HEADS), q, k_l, vt_l, k_c, vt_c, batch, seq_len, n_ctx)
    xl = _attn_out(xl, mod, lat_cond(PROJ_TM), att, at_out, ln_g4, ln_b4, PROJ_TM)

    xl, _ = ffn_both(xl, xc, 1, 1, False)
    return xl.reshape(batch, seq_len, D_MODEL)
```

```python
import functools
import math

import jax
import jax.numpy as jnp
from jax import lax
from jax.experimental import pallas as pl
from jax.experimental.pallas import tpu as pltpu

F32 = jnp.float32
BF16 = jnp.bfloat16

D_MODEL = 2048
GRID_W = 64
N_MOD = 9
D_FF = 5632
CONV_DIM = 1024
FOURIER_DIM = 1024
FOURIER_GROUPS = 8
FOURIER_GROUP_DIM = 128
HEAD_DIM = 64
N_Q_HEADS = 32
N_KV_HEADS = 4
KV_REP = 8
WINDOW = 128
BLOCK = 128
ROPE_BASE = 10000.0
LN_EPS = 1e-5
NEG_INF = -1e30
DEPTH = 2
ALPHA = (2 * DEPTH) ** 0.25
LOG2_E = math.log2(math.e)

LANES = 128
SUBLANES = 8
BF16_SUBLANES = 16
V7X_VMEM_BYTES = 64 * 1024 * 1024
MIB = 1024 * 1024

N_COND = 8
CTX_COND = 2
FFN_TM = 1024
FFN_TF = 512
FFN_SLAB = 256
PROJ_TM = 512
PROJ_SLAB = 256
FFT_N1 = 64


def _cparams(semantics, vmem_mib):
    return pltpu.CompilerParams(dimension_semantics=semantics,
                                vmem_limit_bytes=min(vmem_mib * MIB, V7X_VMEM_BYTES))


def _dot(a, b):
    return jnp.dot(a, b, preferred_element_type=F32)


def _layer_norm_rows(z, g, b):
    mu = jnp.mean(z, axis=-1, keepdims=True)
    zc = z - mu
    var = jnp.mean(zc * zc, axis=-1, keepdims=True)
    return zc * lax.rsqrt(var + LN_EPS) * g + b


def _mod_kernel(c_ref, w_ref, b_ref, o_ref):
    c = c_ref[...]
    a = (c * jax.nn.sigmoid(c)).astype(BF16)
    o_ref[:, 0, :] = _dot(a, w_ref[...].astype(BF16)) + b_ref[...]


def _modulation(cond, w_mod, b_mod):
    n = N_MOD * D_MODEL
    return pl.pallas_call(
        _mod_kernel,
        out_shape=jax.ShapeDtypeStruct((DEPTH, N_COND, N_MOD, 1, D_MODEL), F32),
        grid=(DEPTH, N_MOD),
        in_specs=[
            pl.BlockSpec((N_COND, D_MODEL), lambda l, j: (0, 0)),
            pl.BlockSpec((None, D_MODEL, D_MODEL), lambda l, j: (l, 0, j)),
            pl.BlockSpec((None, 1, D_MODEL), lambda l, j: (l, 0, j)),
        ],
        out_specs=pl.BlockSpec((None, N_COND, None, 1, D_MODEL), lambda l, j: (l, 0, j, 0, 0)),
        compiler_params=_cparams(("parallel", "parallel"), 48),
        name="modulation",
    )(cond, w_mod, b_mod.reshape(DEPTH, 1, n))


def _mod_spec(layer, k, cond_of_tile):
    return pl.BlockSpec((None, None, None, 1, D_MODEL),
                        lambda i, *_: (layer, cond_of_tile(i), k, 0, 0))


def _ln_spec(layer, k):
    return pl.BlockSpec((None, None, 1, D_MODEL), lambda i, *_: (layer, k, 0, 0))


def _ffn_kernel(x_ref, sh_ref, sc_ref, gt_ref, wg_ref, wu_ref, wd_ref, lg_ref, lb_ref, *rest):
    if len(rest) == 2:
        o_ref, xm_ref = rest
        cast_pairs = ()
    else:
        ng_ref, nu_ref, nd_ref, o_ref, cg_ref, cu_ref, cd_ref, xm_ref = rest
        cast_pairs = ((ng_ref, cg_ref), (nu_ref, cu_ref), (nd_ref, cd_ref))
    f = pl.program_id(1)
    last = pl.num_programs(1) - 1
    slabs = [pl.ds(r * FFN_SLAB, FFN_SLAB) for r in range(x_ref.shape[0] // FFN_SLAB)]
    half_gate = 0.5 * gt_ref[...]

    def cast_next_slice():
        for src, dst in cast_pairs:
            dst[...] = src[...].astype(BF16)

    def swiglu_chunk(rows):
        xm = xm_ref[rows, :]
        h_gate = _dot(xm, wg_ref[...])
        h_up = _dot(xm, wu_ref[...])
        act = (h_gate * jax.nn.sigmoid(h_gate) * h_up).astype(BF16)
        return half_gate * _dot(act, wd_ref[...])

    @pl.when(f == 0)
    def _():
        cast_next_slice()
        scale1 = 1.0 + sc_ref[...]
        shift = sh_ref[...]
        for rows in slabs:
            x = x_ref[rows, :]
            xm_ref[rows, :] = (x * scale1 + shift).astype(BF16)
            o_ref[rows, :] = ALPHA * x + swiglu_chunk(rows)

    @pl.when(jnp.logical_and(f > 0, f < last))
    def _():
        cast_next_slice()
        o_ref[...] += swiglu_chunk(slice(None))

    @pl.when(f == last)
    def _():
        cast_next_slice()
        g = lg_ref[...]
        b = lb_ref[...]
        for rows in slabs:
            o_ref[rows, :] = _layer_norm_rows(o_ref[rows, :] + swiglu_chunk(rows), g, b)


def _ffn(x, mod, layer, sub, cond_of_tile, weights, ln_g, ln_b, tm, cast_next=None):
    rows = x.shape[0]
    k0 = 6 * sub
    n_f = D_FF // FFN_TF
    n_steps = (rows // tm) * n_f
    w_in_spec = pl.BlockSpec((D_MODEL, FFN_TF), lambda i, f: (0, f))
    in_specs = [
        pl.BlockSpec((tm, D_MODEL), lambda i, f: (i, 0)),
        _mod_spec(layer, k0, cond_of_tile),
        _mod_spec(layer, k0 + 1, cond_of_tile),
        _mod_spec(layer, k0 + 2, cond_of_tile),
        w_in_spec,
        w_in_spec,
        pl.BlockSpec((FFN_TF, D_MODEL), lambda i, f: (f, 0)),
        _ln_spec(layer, 2 * sub),
        _ln_spec(layer, 2 * sub),
    ]
    out_shape = jax.ShapeDtypeStruct((rows, D_MODEL), F32)
    out_specs = pl.BlockSpec((tm, D_MODEL), lambda i, f: (i, 0))
    args = (x, mod, mod, mod, *weights, ln_g, ln_b)
    semantics = ("parallel", "arbitrary")
    if cast_next is not None:
        n_gate, n_up, n_down, nl, ns = cast_next
        rows_in = BF16_SUBLANES
        blocks_in = D_MODEL // rows_in
        rows_dn = D_FF // n_steps
        assert blocks_in <= n_steps and rows_dn * n_steps == D_FF and rows_dn % BF16_SUBLANES == 0
        step_in = lambda i, f: jnp.minimum(i * n_f + f, blocks_in - 1)
        in_specs += [
            pl.BlockSpec((None, None, rows_in, D_FF), lambda i, f: (nl, ns, step_in(i, f), 0)),
            pl.BlockSpec((None, None, rows_in, D_FF), lambda i, f: (nl, ns, step_in(i, f), 0)),
            pl.BlockSpec((None, None, rows_dn, D_MODEL), lambda i, f: (nl, ns, i * n_f + f, 0)),
        ]
        out_shape = (out_shape, jax.ShapeDtypeStruct((D_MODEL, D_FF), BF16),
                     jax.ShapeDtypeStruct((D_MODEL, D_FF), BF16), jax.ShapeDtypeStruct((D_FF, D_MODEL), BF16))
        out_specs = (out_specs,
                     pl.BlockSpec((rows_in, D_FF), lambda i, f: (step_in(i, f), 0)),
                     pl.BlockSpec((rows_in, D_FF), lambda i, f: (step_in(i, f), 0)),
                     pl.BlockSpec((rows_dn, D_MODEL), lambda i, f: (i * n_f + f, 0)))
        args += (n_gate, n_up, n_down)
        semantics = ("arbitrary", "arbitrary")
    out = pl.pallas_call(
        _ffn_kernel,
        out_shape=out_shape,
        grid=(rows // tm, n_f),
        in_specs=in_specs,
        out_specs=out_specs,
        scratch_shapes=[pltpu.VMEM((tm, D_MODEL), BF16)],
        compiler_params=_cparams(semantics, 60),
        name=f"ffn_l{layer}_s{sub}_{rows}",
    )(*args)
    if cast_next is None:
        return out, None
    return out[0], tuple(out[1:])


def _mix_in_kernel(x_ref, sh_ref, sc_ref, w_ref, cs_ref, gb_ref, v_ref, f_ref):
    xm = (x_ref[...] * (1.0 + sc_ref[...]) + sh_ref[...]).astype(BF16)
    gb_ref[...] = _dot(xm, w_ref[:, 0:CONV_DIM])
    g_c = _dot(xm, w_ref[:, CONV_DIM:2 * CONV_DIM])
    x_in = _dot(xm, w_ref[:, 2 * CONV_DIM:3 * CONV_DIM])
    v_ref[...] = g_c * x_in
    u_f = _dot(xm, w_ref[:, 3 * CONV_DIM:]).astype(BF16)
    cs = cs_ref[...]
    for g in range(FOURIER_GROUPS):
        lanes = slice(g * FOURIER_GROUP_DIM, (g + 1) * FOURIER_GROUP_DIM)
        res = _dot(u_f[:, lanes], cs)
        f_ref[0, :, lanes] = res[:, :FOURIER_GROUP_DIM]
        f_ref[1, :, lanes] = res[:, FOURIER_GROUP_DIM:]


def _mix_in(x, mod, cond_of_tile, w_in, cs, tm):
    rows = x.shape[0]
    n_in = 3 * CONV_DIM + FOURIER_DIM
    return pl.pallas_call(
        _mix_in_kernel,
        out_shape=(jax.ShapeDtypeStruct((rows, CONV_DIM), F32),
                   jax.ShapeDtypeStruct((rows, CONV_DIM), F32),
                   jax.ShapeDtypeStruct((2, rows, FOURIER_DIM), F32)),
        grid=(rows // tm,),
        in_specs=[
            pl.BlockSpec((tm, D_MODEL), lambda i: (i, 0)),
            _mod_spec(0, 3, cond_of_tile),
            _mod_spec(0, 4, cond_of_tile),
            pl.BlockSpec((None, D_MODEL, n_in), lambda i: (0, 0, 0), pipeline_mode=pl.Buffered(1)),
            pl.BlockSpec((FOURIER_GROUP_DIM, 2 * FOURIER_GROUP_DIM), lambda i: (0, 0)),
        ],
        out_specs=(pl.BlockSpec((tm, CONV_DIM), lambda i: (i, 0)),
                   pl.BlockSpec((tm, CONV_DIM), lambda i: (i, 0)),
                   pl.BlockSpec((2, tm, FOURIER_DIM), lambda i: (0, i, 0))),
        compiler_params=_cparams(("parallel",), 56),
        name=f"mix_in_{rows}",
    )(x, mod, mod, w_in, cs)


def _fft1_kernel(m_ref, v_ref, o_ref):
    n1 = v_ref.shape[1]
    ch = v_ref.shape[3]
    m = m_ref[...]
    for j in range(SUBLANES):
        v = jnp.concatenate([v_ref[0, :, j, :], v_ref[1, :, j, :]], axis=0).astype(BF16)
        t = _dot(m, v)
        o_ref[0, :, j, :] = t[:n1]
        o_ref[1, :, j, :] = t[n1:]


def _fft1(v, m1, batch, n1, n2):
    shape = (2, batch, n1, n2 // SUBLANES, SUBLANES, FOURIER_DIM)
    spec = pl.BlockSpec((2, None, n1, None, SUBLANES, FOURIER_DIM), lambda b, j: (0, b, 0, j, 0, 0))
    return pl.pallas_call(
        _fft1_kernel,
        out_shape=jax.ShapeDtypeStruct(shape, F32),
        grid=(batch, n2 // SUBLANES),
        in_specs=[pl.BlockSpec((2 * n1, 2 * n1), lambda b, j: (0, 0)), spec],
        out_specs=spec,
        compiler_params=_cparams(("parallel", "parallel"), 40),
        name="fft_stage1",
    )(m1, v.reshape(shape))


def _fft2_one(tr, ti, twc, tws, f, scale):
    reps = FOURIER_DIM // LANES
    c = jnp.concatenate([twc] * reps, axis=1)
    s = jnp.concatenate([tws] * reps, axis=1)
    pr = (tr * c + ti * s).astype(BF16)
    pi = (ti * c - tr * s).astype(BF16)
    return _dot(f, jnp.concatenate([pr, pi], axis=0)) * scale


def _fft2_kernel(t_ref, twc_ref, tws_ref, f_ref, o_ref, *, scale):
    f = f_ref[...]
    for j in range(SUBLANES):
        o_ref[:, j, :] = _fft2_one(t_ref[0, j], t_ref[1, j], twc_ref[j], tws_ref[j], f, scale)


def _fft2_single_kernel(t_ref, twc_ref, tws_ref, f_ref, o_ref, *, scale):
    o_ref[...] = _fft2_one(t_ref[0], t_ref[1], twc_ref[...], tws_ref[...], f_ref[...], scale)


def _fft2(t, twc, tws, f2, batch, n1, n2):
    t5 = t.reshape(2, batch, n1, n2, FOURIER_DIM)
    scale = 1.0 / math.sqrt(n1 * n2 * FOURIER_GROUP_DIM)
    f_spec = pl.BlockSpec((n2, 2 * n2), lambda b, k: (0, 0))
    if n1 == 1:
        out = pl.pallas_call(
            functools.partial(_fft2_single_kernel, scale=scale),
            out_shape=jax.ShapeDtypeStruct((batch, n2, FOURIER_DIM), F32),
            grid=(batch, 1),
            in_specs=[
                pl.BlockSpec((2, None, None, n2, FOURIER_DIM), lambda b, k: (0, b, 0, 0, 0)),
                pl.BlockSpec((None, n2, LANES), lambda b, k: (0, 0, 0)),
                pl.BlockSpec((None, n2, LANES), lambda b, k: (0, 0, 0)),
                f_spec,
            ],
            out_specs=pl.BlockSpec((None, n2, FOURIER_DIM), lambda b, k: (b, 0, 0)),
            compiler_params=_cparams(("parallel", "parallel"), 32),
            name=f"fft_stage2_{n1}x{n2}",
        )(t5, twc, tws, f2)
    else:
        out = pl.pallas_call(
            functools.partial(_fft2_kernel, scale=scale),
            out_shape=jax.ShapeDtypeStruct((batch, n2, n1 // SUBLANES, SUBLANES, FOURIER_DIM), F32),
            grid=(batch, n1 // SUBLANES),
            in_specs=[
                pl.BlockSpec((2, None, SUBLANES, n2, FOURIER_DIM), lambda b, k: (0, b, k, 0, 0)),
                pl.BlockSpec((SUBLANES, n2, LANES), lambda b, k: (k, 0, 0)),
                pl.BlockSpec((SUBLANES, n2, LANES), lambda b, k: (k, 0, 0)),
                f_spec,
            ],
            out_specs=pl.BlockSpec((None, n2, None, SUBLANES, FOURIER_DIM), lambda b, k: (b, 0, k, 0, 0)),
            compiler_params=_cparams(("parallel", "parallel"), 48),
            name=f"fft_stage2_{n1}x{n2}",
        )(t5, twc, tws, f2)
    return out.reshape(batch * n2 * n1, FOURIER_DIM)


def _mix_out_kernel(x_ref, gt_ref, gb_ref, v_ref, vp_ref, vn_ref, yb_ref, wc_ref, wo_ref,
                    lg_ref, lb_ref, o_ref, *, tiles_per_seq):
    i = pl.program_id(0)
    tm = v_ref.shape[0]
    pos = lax.rem(i, tiles_per_seq)
    v = v_ref[...]
    prev_row = jnp.where(pos == 0, 0.0, vp_ref[7:8, :])
    next_row = jnp.where(pos == tiles_per_seq - 1, 0.0, vn_ref[0:1, :])
    row = lax.broadcasted_iota(jnp.int32, v.shape, 0)
    v_prev = jnp.where(row == 0, prev_row, pltpu.roll(v, 1, 0))
    v_next = jnp.where(row == tm - 1, next_row, pltpu.roll(v, tm - 1, 0))
    conv = wc_ref[0:1, :] * v_prev + wc_ref[1:2, :] * v + wc_ref[2:3, :] * v_next
    y_a = (gb_ref[...] * conv).astype(BF16)
    for r in range(tm // PROJ_SLAB):
        rows = pl.ds(r * PROJ_SLAB, PROJ_SLAB)
        y = (_dot(y_a[r * PROJ_SLAB:(r + 1) * PROJ_SLAB], wo_ref[0:CONV_DIM, :])
             + _dot(yb_ref[rows, :].astype(BF16), wo_ref[CONV_DIM:, :]))
        z = ALPHA * x_ref[rows, :] + gt_ref[...] * y
        o_ref[rows, :] = _layer_norm_rows(z, lg_ref[...], lb_ref[...])


def _mix_out(x, mod, cond_of_tile, gb, v, yb, w_conv, w_out, ln_g, ln_b, tm, seq_len):
    rows = x.shape[0]
    sub = 8
    v3 = v.reshape(rows // sub, sub, CONV_DIM)
    step = tm // sub
    last = rows // sub - 1
    return pl.pallas_call(
        functools.partial(_mix_out_kernel, tiles_per_seq=seq_len // tm),
        out_shape=jax.ShapeDtypeStruct((rows, D_MODEL), F32),
        grid=(rows // tm,),
        in_specs=[
            pl.BlockSpec((tm, D_MODEL), lambda i: (i, 0)),
            _mod_spec(0, 5, cond_of_tile),
            pl.BlockSpec((tm, CONV_DIM), lambda i: (i, 0)),
            pl.BlockSpec((tm, CONV_DIM), lambda i: (i, 0)),
            pl.BlockSpec((None, sub, CONV_DIM), lambda i: (jnp.maximum(i * step - 1, 0), 0, 0)),
            pl.BlockSpec((None, sub, CONV_DIM), lambda i: (jnp.minimum((i + 1) * step, last), 0, 0)),
            pl.BlockSpec((tm, FOURIER_DIM), lambda i: (i, 0)),
            pl.BlockSpec((None, 3, CONV_DIM), lambda i: (0, 0, 0)),
            pl.BlockSpec((None, D_MODEL, D_MODEL), lambda i: (0, 0, 0), pipeline_mode=pl.Buffered(1)),
            _ln_spec(0, 1),
            _ln_spec(0, 1),
        ],
        out_specs=pl.BlockSpec((tm, D_MODEL), lambda i: (i, 0)),
        compiler_params=_cparams(("parallel",), 56),
        name=f"mix_out_{rows}",
    )(x, mod, gb, v, v3, v3, yb, w_conv, w_out, ln_g, ln_b)


def _rope_block(blk, cos, sin, first_half):
    rot = jnp.where(first_half, pltpu.roll(blk, LANES - 16, 1), pltpu.roll(blk, 16, 1))
    return blk * cos + rot * sin


def _dup_heads(blk, low_half):
    swapped = pltpu.roll(blk, HEAD_DIM, 1)
    return jnp.where(low_half, blk, swapped), jnp.where(low_half, swapped, blk)


def _qkv_kernel(x_ref, sh_ref, sc_ref, w_ref, wvt_ref, cos_ref, sin_ref, q_ref, k_ref, vt_ref, *, rope):
    tm = x_ref.shape[0]
    xm = (x_ref[...] * (1.0 + sc_ref[...]) + sh_ref[...]).astype(BF16)
    lane = lax.broadcasted_iota(jnp.int32, (tm, LANES), 1)
    first_half = jnp.bitwise_and(lane, 31) < 16
    low_half = lane < HEAD_DIM
    n_q = N_Q_HEADS * HEAD_DIM
    n_kv = N_KV_HEADS * HEAD_DIM
    if rope:
        cos = cos_ref[...]
        sin = sin_ref[...]
    if q_ref is not None:
        q = _dot(xm, w_ref[:, 0:n_q])
        for j in range(n_q // LANES):
            lanes = slice(j * LANES, (j + 1) * LANES)
            blk = q[:, lanes]
            if rope:
                blk = _rope_block(blk, cos, sin, first_half)
            q_ref[:, lanes] = (blk * (LOG2_E * HEAD_DIM ** -0.5)).astype(BF16)
    k = _dot(xm, w_ref[:, n_q:n_q + n_kv])
    v_t = lax.dot_general(wvt_ref[...], xm, (((1,), (1,)), ((), ())),
                          preferred_element_type=F32).astype(BF16)
    for j in range(n_kv // LANES):
        kb = k[:, j * LANES:(j + 1) * LANES]
        if rope:
            kb = _rope_block(kb, cos, sin, first_half)
        k0, k1 = _dup_heads(kb, low_half)
        k_ref[:, 2 * j * LANES:(2 * j + 1) * LANES] = k0.astype(BF16)
        k_ref[:, (2 * j + 1) * LANES:(2 * j + 2) * LANES] = k1.astype(BF16)
    for g in range(N_KV_HEADS):
        head = v_t[g * HEAD_DIM:(g + 1) * HEAD_DIM, :]
        vt_ref[g * LANES:g * LANES + HEAD_DIM, :] = head
        vt_ref[g * LANES + HEAD_DIM:(g + 1) * LANES, :] = head


def _qkv_latent_kernel(x_ref, sh_ref, sc_ref, w_ref, wvt_ref, cos_ref, sin_ref, q_ref, k_ref, vt_ref):
    _qkv_kernel(x_ref, sh_ref, sc_ref, w_ref, wvt_ref, cos_ref, sin_ref, q_ref, k_ref, vt_ref, rope=True)


def _kv_context_kernel(x_ref, sh_ref, sc_ref, w_ref, wvt_ref, k_ref, vt_ref):
    _qkv_kernel(x_ref, sh_ref, sc_ref, w_ref, wvt_ref, None, None, None, k_ref, vt_ref, rope=False)


def _qkv_latent(x, mod, cond_of_tile, w_in, wv_t, cos_t, sin_t, tm, batch, seq_len):
    rows = x.shape[0]
    n_in = (N_Q_HEADS + 2 * N_KV_HEADS) * HEAD_DIM
    tps = seq_len // tm
    dup = N_KV_HEADS * LANES
    return pl.pallas_call(
        _qkv_latent_kernel,
        out_shape=(jax.ShapeDtypeStruct((rows, N_Q_HEADS * HEAD_DIM), BF16),
                   jax.ShapeDtypeStruct((rows, dup), BF16),
                   jax.ShapeDtypeStruct((batch, dup, seq_len), BF16)),
        grid=(rows // tm,),
        in_specs=[
            pl.BlockSpec((tm, D_MODEL), lambda i: (i, 0)),
            _mod_spec(1, 3, cond_of_tile),
            _mod_spec(1, 4, cond_of_tile),
            pl.BlockSpec((None, D_MODEL, n_in), lambda i: (0, 0, 0), pipeline_mode=pl.Buffered(1)),
            pl.BlockSpec((N_KV_HEADS * HEAD_DIM, D_MODEL), lambda i: (0, 0)),
            pl.BlockSpec((tm, LANES), lambda i: (lax.rem(i, tps), 0)),
            pl.BlockSpec((tm, LANES), lambda i: (lax.rem(i, tps), 0)),
        ],
        out_specs=(pl.BlockSpec((tm, N_Q_HEADS * HEAD_DIM), lambda i: (i, 0)),
                   pl.BlockSpec((tm, dup), lambda i: (i, 0)),
                   pl.BlockSpec((None, dup, tm), lambda i: (i // tps, 0, lax.rem(i, tps)))),
        compiler_params=_cparams(("parallel",), 48),
        name="qkv_latent",
    )(x, mod, mod, w_in, wv_t, cos_t, sin_t)


def _kv_context(x, mod, cond_of_tile, w_in, wv_t, tm, batch):
    rows = x.shape[0]
    n_in = (N_Q_HEADS + 2 * N_KV_HEADS) * HEAD_DIM
    dup = N_KV_HEADS * LANES
    return pl.pallas_call(
        _kv_context_kernel,
        out_shape=(jax.ShapeDtypeStruct((rows, dup), BF16),
                   jax.ShapeDtypeStruct((batch, dup, tm), BF16)),
        grid=(rows // tm,),
        in_specs=[
            pl.BlockSpec((tm, D_MODEL), lambda i: (i, 0)),
            _mod_spec(1, 3, cond_of_tile),
            _mod_spec(1, 4, cond_of_tile),
            pl.BlockSpec((None, D_MODEL, n_in), lambda i: (0, 0, 0), pipeline_mode=pl.Buffered(1)),
            pl.BlockSpec((N_KV_HEADS * HEAD_DIM, D_MODEL), lambda i: (0, 0)),
        ],
        out_specs=(pl.BlockSpec((tm, dup), lambda i: (i, 0)),
                   pl.BlockSpec((None, dup, tm), lambda i: (i, 0, 0))),
        compiler_params=_cparams(("parallel",), 40),
        name="kv_context",
    )(x, mod, mod, w_in, wv_t)


def _band_start(i, seq_len):
    return jnp.clip((i - 1) * BLOCK, 0, seq_len - 3 * BLOCK)


def _attn_kernel(sink_ref, q_ref, kl_ref, vlt_ref, kc_ref, vct_ref, bias_ref, o_ref, *, seq_len):
    i = pl.program_id(1)
    band = 3 * BLOCK
    n_ctx = kc_ref.shape[0]
    pairs = KV_REP // 2
    start = pl.multiple_of(_band_start(i, seq_len), BLOCK)
    low_half = lax.broadcasted_iota(jnp.int32, (BLOCK, LANES), 1) < HEAD_DIM
    top_half = lax.broadcasted_iota(jnp.int32, (LANES, BLOCK), 0) < HEAD_DIM

    for g in range(N_KV_HEADS):
        kv = slice(g * LANES, (g + 1) * LANES)
        q_parts = []
        sink_parts = []
        for p in range(pairs):
            blk = q_ref[:, (g * pairs + p) * LANES:(g * pairs + p + 1) * LANES]
            q_parts.append(jnp.where(low_half, blk, jnp.zeros_like(blk)))
            q_parts.append(jnp.where(low_half, jnp.zeros_like(blk), blk))
            for r in range(2):
                sink_parts.append(jnp.full((1, BLOCK), sink_ref[g * KV_REP + 2 * p + r] * LOG2_E, F32))
        q_g = jnp.concatenate(q_parts, axis=0)
        sink = jnp.concatenate(sink_parts, axis=1)
        k_all = jnp.concatenate([kc_ref[:, kv], kl_ref[pl.ds(start, band), kv]], axis=0)
        s = lax.dot_general(k_all, q_g, (((1,), (1,)), ((), ())), preferred_element_type=F32)
        s_ctx = s[:n_ctx]
        s_loc = s[n_ctx:] + bias_ref[...]
        m = jnp.maximum(jnp.maximum(jnp.max(s_ctx, axis=0, keepdims=True),
                                    jnp.max(s_loc, axis=0, keepdims=True)), sink)
        p_ctx = jnp.exp2(s_ctx - m)
        p_loc = jnp.exp2(s_loc - m)
        denom = (jnp.sum(p_ctx, axis=0, keepdims=True) + jnp.sum(p_loc, axis=0, keepdims=True)
                 + jnp.exp2(sink - m))
        probs_t = jnp.concatenate([p_ctx, p_loc], axis=0).astype(BF16)
        vt_all = jnp.concatenate([vct_ref[kv, :], vlt_ref[kv, pl.ds(start, band)]], axis=1)
        o_t = _dot(vt_all, probs_t) / denom
        for p in range(pairs):
            lo = o_t[:, 2 * p * BLOCK:(2 * p + 1) * BLOCK]
            hi = o_t[:, (2 * p + 1) * BLOCK:(2 * p + 2) * BLOCK]
            pair_t = jnp.where(top_half, lo, hi)
            o_ref[:, (g * pairs + p) * LANES:(g * pairs + p + 1) * LANES] = pair_t.T.astype(BF16)


def _window_bias(seq_len):
    band = 3 * BLOCK
    cols = KV_REP * BLOCK
    n_blk = seq_len // BLOCK
    blocks = jnp.array([0, 1, n_blk - 1], jnp.int32)
    delta = (_band_start(blocks, seq_len) - blocks * BLOCK)[:, None, None]
    j = jnp.arange(band, dtype=jnp.int32)[None, :, None]
    a = jnp.bitwise_and(jnp.arange(cols, dtype=jnp.int32), BLOCK - 1)[None, None, :]
    return jnp.where(jnp.abs(j + delta - a) <= WINDOW, 0.0, NEG_INF).astype(F32)


def _attention(sink, q, k_l, vt_l, k_c, vt_c, batch, seq_len, n_ctx):
    n_blk = seq_len // BLOCK
    assert n_blk >= 3
    dup = N_KV_HEADS * LANES
    n_q = N_Q_HEADS * HEAD_DIM
    band = 3 * BLOCK

    def bias_case(b, i):
        return (jnp.where(i == 0, 0, jnp.where(i == n_blk - 1, 2, 1)), 0, 0)

    return pl.pallas_call(
        functools.partial(_attn_kernel, seq_len=seq_len),
        out_shape=jax.ShapeDtypeStruct((batch * seq_len, n_q), BF16),
        grid=(batch, n_blk),
        in_specs=[
            pl.BlockSpec(memory_space=pltpu.SMEM),
            pl.BlockSpec((BLOCK, n_q), lambda b, i: (b * n_blk + i, 0)),
            pl.BlockSpec((None, seq_len, dup), lambda b, i: (b, 0, 0)),
            pl.BlockSpec((None, dup, seq_len), lambda b, i: (b, 0, 0)),
            pl.BlockSpec((None, n_ctx, dup), lambda b, i: (b, 0, 0)),
            pl.BlockSpec((None, dup, n_ctx), lambda b, i: (b, 0, 0)),
            pl.BlockSpec((None, band, KV_REP * BLOCK), bias_case),
        ],
        out_specs=pl.BlockSpec((BLOCK, n_q), lambda b, i: (b * n_blk + i, 0)),
        compiler_params=_cparams(("parallel", "arbitrary"), 56),
        name="window_attention",
    )(sink, q, k_l.reshape(batch, seq_len, dup), vt_l, k_c.reshape(batch, n_ctx, dup), vt_c,
      _window_bias(seq_len))


def _attn_out_kernel(x_ref, gt_ref, a_ref, wo_ref, lg_ref, lb_ref, o_ref):
    for r in range(x_ref.shape[0] // PROJ_SLAB):
        rows = pl.ds(r * PROJ_SLAB, PROJ_SLAB)
        y = _dot(a_ref[rows, :], wo_ref[...])
        z = ALPHA * x_ref[rows, :] + gt_ref[...] * y
        o_ref[rows, :] = _layer_norm_rows(z, lg_ref[...], lb_ref[...])


def _attn_out(x, mod, cond_of_tile, a, w_out, ln_g, ln_b, tm):
    rows = x.shape[0]
    return pl.pallas_call(
        _attn_out_kernel,
        out_shape=jax.ShapeDtypeStruct((rows, D_MODEL), F32),
        grid=(rows // tm,),
        in_specs=[
            pl.BlockSpec((tm, D_MODEL), lambda i: (i, 0)),
            _mod_spec(1, 5, cond_of_tile),
            pl.BlockSpec((tm, D_MODEL), lambda i: (i, 0)),
            pl.BlockSpec((None, D_MODEL, D_MODEL), lambda i: (0, 0, 0), pipeline_mode=pl.Buffered(1)),
            _ln_spec(1, 1),
            _ln_spec(1, 1),
        ],
        out_specs=pl.BlockSpec((tm, D_MODEL), lambda i: (i, 0)),
        compiler_params=_cparams(("parallel",), 48),
        name="attn_out",
    )(x, mod, a, w_out, ln_g, ln_b)


def _dft_cos_sin(n):
    k = jnp.arange(n, dtype=jnp.int32)
    ang = ((k[:, None] * k[None, :]) % n).astype(F32) * (2.0 * math.pi / n)
    return jnp.cos(ang), jnp.sin(ang)


def _twiddles(n1, n2):
    k1 = jnp.arange(n1, dtype=jnp.int32)[:, None]
    m2 = jnp.arange(n2, dtype=jnp.int32)[None, :]
    n = n1 * n2
    ang = ((k1 * m2) % n).astype(F32) * (2.0 * math.pi / n)
    shape = (n1, n2, LANES)
    return (jnp.broadcast_to(jnp.cos(ang)[:, :, None], shape),
            jnp.broadcast_to(jnp.sin(ang)[:, :, None], shape))


def _rope_lane_tables(seq_len):
    rows = seq_len // GRID_W
    row = jnp.repeat(jnp.arange(rows, dtype=F32), GRID_W)
    col = jnp.tile(jnp.arange(GRID_W, dtype=F32), rows)
    n_freq = HEAD_DIM // 4
    inv_freq = jnp.power(ROPE_BASE, -jnp.arange(n_freq, dtype=F32) / n_freq)
    ang_r = row[:, None] * inv_freq
    ang_c = col[:, None] * inv_freq
    cos_h = jnp.concatenate([jnp.cos(ang_r), jnp.cos(ang_r), jnp.cos(ang_c), jnp.cos(ang_c)], axis=-1)
    sin_h = jnp.concatenate([-jnp.sin(ang_r), jnp.sin(ang_r), -jnp.sin(ang_c), jnp.sin(ang_c)], axis=-1)
    reps = LANES // HEAD_DIM
    return jnp.tile(cos_h, (1, reps)), jnp.tile(sin_h, (1, reps))


def kernel(x, c, ctx, c_ctx, w_mod, b_mod, ln_g, ln_b, ffn_w_gate, ffn_w_up, ffn_w_down,
           ab_w_in, ab_conv, ab_w_out, attn_w_in, attn_sink, attn_w_out):
    batch, seq_len, _ = x.shape
    n_ctx = ctx.shape[1]
    assert batch == 2 and seq_len % FFN_TM == 0 and seq_len % (FFT_N1 * 8) == 0

    xl = x.reshape(batch * seq_len, D_MODEL)
    xc = ctx.reshape(batch * n_ctx, D_MODEL)
    ctx_rows = batch * n_ctx

    cond = jnp.concatenate([c, c_ctx[None, :], jnp.zeros((N_COND - batch - 1, D_MODEL), F32)], axis=0)
    mod = _modulation(cond, w_mod, b_mod)
    ln_g4 = ln_g.reshape(DEPTH, 3, 1, D_MODEL)
    ln_b4 = ln_b.reshape(DEPTH, 3, 1, D_MODEL)

    ffn_f32 = (ffn_w_gate, ffn_w_up, ffn_w_down)
    first_weights = tuple(w[0, 0].astype(BF16) for w in ffn_f32)
    ab_in = ab_w_in.astype(BF16)
    ab_out = ab_w_out.astype(BF16)
    at_in = attn_w_in.astype(BF16)
    at_out = attn_w_out.astype(BF16)

    def lat_cond(tm):
        tiles = seq_len // tm
        return lambda i: i // tiles

    ctx_cond = lambda i: CTX_COND

    def ffn_both(xl, xc, layer, sub, weights, with_ctx):
        nxt = (layer, sub + 1) if sub == 0 else (layer + 1, 0)
        cast_next = (*ffn_f32, *nxt) if nxt[0] < DEPTH else None
        xl, next_weights = _ffn(xl, mod, layer, sub, lat_cond(FFN_TM), weights, ln_g4, ln_b4, FFN_TM, cast_next)
        if with_ctx:
            xc, _ = _ffn(xc, mod, layer, sub, ctx_cond, weights, ln_g4, ln_b4, ctx_rows)
        return xl, xc, next_weights

    xl, xc, weights = ffn_both(xl, xc, 0, 0, first_weights, True)

    cc, sc_ = _dft_cos_sin(FOURIER_GROUP_DIM)
    cs_chan = jnp.concatenate([cc, -sc_], axis=1).astype(BF16)
    n1, n2 = FFT_N1, seq_len // FFT_N1
    c1, s1 = _dft_cos_sin(n1)
    m1 = jnp.concatenate([jnp.concatenate([c1, s1], axis=1),
                          jnp.concatenate([-s1, c1], axis=1)], axis=0).astype(BF16)
    c2, s2 = _dft_cos_sin(n2)
    f2 = jnp.concatenate([c2, s2], axis=1).astype(BF16)
    twc, tws = _twiddles(n1, n2)
    cn, sn = _dft_cos_sin(n_ctx)
    f2_ctx = jnp.concatenate([cn, sn], axis=1).astype(BF16)
    twc_ctx, tws_ctx = _twiddles(1, n_ctx)

    gb, v, vf = _mix_in(xl, mod, lat_cond(PROJ_TM), ab_in, cs_chan, PROJ_TM)
    t = _fft1(vf, m1, batch, n1, n2)
    yb = _fft2(t, twc, tws, f2, batch, n1, n2)
    xl = _mix_out(xl, mod, lat_cond(PROJ_TM), gb, v, yb, ab_conv, ab_out, ln_g4, ln_b4, PROJ_TM, seq_len)

    gb_c, v_c, vf_c = _mix_in(xc, mod, ctx_cond, ab_in, cs_chan, n_ctx)
    yb_c = _fft2(vf_c, twc_ctx, tws_ctx, f2_ctx, batch, 1, n_ctx)
    xc = _mix_out(xc, mod, ctx_cond, gb_c, v_c, yb_c, ab_conv, ab_out, ln_g4, ln_b4, n_ctx, n_ctx)

    xl, xc, weights = ffn_both(xl, xc, 0, 1, weights, True)

    xl, xc, weights = ffn_both(xl, xc, 1, 0, weights, True)

    cos_t, sin_t = _rope_lane_tables(seq_len)
    n_qk = (N_Q_HEADS + N_KV_HEADS) * HEAD_DIM
    wv_t = at_in[0, :, n_qk:].T
    q, k_l, vt_l = _qkv_latent(xl, mod, lat_cond(PROJ_TM), at_in, wv_t, cos_t, sin_t, PROJ_TM, batch, seq_len)
    k_c, vt_c = _kv_context(xc, mod, ctx_cond, at_in, wv_t, n_ctx, batch)
    att = _attention(attn_sink.reshape(N_Q_HEADS), q, k_l, vt_l, k_c, vt_c, batch, seq_len, n_ctx)
    xl = _attn_out(xl, mod, lat_cond(PROJ_TM), att, at_out, ln_g4, ln_b4, PROJ_TM)

    xl, _, _ = ffn_both(xl, xc, 1, 1, weights, False)
    return xl.reshape(batch, seq_len, D_MODEL)
```

```python
import functools
import math

import jax
import jax.numpy as jnp
from jax import lax
from jax.experimental import pallas as pl
from jax.experimental.pallas import tpu as pltpu

F32 = jnp.float32
BF16 = jnp.bfloat16

D_MODEL = 2048
GRID_W = 64
N_MOD = 9
D_FF = 5632
CONV_DIM = 1024
FOURIER_DIM = 1024
FOURIER_GROUPS = 8
FOURIER_GROUP_DIM = 128
HEAD_DIM = 64
N_Q_HEADS = 32
N_KV_HEADS = 4
KV_REP = 8
WINDOW = 128
BLOCK = 128
ROPE_BASE = 10000.0
LN_EPS = 1e-5
NEG_INF = -1e30
DEPTH = 2
ALPHA = (2 * DEPTH) ** 0.25
LOG2_E = math.log2(math.e)

LANES = 128
SUBLANES = 8
BF16_SUBLANES = 16
V7X_VMEM_BYTES = 64 * 1024 * 1024
MIB = 1024 * 1024

N_COND = 8
CTX_COND = 2
FFN_TM = 1024
FFN_TF = 512
FFN_SLAB = 256
PROJ_TM = 512
PROJ_SLAB = 256
ATTN_AHEAD = 2
FFT_N1 = 64


def _cparams(semantics, vmem_mib):
    return pltpu.CompilerParams(dimension_semantics=semantics,
                                vmem_limit_bytes=min(vmem_mib * MIB, V7X_VMEM_BYTES))


def _dot(a, b):
    return jnp.dot(a, b, preferred_element_type=F32)


def _layer_norm_rows(z, g, b):
    mu = jnp.mean(z, axis=-1, keepdims=True)
    zc = z - mu
    var = jnp.mean(zc * zc, axis=-1, keepdims=True)
    return zc * lax.rsqrt(var + LN_EPS) * g + b


def _mod_kernel(c_ref, w_ref, b_ref, o_ref):
    c = c_ref[...]
    a = (c * jax.nn.sigmoid(c)).astype(BF16)
    o_ref[:, 0, :] = _dot(a, w_ref[...].astype(BF16)) + b_ref[...]


def _modulation(cond, w_mod, b_mod):
    n = N_MOD * D_MODEL
    return pl.pallas_call(
        _mod_kernel,
        out_shape=jax.ShapeDtypeStruct((DEPTH, N_COND, N_MOD, 1, D_MODEL), F32),
        grid=(DEPTH, N_MOD),
        in_specs=[
            pl.BlockSpec((N_COND, D_MODEL), lambda l, j: (0, 0)),
            pl.BlockSpec((None, D_MODEL, D_MODEL), lambda l, j: (l, 0, j)),
            pl.BlockSpec((None, 1, D_MODEL), lambda l, j: (l, 0, j)),
        ],
        out_specs=pl.BlockSpec((None, N_COND, None, 1, D_MODEL), lambda l, j: (l, 0, j, 0, 0)),
        compiler_params=_cparams(("parallel", "parallel"), 48),
        name="modulation",
    )(cond, w_mod, b_mod.reshape(DEPTH, 1, n))


def _mod_spec(layer, k, cond_of_tile):
    return pl.BlockSpec((None, None, None, 1, D_MODEL),
                        lambda i, *_: (layer, cond_of_tile(i), k, 0, 0))


def _ln_spec(layer, k):
    return pl.BlockSpec((None, None, 1, D_MODEL), lambda i, *_: (layer, k, 0, 0))


def _ffn_kernel(x_ref, sh_ref, sc_ref, gt_ref, wg_ref, wu_ref, wd_ref, lg_ref, lb_ref, *rest):
    n_cast = (len(rest) - 2) // 2
    o_ref, xm_ref = rest[n_cast], rest[-1]
    cast_pairs = tuple(zip(rest[:n_cast], rest[n_cast + 1:-1]))
    f = pl.program_id(1)
    last = pl.num_programs(1) - 1
    slabs = [pl.ds(r * FFN_SLAB, FFN_SLAB) for r in range(x_ref.shape[0] // FFN_SLAB)]
    half_gate = 0.5 * gt_ref[...]

    def cast_next_slice():
        for src, dst in cast_pairs:
            dst[...] = src[...].astype(BF16)

    def swiglu_chunk(rows):
        xm = xm_ref[rows, :]
        h_gate = _dot(xm, wg_ref[...])
        h_up = _dot(xm, wu_ref[...])
        act = (h_gate * jax.nn.sigmoid(h_gate) * h_up).astype(BF16)
        return half_gate * _dot(act, wd_ref[...])

    @pl.when(f == 0)
    def _():
        cast_next_slice()
        scale1 = 1.0 + sc_ref[...]
        shift = sh_ref[...]
        for rows in slabs:
            x = x_ref[rows, :]
            xm_ref[rows, :] = (x * scale1 + shift).astype(BF16)
            o_ref[rows, :] = ALPHA * x + swiglu_chunk(rows)

    @pl.when(jnp.logical_and(f > 0, f < last))
    def _():
        cast_next_slice()
        o_ref[...] += swiglu_chunk(slice(None))

    @pl.when(f == last)
    def _():
        cast_next_slice()
        g = lg_ref[...]
        b = lb_ref[...]
        for rows in slabs:
            o_ref[rows, :] = _layer_norm_rows(o_ref[rows, :] + swiglu_chunk(rows), g, b)


def _cast_specs(src, lead, n_f, n_steps):
    n_rows, n_cols = src.shape[-2:]
    rows = BF16_SUBLANES
    while n_rows % rows or n_rows // rows > n_steps:
        rows += BF16_SUBLANES
    last_block = n_rows // rows - 1
    step = lambda i, f: jnp.minimum(i * n_f + f, last_block)
    in_spec = pl.BlockSpec((None,) * len(lead) + (rows, n_cols), lambda i, f: (*lead, step(i, f), 0))
    out_spec = pl.BlockSpec((rows, n_cols), lambda i, f: (step(i, f), 0))
    return in_spec, out_spec, jax.ShapeDtypeStruct((n_rows, n_cols), BF16)


def _ffn(x, mod, layer, sub, cond_of_tile, weights, ln_g, ln_b, tm, cast_jobs=()):
    rows = x.shape[0]
    k0 = 6 * sub
    n_f = D_FF // FFN_TF
    n_steps = (rows // tm) * n_f
    w_in_spec = pl.BlockSpec((D_MODEL, FFN_TF), lambda i, f: (0, f))
    in_specs = [
        pl.BlockSpec((tm, D_MODEL), lambda i, f: (i, 0)),
        _mod_spec(layer, k0, cond_of_tile),
        _mod_spec(layer, k0 + 1, cond_of_tile),
        _mod_spec(layer, k0 + 2, cond_of_tile),
        w_in_spec,
        w_in_spec,
        pl.BlockSpec((FFN_TF, D_MODEL), lambda i, f: (f, 0)),
        _ln_spec(layer, 2 * sub),
        _ln_spec(layer, 2 * sub),
    ]
    out_shape = jax.ShapeDtypeStruct((rows, D_MODEL), F32)
    out_specs = pl.BlockSpec((tm, D_MODEL), lambda i, f: (i, 0))
    args = (x, mod, mod, mod, *weights, ln_g, ln_b)
    semantics = ("parallel", "arbitrary")
    if cast_jobs:
        specs = [_cast_specs(src, lead, n_f, n_steps) for src, lead in cast_jobs]
        in_specs += [s[0] for s in specs]
        out_specs = (out_specs, *[s[1] for s in specs])
        out_shape = (out_shape, *[s[2] for s in specs])
        args += tuple(src for src, _ in cast_jobs)
        semantics = ("arbitrary", "arbitrary")
    out = pl.pallas_call(
        _ffn_kernel,
        out_shape=out_shape,
        grid=(rows // tm, n_f),
        in_specs=in_specs,
        out_specs=out_specs,
        scratch_shapes=[pltpu.VMEM((tm, D_MODEL), BF16)],
        compiler_params=_cparams(semantics, 60),
        name=f"ffn_l{layer}_s{sub}_{rows}",
    )(*args)
    if not cast_jobs:
        return out, ()
    return out[0], tuple(out[1:])


def _mix_in_kernel(x_ref, sh_ref, sc_ref, w_ref, cs_ref, gb_ref, v_ref, f_ref):
    xm = (x_ref[...] * (1.0 + sc_ref[...]) + sh_ref[...]).astype(BF16)
    gb_ref[...] = _dot(xm, w_ref[:, 0:CONV_DIM])
    g_c = _dot(xm, w_ref[:, CONV_DIM:2 * CONV_DIM])
    x_in = _dot(xm, w_ref[:, 2 * CONV_DIM:3 * CONV_DIM])
    v_ref[...] = g_c * x_in
    u_f = _dot(xm, w_ref[:, 3 * CONV_DIM:]).astype(BF16)
    cs = cs_ref[...]
    for g in range(FOURIER_GROUPS):
        lanes = slice(g * FOURIER_GROUP_DIM, (g + 1) * FOURIER_GROUP_DIM)
        res = _dot(u_f[:, lanes], cs)
        f_ref[0, :, lanes] = res[:, :FOURIER_GROUP_DIM]
        f_ref[1, :, lanes] = res[:, FOURIER_GROUP_DIM:]


def _mix_in(x, mod, cond_of_tile, w_in, cs, tm):
    rows = x.shape[0]
    n_in = 3 * CONV_DIM + FOURIER_DIM
    return pl.pallas_call(
        _mix_in_kernel,
        out_shape=(jax.ShapeDtypeStruct((rows, CONV_DIM), F32),
                   jax.ShapeDtypeStruct((rows, CONV_DIM), F32),
                   jax.ShapeDtypeStruct((2, rows, FOURIER_DIM), F32)),
        grid=(rows // tm,),
        in_specs=[
            pl.BlockSpec((tm, D_MODEL), lambda i: (i, 0)),
            _mod_spec(0, 3, cond_of_tile),
            _mod_spec(0, 4, cond_of_tile),
            pl.BlockSpec((D_MODEL, n_in), lambda i: (0, 0), pipeline_mode=pl.Buffered(1)),
            pl.BlockSpec((FOURIER_GROUP_DIM, 2 * FOURIER_GROUP_DIM), lambda i: (0, 0)),
        ],
        out_specs=(pl.BlockSpec((tm, CONV_DIM), lambda i: (i, 0)),
                   pl.BlockSpec((tm, CONV_DIM), lambda i: (i, 0)),
                   pl.BlockSpec((2, tm, FOURIER_DIM), lambda i: (0, i, 0))),
        compiler_params=_cparams(("parallel",), 56),
        name=f"mix_in_{rows}",
    )(x, mod, mod, w_in, cs)


def _fft1_kernel(m_ref, v_ref, o_ref):
    n1 = v_ref.shape[1]
    ch = v_ref.shape[3]
    m = m_ref[...]
    for j in range(SUBLANES):
        v = jnp.concatenate([v_ref[0, :, j, :], v_ref[1, :, j, :]], axis=0).astype(BF16)
        t = _dot(m, v)
        o_ref[0, :, j, :] = t[:n1]
        o_ref[1, :, j, :] = t[n1:]


def _fft1(v, m1, batch, n1, n2):
    shape = (2, batch, n1, n2 // SUBLANES, SUBLANES, FOURIER_DIM)
    spec = pl.BlockSpec((2, None, n1, None, SUBLANES, FOURIER_DIM), lambda b, j: (0, b, 0, j, 0, 0))
    return pl.pallas_call(
        _fft1_kernel,
        out_shape=jax.ShapeDtypeStruct(shape, F32),
        grid=(batch, n2 // SUBLANES),
        in_specs=[pl.BlockSpec((2 * n1, 2 * n1), lambda b, j: (0, 0)), spec],
        out_specs=spec,
        compiler_params=_cparams(("parallel", "parallel"), 40),
        name="fft_stage1",
    )(m1, v.reshape(shape))


def _fft2_one(tr, ti, twc, tws, f, scale):
    reps = FOURIER_DIM // LANES
    c = jnp.concatenate([twc] * reps, axis=1)
    s = jnp.concatenate([tws] * reps, axis=1)
    pr = (tr * c + ti * s).astype(BF16)
    pi = (ti * c - tr * s).astype(BF16)
    return _dot(f, jnp.concatenate([pr, pi], axis=0)) * scale


def _fft2_kernel(t_ref, twc_ref, tws_ref, f_ref, o_ref, *, scale):
    f = f_ref[...]
    for j in range(SUBLANES):
        o_ref[:, j, :] = _fft2_one(t_ref[0, j], t_ref[1, j], twc_ref[j], tws_ref[j], f, scale)


def _fft2_single_kernel(t_ref, twc_ref, tws_ref, f_ref, o_ref, *, scale):
    o_ref[...] = _fft2_one(t_ref[0], t_ref[1], twc_ref[...], tws_ref[...], f_ref[...], scale)


def _fft2(t, twc, tws, f2, batch, n1, n2):
    t5 = t.reshape(2, batch, n1, n2, FOURIER_DIM)
    scale = 1.0 / math.sqrt(n1 * n2 * FOURIER_GROUP_DIM)
    f_spec = pl.BlockSpec((n2, 2 * n2), lambda b, k: (0, 0))
    if n1 == 1:
        out = pl.pallas_call(
            functools.partial(_fft2_single_kernel, scale=scale),
            out_shape=jax.ShapeDtypeStruct((batch, n2, FOURIER_DIM), F32),
            grid=(batch, 1),
            in_specs=[
                pl.BlockSpec((2, None, None, n2, FOURIER_DIM), lambda b, k: (0, b, 0, 0, 0)),
                pl.BlockSpec((None, n2, LANES), lambda b, k: (0, 0, 0)),
                pl.BlockSpec((None, n2, LANES), lambda b, k: (0, 0, 0)),
                f_spec,
            ],
            out_specs=pl.BlockSpec((None, n2, FOURIER_DIM), lambda b, k: (b, 0, 0)),
            compiler_params=_cparams(("parallel", "parallel"), 32),
            name=f"fft_stage2_{n1}x{n2}",
        )(t5, twc, tws, f2)
    else:
        out = pl.pallas_call(
            functools.partial(_fft2_kernel, scale=scale),
            out_shape=jax.ShapeDtypeStruct((batch, n2, n1 // SUBLANES, SUBLANES, FOURIER_DIM), F32),
            grid=(batch, n1 // SUBLANES),
            in_specs=[
                pl.BlockSpec((2, None, SUBLANES, n2, FOURIER_DIM), lambda b, k: (0, b, k, 0, 0)),
                pl.BlockSpec((SUBLANES, n2, LANES), lambda b, k: (k, 0, 0)),
                pl.BlockSpec((SUBLANES, n2, LANES), lambda b, k: (k, 0, 0)),
                f_spec,
            ],
            out_specs=pl.BlockSpec((None, n2, None, SUBLANES, FOURIER_DIM), lambda b, k: (b, 0, k, 0, 0)),
            compiler_params=_cparams(("parallel", "parallel"), 48),
            name=f"fft_stage2_{n1}x{n2}",
        )(t5, twc, tws, f2)
    return out.reshape(batch * n2 * n1, FOURIER_DIM)


def _mix_out_kernel(x_ref, gt_ref, gb_ref, v_ref, vp_ref, vn_ref, yb_ref, wc_ref, wo_ref,
                    lg_ref, lb_ref, o_ref, *, tiles_per_seq):
    i = pl.program_id(0)
    tm = v_ref.shape[0]
    pos = lax.rem(i, tiles_per_seq)
    v = v_ref[...]
    prev_row = jnp.where(pos == 0, 0.0, vp_ref[7:8, :])
    next_row = jnp.where(pos == tiles_per_seq - 1, 0.0, vn_ref[0:1, :])
    row = lax.broadcasted_iota(jnp.int32, v.shape, 0)
    v_prev = jnp.where(row == 0, prev_row, pltpu.roll(v, 1, 0))
    v_next = jnp.where(row == tm - 1, next_row, pltpu.roll(v, tm - 1, 0))
    conv = wc_ref[0:1, :] * v_prev + wc_ref[1:2, :] * v + wc_ref[2:3, :] * v_next
    y_a = (gb_ref[...] * conv).astype(BF16)
    for r in range(tm // PROJ_SLAB):
        rows = pl.ds(r * PROJ_SLAB, PROJ_SLAB)
        y = (_dot(y_a[r * PROJ_SLAB:(r + 1) * PROJ_SLAB], wo_ref[0:CONV_DIM, :])
             + _dot(yb_ref[rows, :].astype(BF16), wo_ref[CONV_DIM:, :]))
        z = ALPHA * x_ref[rows, :] + gt_ref[...] * y
        o_ref[rows, :] = _layer_norm_rows(z, lg_ref[...], lb_ref[...])


def _mix_out(x, mod, cond_of_tile, gb, v, yb, w_conv, w_out, ln_g, ln_b, tm, seq_len):
    rows = x.shape[0]
    sub = 8
    v3 = v.reshape(rows // sub, sub, CONV_DIM)
    step = tm // sub
    last = rows // sub - 1
    return pl.pallas_call(
        functools.partial(_mix_out_kernel, tiles_per_seq=seq_len // tm),
        out_shape=jax.ShapeDtypeStruct((rows, D_MODEL), F32),
        grid=(rows // tm,),
        in_specs=[
            pl.BlockSpec((tm, D_MODEL), lambda i: (i, 0)),
            _mod_spec(0, 5, cond_of_tile),
            pl.BlockSpec((tm, CONV_DIM), lambda i: (i, 0)),
            pl.BlockSpec((tm, CONV_DIM), lambda i: (i, 0)),
            pl.BlockSpec((None, sub, CONV_DIM), lambda i: (jnp.maximum(i * step - 1, 0), 0, 0)),
            pl.BlockSpec((None, sub, CONV_DIM), lambda i: (jnp.minimum((i + 1) * step, last), 0, 0)),
            pl.BlockSpec((tm, FOURIER_DIM), lambda i: (i, 0)),
            pl.BlockSpec((None, 3, CONV_DIM), lambda i: (0, 0, 0)),
            pl.BlockSpec((D_MODEL, D_MODEL), lambda i: (0, 0), pipeline_mode=pl.Buffered(1)),
            _ln_spec(0, 1),
            _ln_spec(0, 1),
        ],
        out_specs=pl.BlockSpec((tm, D_MODEL), lambda i: (i, 0)),
        compiler_params=_cparams(("parallel",), 56),
        name=f"mix_out_{rows}",
    )(x, mod, gb, v, v3, v3, yb, w_conv, w_out, ln_g, ln_b)


def _rope_block(blk, cos, sin, first_half):
    rot = jnp.where(first_half, pltpu.roll(blk, LANES - 16, 1), pltpu.roll(blk, 16, 1))
    return blk * cos + rot * sin


def _dup_heads(blk, low_half):
    swapped = pltpu.roll(blk, HEAD_DIM, 1)
    return jnp.where(low_half, blk, swapped), jnp.where(low_half, swapped, blk)


def _qkv_kernel(x_ref, sh_ref, sc_ref, w_ref, wvt_ref, cos_ref, sin_ref, q_ref, k_ref, vt_ref, *, rope):
    tm = x_ref.shape[0]
    xm = (x_ref[...] * (1.0 + sc_ref[...]) + sh_ref[...]).astype(BF16)
    lane = lax.broadcasted_iota(jnp.int32, (tm, LANES), 1)
    first_half = jnp.bitwise_and(lane, 31) < 16
    low_half = lane < HEAD_DIM
    n_q = N_Q_HEADS * HEAD_DIM
    n_kv = N_KV_HEADS * HEAD_DIM
    if rope:
        cos = cos_ref[...]
        sin = sin_ref[...]
    if q_ref is not None:
        q = _dot(xm, w_ref[:, 0:n_q])
        for j in range(n_q // LANES):
            lanes = slice(j * LANES, (j + 1) * LANES)
            blk = q[:, lanes]
            if rope:
                blk = _rope_block(blk, cos, sin, first_half)
            q_ref[:, lanes] = (blk * (LOG2_E * HEAD_DIM ** -0.5)).astype(BF16)
    k = _dot(xm, w_ref[:, n_q:n_q + n_kv])
    v_t = lax.dot_general(wvt_ref[...], xm, (((1,), (1,)), ((), ())),
                          preferred_element_type=F32).astype(BF16)
    for j in range(n_kv // LANES):
        kb = k[:, j * LANES:(j + 1) * LANES]
        if rope:
            kb = _rope_block(kb, cos, sin, first_half)
        k0, k1 = _dup_heads(kb, low_half)
        k_ref[:, 2 * j * LANES:(2 * j + 1) * LANES] = k0.astype(BF16)
        k_ref[:, (2 * j + 1) * LANES:(2 * j + 2) * LANES] = k1.astype(BF16)
    for g in range(N_KV_HEADS):
        head = v_t[g * HEAD_DIM:(g + 1) * HEAD_DIM, :]
        vt_ref[g * LANES:g * LANES + HEAD_DIM, :] = head
        vt_ref[g * LANES + HEAD_DIM:(g + 1) * LANES, :] = head


def _qkv_latent_kernel(x_ref, sh_ref, sc_ref, w_ref, wvt_ref, cos_ref, sin_ref, q_ref, k_ref, vt_ref):
    _qkv_kernel(x_ref, sh_ref, sc_ref, w_ref, wvt_ref, cos_ref, sin_ref, q_ref, k_ref, vt_ref, rope=True)


def _kv_context_kernel(x_ref, sh_ref, sc_ref, w_ref, wvt_ref, k_ref, vt_ref):
    _qkv_kernel(x_ref, sh_ref, sc_ref, w_ref, wvt_ref, None, None, None, k_ref, vt_ref, rope=False)


def _qkv_latent(x, mod, cond_of_tile, w_in, wv_t, cos_t, sin_t, tm, batch, seq_len):
    rows = x.shape[0]
    n_in = (N_Q_HEADS + 2 * N_KV_HEADS) * HEAD_DIM
    tps = seq_len // tm
    dup = N_KV_HEADS * LANES
    return pl.pallas_call(
        _qkv_latent_kernel,
        out_shape=(jax.ShapeDtypeStruct((rows, N_Q_HEADS * HEAD_DIM), BF16),
                   jax.ShapeDtypeStruct((rows, dup), BF16),
                   jax.ShapeDtypeStruct((batch, dup, seq_len), BF16)),
        grid=(rows // tm,),
        in_specs=[
            pl.BlockSpec((tm, D_MODEL), lambda i: (i, 0)),
            _mod_spec(1, 3, cond_of_tile),
            _mod_spec(1, 4, cond_of_tile),
            pl.BlockSpec((D_MODEL, n_in), lambda i: (0, 0), pipeline_mode=pl.Buffered(1)),
            pl.BlockSpec((N_KV_HEADS * HEAD_DIM, D_MODEL), lambda i: (0, 0)),
            pl.BlockSpec((tm, LANES), lambda i: (lax.rem(i, tps), 0)),
            pl.BlockSpec((tm, LANES), lambda i: (lax.rem(i, tps), 0)),
        ],
        out_specs=(pl.BlockSpec((tm, N_Q_HEADS * HEAD_DIM), lambda i: (i, 0)),
                   pl.BlockSpec((tm, dup), lambda i: (i, 0)),
                   pl.BlockSpec((None, dup, tm), lambda i: (i // tps, 0, lax.rem(i, tps)))),
        compiler_params=_cparams(("parallel",), 48),
        name="qkv_latent",
    )(x, mod, mod, w_in, wv_t, cos_t, sin_t)


def _kv_context(x, mod, cond_of_tile, w_in, wv_t, tm, batch):
    rows = x.shape[0]
    n_in = (N_Q_HEADS + 2 * N_KV_HEADS) * HEAD_DIM
    dup = N_KV_HEADS * LANES
    return pl.pallas_call(
        _kv_context_kernel,
        out_shape=(jax.ShapeDtypeStruct((rows, dup), BF16),
                   jax.ShapeDtypeStruct((batch, dup, tm), BF16)),
        grid=(rows // tm,),
        in_specs=[
            pl.BlockSpec((tm, D_MODEL), lambda i: (i, 0)),
            _mod_spec(1, 3, cond_of_tile),
            _mod_spec(1, 4, cond_of_tile),
            pl.BlockSpec((D_MODEL, n_in), lambda i: (0, 0), pipeline_mode=pl.Buffered(1)),
            pl.BlockSpec((N_KV_HEADS * HEAD_DIM, D_MODEL), lambda i: (0, 0)),
        ],
        out_specs=(pl.BlockSpec((tm, dup), lambda i: (i, 0)),
                   pl.BlockSpec((None, dup, tm), lambda i: (i, 0, 0))),
        compiler_params=_cparams(("parallel",), 40),
        name="kv_context",
    )(x, mod, mod, w_in, wv_t)


def _band_start(i, seq_len):
    return jnp.clip((i - 1) * BLOCK, 0, seq_len - 3 * BLOCK)


def _attn_kernel(sink_ref, q_ref, kl_ref, vlt_ref, kc_ref, vct_ref, bias_ref, o_ref, *, seq_len):
    i = pl.program_id(1)
    band = 3 * BLOCK
    n_ctx = kc_ref.shape[0]
    pairs = KV_REP // 2
    start = pl.multiple_of(_band_start(i, seq_len), BLOCK)
    low_half = lax.broadcasted_iota(jnp.int32, (BLOCK, LANES), 1) < HEAD_DIM
    top_half = lax.broadcasted_iota(jnp.int32, (LANES, BLOCK), 0) < HEAD_DIM

    def scores_t(g):
        kv = slice(g * LANES, (g + 1) * LANES)
        q_parts = []
        for p in range(pairs):
            blk = q_ref[:, (g * pairs + p) * LANES:(g * pairs + p + 1) * LANES]
            q_parts.append(jnp.where(low_half, blk, jnp.zeros_like(blk)))
            q_parts.append(jnp.where(low_half, jnp.zeros_like(blk), blk))
        q_g = jnp.concatenate(q_parts, axis=0)
        k_all = jnp.concatenate([kc_ref[:, kv], kl_ref[pl.ds(start, band), kv]], axis=0)
        return lax.dot_general(k_all, q_g, (((1,), (1,)), ((), ())), preferred_element_type=F32)

    pending = [scores_t(g) for g in range(ATTN_AHEAD)]
    for g in range(N_KV_HEADS):
        if g + ATTN_AHEAD < N_KV_HEADS:
            pending.append(scores_t(g + ATTN_AHEAD))
        kv = slice(g * LANES, (g + 1) * LANES)
        s = pending[g]
        sink = jnp.concatenate([jnp.full((1, BLOCK), sink_ref[g * KV_REP + h] * LOG2_E, F32)
                                for h in range(KV_REP)], axis=1)
        s_ctx = s[:n_ctx]
        s_loc = s[n_ctx:] + bias_ref[...]
        m = jnp.maximum(jnp.maximum(jnp.max(s_ctx, axis=0, keepdims=True),
                                    jnp.max(s_loc, axis=0, keepdims=True)), sink)
        p_ctx = jnp.exp2(s_ctx - m)
        p_loc = jnp.exp2(s_loc - m)
        denom = (jnp.sum(p_ctx, axis=0, keepdims=True) + jnp.sum(p_loc, axis=0, keepdims=True)
                 + jnp.exp2(sink - m))
        probs_t = jnp.concatenate([p_ctx, p_loc], axis=0).astype(BF16)
        vt_all = jnp.concatenate([vct_ref[kv, :], vlt_ref[kv, pl.ds(start, band)]], axis=1)
        o_t = _dot(vt_all, probs_t) / denom
        for p in range(pairs):
            lo = o_t[:, 2 * p * BLOCK:(2 * p + 1) * BLOCK]
            hi = o_t[:, (2 * p + 1) * BLOCK:(2 * p + 2) * BLOCK]
            pair_t = jnp.where(top_half, lo, hi)
            o_ref[:, (g * pairs + p) * LANES:(g * pairs + p + 1) * LANES] = pair_t.T.astype(BF16)


def _window_bias(seq_len):
    band = 3 * BLOCK
    cols = KV_REP * BLOCK
    n_blk = seq_len // BLOCK
    blocks = jnp.array([0, 1, n_blk - 1], jnp.int32)
    delta = (_band_start(blocks, seq_len) - blocks * BLOCK)[:, None, None]
    j = jnp.arange(band, dtype=jnp.int32)[None, :, None]
    a = jnp.bitwise_and(jnp.arange(cols, dtype=jnp.int32), BLOCK - 1)[None, None, :]
    return jnp.where(jnp.abs(j + delta - a) <= WINDOW, 0.0, NEG_INF).astype(F32)


def _attention(sink, q, k_l, vt_l, k_c, vt_c, batch, seq_len, n_ctx):
    n_blk = seq_len // BLOCK
    assert n_blk >= 3
    dup = N_KV_HEADS * LANES
    n_q = N_Q_HEADS * HEAD_DIM
    band = 3 * BLOCK

    def bias_case(b, i):
        return (jnp.where(i == 0, 0, jnp.where(i == n_blk - 1, 2, 1)), 0, 0)

    return pl.pallas_call(
        functools.partial(_attn_kernel, seq_len=seq_len),
        out_shape=jax.ShapeDtypeStruct((batch * seq_len, n_q), BF16),
        grid=(batch, n_blk),
        in_specs=[
            pl.BlockSpec(memory_space=pltpu.SMEM),
            pl.BlockSpec((BLOCK, n_q), lambda b, i: (b * n_blk + i, 0)),
            pl.BlockSpec((None, seq_len, dup), lambda b, i: (b, 0, 0)),
            pl.BlockSpec((None, dup, seq_len), lambda b, i: (b, 0, 0)),
            pl.BlockSpec((None, n_ctx, dup), lambda b, i: (b, 0, 0)),
            pl.BlockSpec((None, dup, n_ctx), lambda b, i: (b, 0, 0)),
            pl.BlockSpec((None, band, KV_REP * BLOCK), bias_case),
        ],
        out_specs=pl.BlockSpec((BLOCK, n_q), lambda b, i: (b * n_blk + i, 0)),
        compiler_params=_cparams(("parallel", "arbitrary"), 56),
        name="window_attention",
    )(sink, q, k_l.reshape(batch, seq_len, dup), vt_l, k_c.reshape(batch, n_ctx, dup), vt_c,
      _window_bias(seq_len))


def _attn_out_kernel(x_ref, gt_ref, a_ref, wo_ref, lg_ref, lb_ref, o_ref):
    for r in range(x_ref.shape[0] // PROJ_SLAB):
        rows = pl.ds(r * PROJ_SLAB, PROJ_SLAB)
        y = _dot(a_ref[rows, :], wo_ref[...])
        z = ALPHA * x_ref[rows, :] + gt_ref[...] * y
        o_ref[rows, :] = _layer_norm_rows(z, lg_ref[...], lb_ref[...])


def _attn_out(x, mod, cond_of_tile, a, w_out, ln_g, ln_b, tm):
    rows = x.shape[0]
    return pl.pallas_call(
        _attn_out_kernel,
        out_shape=jax.ShapeDtypeStruct((rows, D_MODEL), F32),
        grid=(rows // tm,),
        in_specs=[
            pl.BlockSpec((tm, D_MODEL), lambda i: (i, 0)),
            _mod_spec(1, 5, cond_of_tile),
            pl.BlockSpec((tm, D_MODEL), lambda i: (i, 0)),
            pl.BlockSpec((D_MODEL, D_MODEL), lambda i: (0, 0), pipeline_mode=pl.Buffered(1)),
            _ln_spec(1, 1),
            _ln_spec(1, 1),
        ],
        out_specs=pl.BlockSpec((tm, D_MODEL), lambda i: (i, 0)),
        compiler_params=_cparams(("parallel",), 48),
        name="attn_out",
    )(x, mod, a, w_out, ln_g, ln_b)


def _dft_cos_sin(n):
    k = jnp.arange(n, dtype=jnp.int32)
    ang = ((k[:, None] * k[None, :]) % n).astype(F32) * (2.0 * math.pi / n)
    return jnp.cos(ang), jnp.sin(ang)


def _twiddles(n1, n2):
    k1 = jnp.arange(n1, dtype=jnp.int32)[:, None]
    m2 = jnp.arange(n2, dtype=jnp.int32)[None, :]
    n = n1 * n2
    ang = ((k1 * m2) % n).astype(F32) * (2.0 * math.pi / n)
    shape = (n1, n2, LANES)
    return (jnp.broadcast_to(jnp.cos(ang)[:, :, None], shape),
            jnp.broadcast_to(jnp.sin(ang)[:, :, None], shape))


def _rope_lane_tables(seq_len):
    rows = seq_len // GRID_W
    row = jnp.repeat(jnp.arange(rows, dtype=F32), GRID_W)
    col = jnp.tile(jnp.arange(GRID_W, dtype=F32), rows)
    n_freq = HEAD_DIM // 4
    inv_freq = jnp.power(ROPE_BASE, -jnp.arange(n_freq, dtype=F32) / n_freq)
    ang_r = row[:, None] * inv_freq
    ang_c = col[:, None] * inv_freq
    cos_h = jnp.concatenate([jnp.cos(ang_r), jnp.cos(ang_r), jnp.cos(ang_c), jnp.cos(ang_c)], axis=-1)
    sin_h = jnp.concatenate([-jnp.sin(ang_r), jnp.sin(ang_r), -jnp.sin(ang_c), jnp.sin(ang_c)], axis=-1)
    reps = LANES // HEAD_DIM
    return jnp.tile(cos_h, (1, reps)), jnp.tile(sin_h, (1, reps))


def kernel(x, c, ctx, c_ctx, w_mod, b_mod, ln_g, ln_b, ffn_w_gate, ffn_w_up, ffn_w_down,
           ab_w_in, ab_conv, ab_w_out, attn_w_in, attn_sink, attn_w_out):
    batch, seq_len, _ = x.shape
    n_ctx = ctx.shape[1]
    assert batch == 2 and seq_len % FFN_TM == 0 and seq_len % (FFT_N1 * 8) == 0

    xl = x.reshape(batch * seq_len, D_MODEL)
    xc = ctx.reshape(batch * n_ctx, D_MODEL)
    ctx_rows = batch * n_ctx

    cond = jnp.concatenate([c, c_ctx[None, :], jnp.zeros((N_COND - batch - 1, D_MODEL), F32)], axis=0)
    mod = _modulation(cond, w_mod, b_mod)
    ln_g4 = ln_g.reshape(DEPTH, 3, 1, D_MODEL)
    ln_b4 = ln_b.reshape(DEPTH, 3, 1, D_MODEL)

    ffn_f32 = (ffn_w_gate, ffn_w_up, ffn_w_down)
    first_weights = tuple(w[0, 0].astype(BF16) for w in ffn_f32)
    mixer_f32 = (ab_w_in, ab_w_out, attn_w_in, attn_w_out)

    def lat_cond(tm):
        tiles = seq_len // tm
        return lambda i: i // tiles

    ctx_cond = lambda i: CTX_COND

    def ffn_both(xl, xc, layer, sub, weights, with_ctx, extra_jobs=()):
        nxt = (layer, sub + 1) if sub == 0 else (layer + 1, 0)
        jobs = tuple((w, nxt) for w in ffn_f32) if nxt[0] < DEPTH else ()
        xl, copies = _ffn(xl, mod, layer, sub, lat_cond(FFN_TM), weights, ln_g4, ln_b4, FFN_TM,
                          jobs + tuple(extra_jobs))
        if with_ctx:
            xc, _ = _ffn(xc, mod, layer, sub, ctx_cond, weights, ln_g4, ln_b4, ctx_rows)
        return xl, xc, copies[:len(jobs)], copies[len(jobs):]

    xl, xc, weights, (ab_in, ab_out, at_in, at_out) = ffn_both(
        xl, xc, 0, 0, first_weights, True, [(w, (0,)) for w in mixer_f32])

    cc, sc_ = _dft_cos_sin(FOURIER_GROUP_DIM)
    cs_chan = jnp.concatenate([cc, -sc_], axis=1).astype(BF16)
    n1, n2 = FFT_N1, seq_len // FFT_N1
    c1, s1 = _dft_cos_sin(n1)
    m1 = jnp.concatenate([jnp.concatenate([c1, s1], axis=1),
                          jnp.concatenate([-s1, c1], axis=1)], axis=0).astype(BF16)
    c2, s2 = _dft_cos_sin(n2)
    f2 = jnp.concatenate([c2, s2], axis=1).astype(BF16)
    twc, tws = _twiddles(n1, n2)
    cn, sn = _dft_cos_sin(n_ctx)
    f2_ctx = jnp.concatenate([cn, sn], axis=1).astype(BF16)
    twc_ctx, tws_ctx = _twiddles(1, n_ctx)

    gb, v, vf = _mix_in(xl, mod, lat_cond(PROJ_TM), ab_in, cs_chan, PROJ_TM)
    t = _fft1(vf, m1, batch, n1, n2)
    yb = _fft2(t, twc, tws, f2, batch, n1, n2)
    xl = _mix_out(xl, mod, lat_cond(PROJ_TM), gb, v, yb, ab_conv, ab_out, ln_g4, ln_b4, PROJ_TM, seq_len)

    gb_c, v_c, vf_c = _mix_in(xc, mod, ctx_cond, ab_in, cs_chan, n_ctx)
    yb_c = _fft2(vf_c, twc_ctx, tws_ctx, f2_ctx, batch, 1, n_ctx)
    xc = _mix_out(xc, mod, ctx_cond, gb_c, v_c, yb_c, ab_conv, ab_out, ln_g4, ln_b4, n_ctx, n_ctx)

    xl, xc, weights, _ = ffn_both(xl, xc, 0, 1, weights, True)

    xl, xc, weights, _ = ffn_both(xl, xc, 1, 0, weights, True)

    cos_t, sin_t = _rope_lane_tables(seq_len)
    n_qk = (N_Q_HEADS + N_KV_HEADS) * HEAD_DIM
    wv_t = jnp.transpose(attn_w_in[0, :, n_qk:]).astype(BF16)
    q, k_l, vt_l = _qkv_latent(xl, mod, lat_cond(PROJ_TM), at_in, wv_t, cos_t, sin_t, PROJ_TM, batch, seq_len)
    k_c, vt_c = _kv_context(xc, mod, ctx_cond, at_in, wv_t, n_ctx, batch)
    att = _attention(attn_sink.reshape(N_Q_HEADS), q, k_l, vt_l, k_c, vt_c, batch, seq_len, n_ctx)
    xl = _attn_out(xl, mod, lat_cond(PROJ_TM), att, at_out, ln_g4, ln_b4, PROJ_TM)

    xl, _, _, _ = ffn_both(xl, xc, 1, 1, weights, False)
    return xl.reshape(batch, seq_len, D_MODEL)
```

```python
import functools
import math

import jax
import jax.numpy as jnp
from jax import lax
from jax.experimental import pallas as pl
from jax.experimental.pallas import tpu as pltpu

F32 = jnp.float32
BF16 = jnp.bfloat16

D_MODEL = 2048
GRID_W = 64
N_MOD = 9
D_FF = 5632
CONV_DIM = 1024
FOURIER_DIM = 1024
FOURIER_GROUPS = 8
FOURIER_GROUP_DIM = 128
HEAD_DIM = 64
N_Q_HEADS = 32
N_KV_HEADS = 4
KV_REP = 8
WINDOW = 128
BLOCK = 128
ROPE_BASE = 10000.0
LN_EPS = 1e-5
NEG_INF = -1e30
DEPTH = 2
ALPHA = (2 * DEPTH) ** 0.25
LOG2_E = math.log2(math.e)

LANES = 128
SUBLANES = 8
BF16_SUBLANES = 16
V7X_VMEM_BYTES = 64 * 1024 * 1024
MIB = 1024 * 1024

N_COND = 8
CTX_COND = 2
FFN_TM = 1024
FFN_TF = 512
FFN_TF_CTX = 1408
FFN_SLAB = 256
PROJ_TM = 512
PROJ_SLAB = 256
ATTN_AHEAD = 2
ATTN_OUT_TM = 1024
MOD_RIDER_TN = 1024
FFT_N1 = 64


def _cparams(semantics, vmem_mib):
    return pltpu.CompilerParams(dimension_semantics=semantics,
                                vmem_limit_bytes=min(vmem_mib * MIB, V7X_VMEM_BYTES))


def _dot(a, b):
    return jnp.dot(a, b, preferred_element_type=F32)


def _layer_norm_rows(z, g, b):
    mu = jnp.mean(z, axis=-1, keepdims=True)
    zc = z - mu
    var = jnp.mean(zc * zc, axis=-1, keepdims=True)
    return zc * lax.rsqrt(var + LN_EPS) * g + b


def _mod_block(c_ref, w_ref, b_ref):
    c = c_ref[...]
    a = (c * jax.nn.sigmoid(c)).astype(BF16)
    return _dot(a, w_ref[...].astype(BF16)) + b_ref[...]


def _mod_kernel(c_ref, w_ref, b_ref, o_ref):
    o_ref[:, 0, :] = _mod_block(c_ref, w_ref, b_ref)


def _modulation(cond, w_mod, b_mod3, layer):
    return pl.pallas_call(
        _mod_kernel,
        out_shape=jax.ShapeDtypeStruct((N_COND, N_MOD, 1, D_MODEL), F32),
        grid=(N_MOD,),
        in_specs=[
            pl.BlockSpec((N_COND, D_MODEL), lambda j: (0, 0)),
            pl.BlockSpec((None, D_MODEL, D_MODEL), lambda j: (layer, 0, j)),
            pl.BlockSpec((None, 1, D_MODEL), lambda j: (layer, 0, j)),
        ],
        out_specs=pl.BlockSpec((N_COND, None, 1, D_MODEL), lambda j: (0, j, 0, 0)),
        compiler_params=_cparams(("parallel",), 48),
        name=f"modulation_l{layer}",
    )(cond, w_mod, b_mod3)


def _mod_spec(k, cond_of_tile):
    return pl.BlockSpec((None, None, 1, D_MODEL), lambda i, *_: (cond_of_tile(i), k, 0, 0))


def _ln_spec(layer, k):
    return pl.BlockSpec((None, None, 1, D_MODEL), lambda i, *_: (layer, k, 0, 0))


def _ffn_kernel(x_ref, sh_ref, sc_ref, gt_ref, wg_ref, wu_ref, wd_ref, lg_ref, lb_ref, *rest):
    n_cast = (len(rest) - 2) // 2
    o_ref, xm_ref = rest[n_cast], rest[-1]
    cast_pairs = tuple(zip(rest[:n_cast], rest[n_cast + 1:-1]))
    f = pl.program_id(1)
    last = pl.num_programs(1) - 1
    slabs = [pl.ds(r * FFN_SLAB, FFN_SLAB) for r in range(x_ref.shape[0] // FFN_SLAB)]
    half_gate = 0.5 * gt_ref[...]

    def cast_next_slice():
        for src, dst in cast_pairs:
            dst[...] = src[...].astype(BF16)

    def swiglu_chunk(rows):
        xm = xm_ref[rows, :]
        h_gate = _dot(xm, wg_ref[...])
        h_up = _dot(xm, wu_ref[...])
        act = (h_gate * jax.nn.sigmoid(h_gate) * h_up).astype(BF16)
        return half_gate * _dot(act, wd_ref[...])

    @pl.when(f == 0)
    def _():
        cast_next_slice()
        scale1 = 1.0 + sc_ref[...]
        shift = sh_ref[...]
        for rows in slabs:
            x = x_ref[rows, :]
            xm_ref[rows, :] = (x * scale1 + shift).astype(BF16)
            o_ref[rows, :] = ALPHA * x + swiglu_chunk(rows)

    @pl.when(jnp.logical_and(f > 0, f < last))
    def _():
        cast_next_slice()
        o_ref[...] += swiglu_chunk(slice(None))

    @pl.when(f == last)
    def _():
        cast_next_slice()
        g = lg_ref[...]
        b = lb_ref[...]
        for rows in slabs:
            o_ref[rows, :] = _layer_norm_rows(o_ref[rows, :] + swiglu_chunk(rows), g, b)


def _cast_specs(src, lead, n_f, n_steps):
    n_rows, n_cols = src.shape[-2:]
    rows = BF16_SUBLANES
    while n_rows % rows or n_rows // rows > n_steps:
        rows += BF16_SUBLANES
    last_block = n_rows // rows - 1
    step = lambda i, f: jnp.minimum(i * n_f + f, last_block)
    in_spec = pl.BlockSpec((None,) * len(lead) + (rows, n_cols), lambda i, f: (*lead, step(i, f), 0))
    out_spec = pl.BlockSpec((rows, n_cols), lambda i, f: (step(i, f), 0))
    return in_spec, out_spec, jax.ShapeDtypeStruct((n_rows, n_cols), BF16)


def _ffn(x, mod, layer, sub, cond_of_tile, weights, ln_g, ln_b, tm, cast_jobs=(), tf=FFN_TF):
    rows = x.shape[0]
    k0 = 6 * sub
    n_f = D_FF // tf
    assert n_f * tf == D_FF and n_f >= 2
    n_steps = (rows // tm) * n_f
    w_in_spec = pl.BlockSpec((D_MODEL, tf), lambda i, f: (0, f))
    in_specs = [
        pl.BlockSpec((tm, D_MODEL), lambda i, f: (i, 0)),
        _mod_spec(k0, cond_of_tile),
        _mod_spec(k0 + 1, cond_of_tile),
        _mod_spec(k0 + 2, cond_of_tile),
        w_in_spec,
        w_in_spec,
        pl.BlockSpec((tf, D_MODEL), lambda i, f: (f, 0)),
        _ln_spec(layer, 2 * sub),
        _ln_spec(layer, 2 * sub),
    ]
    out_shape = jax.ShapeDtypeStruct((rows, D_MODEL), F32)
    out_specs = pl.BlockSpec((tm, D_MODEL), lambda i, f: (i, 0))
    args = (x, mod, mod, mod, *weights, ln_g, ln_b)
    semantics = ("parallel", "arbitrary")
    if cast_jobs:
        specs = [_cast_specs(src, lead, n_f, n_steps) for src, lead in cast_jobs]
        in_specs += [s[0] for s in specs]
        out_specs = (out_specs, *[s[1] for s in specs])
        out_shape = (out_shape, *[s[2] for s in specs])
        args += tuple(src for src, _ in cast_jobs)
        semantics = ("arbitrary", "arbitrary")
    out = pl.pallas_call(
        _ffn_kernel,
        out_shape=out_shape,
        grid=(rows // tm, n_f),
        in_specs=in_specs,
        out_specs=out_specs,
        scratch_shapes=[pltpu.VMEM((tm, D_MODEL), BF16)],
        compiler_params=_cparams(semantics, 60),
        name=f"ffn_l{layer}_s{sub}_{rows}",
    )(*args)
    if not cast_jobs:
        return out, ()
    return out[0], tuple(out[1:])


def _mix_in_kernel(x_ref, sh_ref, sc_ref, w_ref, cs_ref, gb_ref, v_ref, f_ref):
    xm = (x_ref[...] * (1.0 + sc_ref[...]) + sh_ref[...]).astype(BF16)
    gb_ref[...] = _dot(xm, w_ref[:, 0:CONV_DIM])
    g_c = _dot(xm, w_ref[:, CONV_DIM:2 * CONV_DIM])
    x_in = _dot(xm, w_ref[:, 2 * CONV_DIM:3 * CONV_DIM])
    v_ref[...] = g_c * x_in
    u_f = _dot(xm, w_ref[:, 3 * CONV_DIM:]).astype(BF16)
    cs = cs_ref[...]
    for g in range(FOURIER_GROUPS):
        lanes = slice(g * FOURIER_GROUP_DIM, (g + 1) * FOURIER_GROUP_DIM)
        res = _dot(u_f[:, lanes], cs)
        f_ref[0, :, lanes] = res[:, :FOURIER_GROUP_DIM]
        f_ref[1, :, lanes] = res[:, FOURIER_GROUP_DIM:]


def _mix_in(x, mod, cond_of_tile, w_in, cs, tm):
    rows = x.shape[0]
    n_in = 3 * CONV_DIM + FOURIER_DIM
    return pl.pallas_call(
        _mix_in_kernel,
        out_shape=(jax.ShapeDtypeStruct((rows, CONV_DIM), F32),
                   jax.ShapeDtypeStruct((rows, CONV_DIM), F32),
                   jax.ShapeDtypeStruct((2, rows, FOURIER_DIM), F32)),
        grid=(rows // tm,),
        in_specs=[
            pl.BlockSpec((tm, D_MODEL), lambda i: (i, 0)),
            _mod_spec(3, cond_of_tile),
            _mod_spec(4, cond_of_tile),
            pl.BlockSpec((D_MODEL, n_in), lambda i: (0, 0), pipeline_mode=pl.Buffered(1)),
            pl.BlockSpec((FOURIER_GROUP_DIM, 2 * FOURIER_GROUP_DIM), lambda i: (0, 0)),
        ],
        out_specs=(pl.BlockSpec((tm, CONV_DIM), lambda i: (i, 0)),
                   pl.BlockSpec((tm, CONV_DIM), lambda i: (i, 0)),
                   pl.BlockSpec((2, tm, FOURIER_DIM), lambda i: (0, i, 0))),
        compiler_params=_cparams(("parallel",), 56),
        name=f"mix_in_{rows}",
    )(x, mod, mod, w_in, cs)


def _fft1_kernel(m_ref, v_ref, o_ref):
    n1 = v_ref.shape[1]
    ch = v_ref.shape[3]
    m = m_ref[...]
    for j in range(SUBLANES):
        v = jnp.concatenate([v_ref[0, :, j, :], v_ref[1, :, j, :]], axis=0).astype(BF16)
        t = _dot(m, v)
        o_ref[0, :, j, :] = t[:n1]
        o_ref[1, :, j, :] = t[n1:]


def _fft1(v, m1, batch, n1, n2):
    shape = (2, batch, n1, n2 // SUBLANES, SUBLANES, FOURIER_DIM)
    spec = pl.BlockSpec((2, None, n1, None, SUBLANES, FOURIER_DIM), lambda b, j: (0, b, 0, j, 0, 0))
    return pl.pallas_call(
        _fft1_kernel,
        out_shape=jax.ShapeDtypeStruct(shape, F32),
        grid=(batch, n2 // SUBLANES),
        in_specs=[pl.BlockSpec((2 * n1, 2 * n1), lambda b, j: (0, 0)), spec],
        out_specs=spec,
        compiler_params=_cparams(("parallel", "parallel"), 40),
        name="fft_stage1",
    )(m1, v.reshape(shape))


def _fft2_one(tr, ti, twc, tws, f, scale):
    reps = FOURIER_DIM // LANES
    c = jnp.concatenate([twc] * reps, axis=1)
    s = jnp.concatenate([tws] * reps, axis=1)
    pr = (tr * c + ti * s).astype(BF16)
    pi = (ti * c - tr * s).astype(BF16)
    return _dot(f, jnp.concatenate([pr, pi], axis=0)) * scale


def _fft2_kernel(t_ref, twc_ref, tws_ref, f_ref, o_ref, *, scale):
    f = f_ref[...]
    for j in range(SUBLANES):
        o_ref[:, j, :] = _fft2_one(t_ref[0, j], t_ref[1, j], twc_ref[j], tws_ref[j], f, scale)


def _fft2_single_kernel(t_ref, twc_ref, tws_ref, f_ref, o_ref, *, scale):
    o_ref[...] = _fft2_one(t_ref[0], t_ref[1], twc_ref[...], tws_ref[...], f_ref[...], scale)


def _fft2(t, twc, tws, f2, batch, n1, n2):
    t5 = t.reshape(2, batch, n1, n2, FOURIER_DIM)
    scale = 1.0 / math.sqrt(n1 * n2 * FOURIER_GROUP_DIM)
    f_spec = pl.BlockSpec((n2, 2 * n2), lambda b, k: (0, 0))
    if n1 == 1:
        out = pl.pallas_call(
            functools.partial(_fft2_single_kernel, scale=scale),
            out_shape=jax.ShapeDtypeStruct((batch, n2, FOURIER_DIM), F32),
            grid=(batch, 1),
            in_specs=[
                pl.BlockSpec((2, None, None, n2, FOURIER_DIM), lambda b, k: (0, b, 0, 0, 0)),
                pl.BlockSpec((None, n2, LANES), lambda b, k: (0, 0, 0)),
                pl.BlockSpec((None, n2, LANES), lambda b, k: (0, 0, 0)),
                f_spec,
            ],
            out_specs=pl.BlockSpec((None, n2, FOURIER_DIM), lambda b, k: (b, 0, 0)),
            compiler_params=_cparams(("parallel", "parallel"), 32),
            name=f"fft_stage2_{n1}x{n2}",
        )(t5, twc, tws, f2)
    else:
        out = pl.pallas_call(
            functools.partial(_fft2_kernel, scale=scale),
            out_shape=jax.ShapeDtypeStruct((batch, n2, n1 // SUBLANES, SUBLANES, FOURIER_DIM), F32),
            grid=(batch, n1 // SUBLANES),
            in_specs=[
                pl.BlockSpec((2, None, SUBLANES, n2, FOURIER_DIM), lambda b, k: (0, b, k, 0, 0)),
                pl.BlockSpec((SUBLANES, n2, LANES), lambda b, k: (k, 0, 0)),
                pl.BlockSpec((SUBLANES, n2, LANES), lambda b, k: (k, 0, 0)),
                f_spec,
            ],
            out_specs=pl.BlockSpec((None, n2, None, SUBLANES, FOURIER_DIM), lambda b, k: (b, 0, k, 0, 0)),
            compiler_params=_cparams(("parallel", "parallel"), 48),
            name=f"fft_stage2_{n1}x{n2}",
        )(t5, twc, tws, f2)
    return out.reshape(batch * n2 * n1, FOURIER_DIM)


def _mix_out_kernel(x_ref, gt_ref, gb_ref, v_ref, vp_ref, vn_ref, yb_ref, wc_ref, wo_ref,
                    lg_ref, lb_ref, *rest, tiles_per_seq):
    if len(rest) == 1:
        o_ref, = rest
    else:
        c_ref, wm_ref, bm_ref, o_ref, mod_ref = rest
        mod_ref[:, 0, :] = _mod_block(c_ref, wm_ref, bm_ref)
    i = pl.program_id(0)
    tm = v_ref.shape[0]
    pos = lax.rem(i, tiles_per_seq)
    v = v_ref[...]
    prev_row = jnp.where(pos == 0, 0.0, vp_ref[7:8, :])
    next_row = jnp.where(pos == tiles_per_seq - 1, 0.0, vn_ref[0:1, :])
    row = lax.broadcasted_iota(jnp.int32, v.shape, 0)
    v_prev = jnp.where(row == 0, prev_row, pltpu.roll(v, 1, 0))
    v_next = jnp.where(row == tm - 1, next_row, pltpu.roll(v, tm - 1, 0))
    conv = wc_ref[0:1, :] * v_prev + wc_ref[1:2, :] * v + wc_ref[2:3, :] * v_next
    y_a = (gb_ref[...] * conv).astype(BF16)
    for r in range(tm // PROJ_SLAB):
        rows = pl.ds(r * PROJ_SLAB, PROJ_SLAB)
        y = (_dot(y_a[r * PROJ_SLAB:(r + 1) * PROJ_SLAB], wo_ref[0:CONV_DIM, :])
             + _dot(yb_ref[rows, :].astype(BF16), wo_ref[CONV_DIM:, :]))
        z = ALPHA * x_ref[rows, :] + gt_ref[...] * y
        o_ref[rows, :] = _layer_norm_rows(z, lg_ref[...], lb_ref[...])


def _mix_out(x, mod, cond_of_tile, gb, v, yb, w_conv, w_out, ln_g, ln_b, tm, seq_len, next_mod=None):
    rows = x.shape[0]
    sub = 8
    v3 = v.reshape(rows // sub, sub, CONV_DIM)
    step = tm // sub
    last = rows // sub - 1
    steps = rows // tm
    in_specs = [
        pl.BlockSpec((tm, D_MODEL), lambda i: (i, 0)),
        _mod_spec(5, cond_of_tile),
        pl.BlockSpec((tm, CONV_DIM), lambda i: (i, 0)),
        pl.BlockSpec((tm, CONV_DIM), lambda i: (i, 0)),
        pl.BlockSpec((None, sub, CONV_DIM), lambda i: (jnp.maximum(i * step - 1, 0), 0, 0)),
        pl.BlockSpec((None, sub, CONV_DIM), lambda i: (jnp.minimum((i + 1) * step, last), 0, 0)),
        pl.BlockSpec((tm, FOURIER_DIM), lambda i: (i, 0)),
        pl.BlockSpec((None, 3, CONV_DIM), lambda i: (0, 0, 0)),
        pl.BlockSpec((D_MODEL, D_MODEL), lambda i: (0, 0), pipeline_mode=pl.Buffered(1)),
        _ln_spec(0, 1),
        _ln_spec(0, 1),
    ]
    out_shape = jax.ShapeDtypeStruct((rows, D_MODEL), F32)
    out_specs = pl.BlockSpec((tm, D_MODEL), lambda i: (i, 0))
    args = (x, mod, gb, v, v3, v3, yb, w_conv, w_out, ln_g, ln_b)
    semantics = ("parallel",)
    if next_mod is not None:
        cond, w_mod, b_mod3, layer = next_mod
        tn = MOD_RIDER_TN
        per_vec = D_MODEL // tn
        n_blocks = N_MOD * per_vec
        assert n_blocks <= steps
        blk = lambda i: jnp.minimum(i, n_blocks - 1)
        in_specs += [
            pl.BlockSpec((N_COND, D_MODEL), lambda i: (0, 0)),
            pl.BlockSpec((None, D_MODEL, tn), lambda i: (layer, 0, blk(i))),
            pl.BlockSpec((None, 1, tn), lambda i: (layer, 0, blk(i))),
        ]
        out_shape = (out_shape, jax.ShapeDtypeStruct((N_COND, N_MOD, 1, D_MODEL), F32))
        out_specs = (out_specs,
                     pl.BlockSpec((N_COND, None, 1, tn), lambda i: (0, blk(i) // per_vec, 0, blk(i) % per_vec)))
        args += (cond, w_mod, b_mod3)
        semantics = ("arbitrary",)
    return pl.pallas_call(
        functools.partial(_mix_out_kernel, tiles_per_seq=seq_len // tm),
        out_shape=out_shape,
        grid=(steps,),
        in_specs=in_specs,
        out_specs=out_specs,
        compiler_params=_cparams(semantics, 60),
        name=f"mix_out_{rows}",
    )(*args)


def _rope_block(blk, cos, sin, first_half):
    rot = jnp.where(first_half, pltpu.roll(blk, LANES - 16, 1), pltpu.roll(blk, 16, 1))
    return blk * cos + rot * sin


def _dup_heads(blk, low_half):
    swapped = pltpu.roll(blk, HEAD_DIM, 1)
    return jnp.where(low_half, blk, swapped), jnp.where(low_half, swapped, blk)


def _qkv_kernel(x_ref, sh_ref, sc_ref, w_ref, wvt_ref, cos_ref, sin_ref, q_ref, k_ref, vt_ref, *, rope):
    tm = x_ref.shape[0]
    xm = (x_ref[...] * (1.0 + sc_ref[...]) + sh_ref[...]).astype(BF16)
    lane = lax.broadcasted_iota(jnp.int32, (tm, LANES), 1)
    first_half = jnp.bitwise_and(lane, 31) < 16
    low_half = lane < HEAD_DIM
    n_q = N_Q_HEADS * HEAD_DIM
    n_kv = N_KV_HEADS * HEAD_DIM
    if rope:
        cos = cos_ref[...]
        sin = sin_ref[...]
    if q_ref is not None:
        q = _dot(xm, w_ref[:, 0:n_q])
        for j in range(n_q // LANES):
            lanes = slice(j * LANES, (j + 1) * LANES)
            blk = q[:, lanes]
            if rope:
                blk = _rope_block(blk, cos, sin, first_half)
            q_ref[:, lanes] = (blk * (LOG2_E * HEAD_DIM ** -0.5)).astype(BF16)
    k = _dot(xm, w_ref[:, n_q:n_q + n_kv])
    v_t = lax.dot_general(wvt_ref[...], xm, (((1,), (1,)), ((), ())),
                          preferred_element_type=F32).astype(BF16)
    for j in range(n_kv // LANES):
        kb = k[:, j * LANES:(j + 1) * LANES]
        if rope:
            kb = _rope_block(kb, cos, sin, first_half)
        k0, k1 = _dup_heads(kb, low_half)
        k_ref[:, 2 * j * LANES:(2 * j + 1) * LANES] = k0.astype(BF16)
        k_ref[:, (2 * j + 1) * LANES:(2 * j + 2) * LANES] = k1.astype(BF16)
    for g in range(N_KV_HEADS):
        head = v_t[g * HEAD_DIM:(g + 1) * HEAD_DIM, :]
        vt_ref[g * LANES:g * LANES + HEAD_DIM, :] = head
        vt_ref[g * LANES + HEAD_DIM:(g + 1) * LANES, :] = head


def _qkv_latent_kernel(x_ref, sh_ref, sc_ref, w_ref, wvt_ref, cos_ref, sin_ref, q_ref, k_ref, vt_ref):
    _qkv_kernel(x_ref, sh_ref, sc_ref, w_ref, wvt_ref, cos_ref, sin_ref, q_ref, k_ref, vt_ref, rope=True)


def _kv_context_kernel(x_ref, sh_ref, sc_ref, w_ref, wvt_ref, k_ref, vt_ref):
    _qkv_kernel(x_ref, sh_ref, sc_ref, w_ref, wvt_ref, None, None, None, k_ref, vt_ref, rope=False)


def _qkv_latent(x, mod, cond_of_tile, w_in, wv_t, cos_t, sin_t, tm, batch, seq_len):
    rows = x.shape[0]
    n_in = (N_Q_HEADS + 2 * N_KV_HEADS) * HEAD_DIM
    tps = seq_len // tm
    dup = N_KV_HEADS * LANES
    return pl.pallas_call(
        _qkv_latent_kernel,
        out_shape=(jax.ShapeDtypeStruct((rows, N_Q_HEADS * HEAD_DIM), BF16),
                   jax.ShapeDtypeStruct((rows, dup), BF16),
                   jax.ShapeDtypeStruct((batch, dup, seq_len), BF16)),
        grid=(rows // tm,),
        in_specs=[
            pl.BlockSpec((tm, D_MODEL), lambda i: (i, 0)),
            _mod_spec(3, cond_of_tile),
            _mod_spec(4, cond_of_tile),
            pl.BlockSpec((D_MODEL, n_in), lambda i: (0, 0), pipeline_mode=pl.Buffered(1)),
            pl.BlockSpec((N_KV_HEADS * HEAD_DIM, D_MODEL), lambda i: (0, 0)),
            pl.BlockSpec((tm, LANES), lambda i: (lax.rem(i, tps), 0)),
            pl.BlockSpec((tm, LANES), lambda i: (lax.rem(i, tps), 0)),
        ],
        out_specs=(pl.BlockSpec((tm, N_Q_HEADS * HEAD_DIM), lambda i: (i, 0)),
                   pl.BlockSpec((tm, dup), lambda i: (i, 0)),
                   pl.BlockSpec((None, dup, tm), lambda i: (i // tps, 0, lax.rem(i, tps)))),
        compiler_params=_cparams(("parallel",), 48),
        name="qkv_latent",
    )(x, mod, mod, w_in, wv_t, cos_t, sin_t)


def _kv_context(x, mod, cond_of_tile, w_in, wv_t, tm, batch):
    rows = x.shape[0]
    n_in = (N_Q_HEADS + 2 * N_KV_HEADS) * HEAD_DIM
    dup = N_KV_HEADS * LANES
    return pl.pallas_call(
        _kv_context_kernel,
        out_shape=(jax.ShapeDtypeStruct((rows, dup), BF16),
                   jax.ShapeDtypeStruct((batch, dup, tm), BF16)),
        grid=(rows // tm,),
        in_specs=[
            pl.BlockSpec((tm, D_MODEL), lambda i: (i, 0)),
            _mod_spec(3, cond_of_tile),
            _mod_spec(4, cond_of_tile),
            pl.BlockSpec((D_MODEL, n_in), lambda i: (0, 0), pipeline_mode=pl.Buffered(1)),
            pl.BlockSpec((N_KV_HEADS * HEAD_DIM, D_MODEL), lambda i: (0, 0)),
        ],
        out_specs=(pl.BlockSpec((tm, dup), lambda i: (i, 0)),
                   pl.BlockSpec((None, dup, tm), lambda i: (i, 0, 0))),
        compiler_params=_cparams(("parallel",), 40),
        name="kv_context",
    )(x, mod, mod, w_in, wv_t)


def _band_start(i, seq_len):
    return jnp.clip((i - 1) * BLOCK, 0, seq_len - 3 * BLOCK)


def _attn_kernel(sink_ref, q_ref, kl_ref, vlt_ref, kc_ref, vct_ref, bias_ref, o_ref, *, seq_len):
    i = pl.program_id(1)
    band = 3 * BLOCK
    n_ctx = kc_ref.shape[0]
    pairs = KV_REP // 2
    start = pl.multiple_of(_band_start(i, seq_len), BLOCK)
    low_half = lax.broadcasted_iota(jnp.int32, (BLOCK, LANES), 1) < HEAD_DIM
    top_half = lax.broadcasted_iota(jnp.int32, (LANES, BLOCK), 0) < HEAD_DIM

    def scores_t(g):
        kv = slice(g * LANES, (g + 1) * LANES)
        q_parts = []
        for p in range(pairs):
            blk = q_ref[:, (g * pairs + p) * LANES:(g * pairs + p + 1) * LANES]
            q_parts.append(jnp.where(low_half, blk, jnp.zeros_like(blk)))
            q_parts.append(jnp.where(low_half, jnp.zeros_like(blk), blk))
        q_g = jnp.concatenate(q_parts, axis=0)
        k_all = jnp.concatenate([kc_ref[:, kv], kl_ref[pl.ds(start, band), kv]], axis=0)
        return lax.dot_general(k_all, q_g, (((1,), (1,)), ((), ())), preferred_element_type=F32)

    pending = [scores_t(g) for g in range(ATTN_AHEAD)]
    for g in range(N_KV_HEADS):
        if g + ATTN_AHEAD < N_KV_HEADS:
            pending.append(scores_t(g + ATTN_AHEAD))
        kv = slice(g * LANES, (g + 1) * LANES)
        s = pending[g]
        sink = jnp.concatenate([jnp.full((1, BLOCK), sink_ref[g * KV_REP + h] * LOG2_E, F32)
                                for h in range(KV_REP)], axis=1)
        s_ctx = s[:n_ctx]
        s_loc = s[n_ctx:] + bias_ref[...]
        m = jnp.maximum(jnp.maximum(jnp.max(s_ctx, axis=0, keepdims=True),
                                    jnp.max(s_loc, axis=0, keepdims=True)), sink)
        p_ctx = jnp.exp2(s_ctx - m)
        p_loc = jnp.exp2(s_loc - m)
        denom = (jnp.sum(p_ctx, axis=0, keepdims=True) + jnp.sum(p_loc, axis=0, keepdims=True)
                 + jnp.exp2(sink - m))
        probs_t = jnp.concatenate([p_ctx, p_loc], axis=0).astype(BF16)
        vt_all = jnp.concatenate([vct_ref[kv, :], vlt_ref[kv, pl.ds(start, band)]], axis=1)
        o_t = _dot(vt_all, probs_t) / denom
        for p in range(pairs):
            lo = o_t[:, 2 * p * BLOCK:(2 * p + 1) * BLOCK]
            hi = o_t[:, (2 * p + 1) * BLOCK:(2 * p + 2) * BLOCK]
            pair_t = jnp.where(top_half, lo, hi)
            o_ref[:, (g * pairs + p) * LANES:(g * pairs + p + 1) * LANES] = pair_t.T.astype(BF16)


def _window_bias(seq_len):
    band = 3 * BLOCK
    cols = KV_REP * BLOCK
    n_blk = seq_len // BLOCK
    blocks = jnp.array([0, 1, n_blk - 1], jnp.int32)
    delta = (_band_start(blocks, seq_len) - blocks * BLOCK)[:, None, None]
    j = jnp.arange(band, dtype=jnp.int32)[None, :, None]
    a = jnp.bitwise_and(jnp.arange(cols, dtype=jnp.int32), BLOCK - 1)[None, None, :]
    return jnp.where(jnp.abs(j + delta - a) <= WINDOW, 0.0, NEG_INF).astype(F32)


def _attention(sink, q, k_l, vt_l, k_c, vt_c, batch, seq_len, n_ctx):
    n_blk = seq_len // BLOCK
    assert n_blk >= 3
    dup = N_KV_HEADS * LANES
    n_q = N_Q_HEADS * HEAD_DIM
    band = 3 * BLOCK

    def bias_case(b, i):
        return (jnp.where(i == 0, 0, jnp.where(i == n_blk - 1, 2, 1)), 0, 0)

    return pl.pallas_call(
        functools.partial(_attn_kernel, seq_len=seq_len),
        out_shape=jax.ShapeDtypeStruct((batch * seq_len, n_q), BF16),
        grid=(batch, n_blk),
        in_specs=[
            pl.BlockSpec(memory_space=pltpu.SMEM),
            pl.BlockSpec((BLOCK, n_q), lambda b, i: (b * n_blk + i, 0)),
            pl.BlockSpec((None, seq_len, dup), lambda b, i: (b, 0, 0)),
            pl.BlockSpec((None, dup, seq_len), lambda b, i: (b, 0, 0)),
            pl.BlockSpec((None, n_ctx, dup), lambda b, i: (b, 0, 0)),
            pl.BlockSpec((None, dup, n_ctx), lambda b, i: (b, 0, 0)),
            pl.BlockSpec((None, band, KV_REP * BLOCK), bias_case),
        ],
        out_specs=pl.BlockSpec((BLOCK, n_q), lambda b, i: (b * n_blk + i, 0)),
        compiler_params=_cparams(("parallel", "arbitrary"), 56),
        name="window_attention",
    )(sink, q, k_l.reshape(batch, seq_len, dup), vt_l, k_c.reshape(batch, n_ctx, dup), vt_c,
      _window_bias(seq_len))


def _attn_out_kernel(x_ref, gt_ref, a_ref, wo_ref, lg_ref, lb_ref, o_ref):
    for r in range(x_ref.shape[0] // PROJ_SLAB):
        rows = pl.ds(r * PROJ_SLAB, PROJ_SLAB)
        y = _dot(a_ref[rows, :], wo_ref[...])
        z = ALPHA * x_ref[rows, :] + gt_ref[...] * y
        o_ref[rows, :] = _layer_norm_rows(z, lg_ref[...], lb_ref[...])


def _attn_out(x, mod, cond_of_tile, a, w_out, ln_g, ln_b, tm):
    rows = x.shape[0]
    return pl.pallas_call(
        _attn_out_kernel,
        out_shape=jax.ShapeDtypeStruct((rows, D_MODEL), F32),
        grid=(rows // tm,),
        in_specs=[
            pl.BlockSpec((tm, D_MODEL), lambda i: (i, 0)),
            _mod_spec(5, cond_of_tile),
            pl.BlockSpec((tm, D_MODEL), lambda i: (i, 0)),
            pl.BlockSpec((D_MODEL, D_MODEL), lambda i: (0, 0), pipeline_mode=pl.Buffered(1)),
            _ln_spec(1, 1),
            _ln_spec(1, 1),
        ],
        out_specs=pl.BlockSpec((tm, D_MODEL), lambda i: (i, 0)),
        compiler_params=_cparams(("parallel",), 56),
        name="attn_out",
    )(x, mod, a, w_out, ln_g, ln_b)


def _dft_cos_sin(n):
    k = jnp.arange(n, dtype=jnp.int32)
    ang = ((k[:, None] * k[None, :]) % n).astype(F32) * (2.0 * math.pi / n)
    return jnp.cos(ang), jnp.sin(ang)


def _twiddles(n1, n2):
    k1 = jnp.arange(n1, dtype=jnp.int32)[:, None]
    m2 = jnp.arange(n2, dtype=jnp.int32)[None, :]
    n = n1 * n2
    ang = ((k1 * m2) % n).astype(F32) * (2.0 * math.pi / n)
    shape = (n1, n2, LANES)
    return (jnp.broadcast_to(jnp.cos(ang)[:, :, None], shape),
            jnp.broadcast_to(jnp.sin(ang)[:, :, None], shape))


def _rope_lane_tables(seq_len):
    rows = seq_len // GRID_W
    row = jnp.repeat(jnp.arange(rows, dtype=F32), GRID_W)
    col = jnp.tile(jnp.arange(GRID_W, dtype=F32), rows)
    n_freq = HEAD_DIM // 4
    inv_freq = jnp.power(ROPE_BASE, -jnp.arange(n_freq, dtype=F32) / n_freq)
    ang_r = row[:, None] * inv_freq
    ang_c = col[:, None] * inv_freq
    cos_h = jnp.concatenate([jnp.cos(ang_r), jnp.cos(ang_r), jnp.cos(ang_c), jnp.cos(ang_c)], axis=-1)
    sin_h = jnp.concatenate([-jnp.sin(ang_r), jnp.sin(ang_r), -jnp.sin(ang_c), jnp.sin(ang_c)], axis=-1)
    reps = LANES // HEAD_DIM
    return jnp.tile(cos_h, (1, reps)), jnp.tile(sin_h, (1, reps))


def kernel(x, c, ctx, c_ctx, w_mod, b_mod, ln_g, ln_b, ffn_w_gate, ffn_w_up, ffn_w_down,
           ab_w_in, ab_conv, ab_w_out, attn_w_in, attn_sink, attn_w_out):
    batch, seq_len, _ = x.shape
    n_ctx = ctx.shape[1]
    assert batch == 2 and seq_len % FFN_TM == 0 and seq_len % (FFT_N1 * 8) == 0

    xl = x.reshape(batch * seq_len, D_MODEL)
    xc = ctx.reshape(batch * n_ctx, D_MODEL)
    ctx_rows = batch * n_ctx

    cond = jnp.concatenate([c, c_ctx[None, :], jnp.zeros((N_COND - batch - 1, D_MODEL), F32)], axis=0)
    b_mod3 = b_mod.reshape(DEPTH, 1, N_MOD * D_MODEL)
    mods = [_modulation(cond, w_mod, b_mod3, 0), None]
    ln_g4 = ln_g.reshape(DEPTH, 3, 1, D_MODEL)
    ln_b4 = ln_b.reshape(DEPTH, 3, 1, D_MODEL)

    ffn_f32 = (ffn_w_gate, ffn_w_up, ffn_w_down)
    first_weights = tuple(w[0, 0].astype(BF16) for w in ffn_f32)
    mixer_f32 = (ab_w_in, ab_w_out, attn_w_in, attn_w_out)

    def lat_cond(tm):
        tiles = seq_len // tm
        return lambda i: i // tiles

    ctx_cond = lambda i: CTX_COND

    def ffn_both(xl, xc, layer, sub, weights, with_ctx, extra_jobs=()):
        nxt = (layer, sub + 1) if sub == 0 else (layer + 1, 0)
        jobs = tuple((w, nxt) for w in ffn_f32) if nxt[0] < DEPTH else ()
        xl, copies = _ffn(xl, mods[layer], layer, sub, lat_cond(FFN_TM), weights, ln_g4, ln_b4, FFN_TM,
                          jobs + tuple(extra_jobs))
        if with_ctx:
            xc, _ = _ffn(xc, mods[layer], layer, sub, ctx_cond, weights, ln_g4, ln_b4, ctx_rows, tf=FFN_TF_CTX)
        return xl, xc, copies[:len(jobs)], copies[len(jobs):]

    xl, xc, weights, (ab_in, ab_out, at_in, at_out) = ffn_both(
        xl, xc, 0, 0, first_weights, True, [(w, (0,)) for w in mixer_f32])

    cc, sc_ = _dft_cos_sin(FOURIER_GROUP_DIM)
    cs_chan = jnp.concatenate([cc, -sc_], axis=1).astype(BF16)
    n1, n2 = FFT_N1, seq_len // FFT_N1
    c1, s1 = _dft_cos_sin(n1)
    m1 = jnp.concatenate([jnp.concatenate([c1, s1], axis=1),
                          jnp.concatenate([-s1, c1], axis=1)], axis=0).astype(BF16)
    c2, s2 = _dft_cos_sin(n2)
    f2 = jnp.concatenate([c2, s2], axis=1).astype(BF16)
    twc, tws = _twiddles(n1, n2)
    cn, sn = _dft_cos_sin(n_ctx)
    f2_ctx = jnp.concatenate([cn, sn], axis=1).astype(BF16)
    twc_ctx, tws_ctx = _twiddles(1, n_ctx)

    gb, v, vf = _mix_in(xl, mods[0], lat_cond(PROJ_TM), ab_in, cs_chan, PROJ_TM)
    t = _fft1(vf, m1, batch, n1, n2)
    yb = _fft2(t, twc, tws, f2, batch, n1, n2)
    xl, mods[1] = _mix_out(xl, mods[0], lat_cond(PROJ_TM), gb, v, yb, ab_conv, ab_out, ln_g4, ln_b4,
                           PROJ_TM, seq_len, next_mod=(cond, w_mod, b_mod3, 1))

    gb_c, v_c, vf_c = _mix_in(xc, mods[0], ctx_cond, ab_in, cs_chan, n_ctx)
    yb_c = _fft2(vf_c, twc_ctx, tws_ctx, f2_ctx, batch, 1, n_ctx)
    xc = _mix_out(xc, mods[0], ctx_cond, gb_c, v_c, yb_c, ab_conv, ab_out, ln_g4, ln_b4, n_ctx, n_ctx)

    xl, xc, weights, _ = ffn_both(xl, xc, 0, 1, weights, True)

    xl, xc, weights, _ = ffn_both(xl, xc, 1, 0, weights, True)

    cos_t, sin_t = _rope_lane_tables(seq_len)
    n_qk = (N_Q_HEADS + N_KV_HEADS) * HEAD_DIM
    wv_t = jnp.transpose(attn_w_in[0, :, n_qk:]).astype(BF16)
    q, k_l, vt_l = _qkv_latent(xl, mods[1], lat_cond(PROJ_TM), at_in, wv_t, cos_t, sin_t, PROJ_TM, batch, seq_len)
    k_c, vt_c = _kv_context(xc, mods[1], ctx_cond, at_in, wv_t, n_ctx, batch)
    att = _attention(attn_sink.reshape(N_Q_HEADS), q, k_l, vt_l, k_c, vt_c, batch, seq_len, n_ctx)
    xl = _attn_out(xl, mods[1], lat_cond(ATTN_OUT_TM), att, at_out, ln_g4, ln_b4, ATTN_OUT_TM)

    xl, _, _, _ = ffn_both(xl, xc, 1, 1, weights, False)
    return xl.reshape(batch, seq_len, D_MODEL)
```

```python
import functools
import math

import jax
import jax.numpy as jnp
from jax import lax
from jax.experimental import pallas as pl
from jax.experimental.pallas import tpu as pltpu

F32 = jnp.float32
BF16 = jnp.bfloat16

D_MODEL = 2048
GRID_W = 64
N_MOD = 9
D_FF = 5632
CONV_DIM = 1024
FOURIER_DIM = 1024
FOURIER_GROUPS = 8
FOURIER_GROUP_DIM = 128
HEAD_DIM = 64
N_Q_HEADS = 32
N_KV_HEADS = 4
KV_REP = 8
WINDOW = 128
BLOCK = 128
ROPE_BASE = 10000.0
LN_EPS = 1e-5
NEG_INF = -1e30
DEPTH = 2
ALPHA = (2 * DEPTH) ** 0.25
LOG2_E = math.log2(math.e)

LANES = 128
SUBLANES = 8
BF16_SUBLANES = 16
V7X_VMEM_BYTES = 64 * 1024 * 1024
MIB = 1024 * 1024

N_COND = 8
CTX_COND = 2
FFN_TM = 1024
FFN_TF = 512
FFN_SLAB = 256
PROJ_TM = 512
PROJ_SLAB = 256
ATTN_AHEAD = 2
FFT_N1 = 64
FFT1_GROUPS = 2


def _cparams(semantics, vmem_mib):
    return pltpu.CompilerParams(dimension_semantics=semantics,
                                vmem_limit_bytes=min(vmem_mib * MIB, V7X_VMEM_BYTES))


def _dot(a, b):
    return jnp.dot(a, b, preferred_element_type=F32)


def _layer_norm_rows(z, g, b):
    mu = jnp.mean(z, axis=-1, keepdims=True)
    zc = z - mu
    var = jnp.mean(zc * zc, axis=-1, keepdims=True)
    return zc * lax.rsqrt(var + LN_EPS) * g + b


def _mod_kernel(c_ref, w_ref, b_ref, o_ref):
    c = c_ref[...]
    a = (c * jax.nn.sigmoid(c)).astype(BF16)
    o_ref[:, 0, :] = _dot(a, w_ref[...].astype(BF16)) + b_ref[...]


def _modulation(cond, w_mod, b_mod):
    n = N_MOD * D_MODEL
    return pl.pallas_call(
        _mod_kernel,
        out_shape=jax.ShapeDtypeStruct((DEPTH, N_COND, N_MOD, 1, D_MODEL), F32),
        grid=(DEPTH, N_MOD),
        in_specs=[
            pl.BlockSpec((N_COND, D_MODEL), lambda l, j: (0, 0)),
            pl.BlockSpec((None, D_MODEL, D_MODEL), lambda l, j: (l, 0, j)),
            pl.BlockSpec((None, 1, D_MODEL), lambda l, j: (l, 0, j)),
        ],
        out_specs=pl.BlockSpec((None, N_COND, None, 1, D_MODEL), lambda l, j: (l, 0, j, 0, 0)),
        compiler_params=_cparams(("parallel", "parallel"), 48),
        name="modulation",
    )(cond, w_mod, b_mod.reshape(DEPTH, 1, n))


def _mod_spec(layer, k, cond_of_tile):
    return pl.BlockSpec((None, None, None, 1, D_MODEL),
                        lambda i, *_: (layer, cond_of_tile(i), k, 0, 0))


def _ln_spec(layer, k):
    return pl.BlockSpec((None, None, 1, D_MODEL), lambda i, *_: (layer, k, 0, 0))


def _ffn_kernel(x_ref, sh_ref, sc_ref, gt_ref, wg_ref, wu_ref, wd_ref, lg_ref, lb_ref, *rest):
    n_cast = (len(rest) - 2) // 2
    o_ref, xm_ref = rest[n_cast], rest[-1]
    cast_pairs = tuple(zip(rest[:n_cast], rest[n_cast + 1:-1]))
    f = pl.program_id(1)
    last = pl.num_programs(1) - 1
    slabs = [pl.ds(r * FFN_SLAB, FFN_SLAB) for r in range(x_ref.shape[0] // FFN_SLAB)]
    half_gate = 0.5 * gt_ref[...]

    def cast_next_slice():
        for src, dst in cast_pairs:
            dst[...] = src[...].astype(BF16)

    def swiglu_chunk(rows):
        xm = xm_ref[rows, :]
        h_gate = _dot(xm, wg_ref[...])
        h_up = _dot(xm, wu_ref[...])
        act = (h_gate * jax.nn.sigmoid(h_gate) * h_up).astype(BF16)
        return half_gate * _dot(act, wd_ref[...])

    @pl.when(f == 0)
    def _():
        cast_next_slice()
        scale1 = 1.0 + sc_ref[...]
        shift = sh_ref[...]
        for rows in slabs:
            x = x_ref[rows, :]
            xm_ref[rows, :] = (x * scale1 + shift).astype(BF16)
            o_ref[rows, :] = ALPHA * x + swiglu_chunk(rows)

    @pl.when(jnp.logical_and(f > 0, f < last))
    def _():
        cast_next_slice()
        o_ref[...] += swiglu_chunk(slice(None))

    @pl.when(f == last)
    def _():
        cast_next_slice()
        g = lg_ref[...]
        b = lb_ref[...]
        for rows in slabs:
            o_ref[rows, :] = _layer_norm_rows(o_ref[rows, :] + swiglu_chunk(rows), g, b)


def _cast_specs(src, lead, n_f, n_steps):
    n_rows, n_cols = src.shape[-2:]
    rows = BF16_SUBLANES
    while n_rows % rows or n_rows // rows > n_steps:
        rows += BF16_SUBLANES
    last_block = n_rows // rows - 1
    step = lambda i, f: jnp.minimum(i * n_f + f, last_block)
    in_spec = pl.BlockSpec((None,) * len(lead) + (rows, n_cols), lambda i, f: (*lead, step(i, f), 0))
    out_spec = pl.BlockSpec((rows, n_cols), lambda i, f: (step(i, f), 0))
    return in_spec, out_spec, jax.ShapeDtypeStruct((n_rows, n_cols), BF16)


def _ffn(x, mod, layer, sub, cond_of_tile, weights, ln_g, ln_b, tm, cast_jobs=()):
    rows = x.shape[0]
    k0 = 6 * sub
    n_f = D_FF // FFN_TF
    n_steps = (rows // tm) * n_f
    w_in_spec = pl.BlockSpec((D_MODEL, FFN_TF), lambda i, f: (0, f))
    in_specs = [
        pl.BlockSpec((tm, D_MODEL), lambda i, f: (i, 0)),
        _mod_spec(layer, k0, cond_of_tile),
        _mod_spec(layer, k0 + 1, cond_of_tile),
        _mod_spec(layer, k0 + 2, cond_of_tile),
        w_in_spec,
        w_in_spec,
        pl.BlockSpec((FFN_TF, D_MODEL), lambda i, f: (f, 0)),
        _ln_spec(layer, 2 * sub),
        _ln_spec(layer, 2 * sub),
    ]
    out_shape = jax.ShapeDtypeStruct((rows, D_MODEL), F32)
    out_specs = pl.BlockSpec((tm, D_MODEL), lambda i, f: (i, 0))
    args = (x, mod, mod, mod, *weights, ln_g, ln_b)
    semantics = ("parallel", "arbitrary")
    if cast_jobs:
        specs = [_cast_specs(src, lead, n_f, n_steps) for src, lead in cast_jobs]
        in_specs += [s[0] for s in specs]
        out_specs = (out_specs, *[s[1] for s in specs])
        out_shape = (out_shape, *[s[2] for s in specs])
        args += tuple(src for src, _ in cast_jobs)
        semantics = ("arbitrary", "arbitrary")
    out = pl.pallas_call(
        _ffn_kernel,
        out_shape=out_shape,
        grid=(rows // tm, n_f),
        in_specs=in_specs,
        out_specs=out_specs,
        scratch_shapes=[pltpu.VMEM((tm, D_MODEL), BF16)],
        compiler_params=_cparams(semantics, 60),
        name=f"ffn_l{layer}_s{sub}_{rows}",
    )(*args)
    if not cast_jobs:
        return out, ()
    return out[0], tuple(out[1:])


def _mix_in_kernel(x_ref, sh_ref, sc_ref, w_ref, cs_ref, gb_ref, v_ref, f_ref):
    xm = (x_ref[...] * (1.0 + sc_ref[...]) + sh_ref[...]).astype(BF16)
    gb_ref[...] = _dot(xm, w_ref[:, 0:CONV_DIM])
    g_c = _dot(xm, w_ref[:, CONV_DIM:2 * CONV_DIM])
    x_in = _dot(xm, w_ref[:, 2 * CONV_DIM:3 * CONV_DIM])
    v_ref[...] = g_c * x_in
    u_f = _dot(xm, w_ref[:, 3 * CONV_DIM:]).astype(BF16)
    cs = cs_ref[...]
    for g in range(FOURIER_GROUPS):
        lanes = slice(g * FOURIER_GROUP_DIM, (g + 1) * FOURIER_GROUP_DIM)
        res = _dot(u_f[:, lanes], cs)
        f_ref[0, :, lanes] = res[:, :FOURIER_GROUP_DIM]
        f_ref[1, :, lanes] = res[:, FOURIER_GROUP_DIM:]


def _mix_in(x, mod, cond_of_tile, w_in, cs, tm):
    rows = x.shape[0]
    n_in = 3 * CONV_DIM + FOURIER_DIM
    return pl.pallas_call(
        _mix_in_kernel,
        out_shape=(jax.ShapeDtypeStruct((rows, CONV_DIM), F32),
                   jax.ShapeDtypeStruct((rows, CONV_DIM), F32),
                   jax.ShapeDtypeStruct((2, rows, FOURIER_DIM), F32)),
        grid=(rows // tm,),
        in_specs=[
            pl.BlockSpec((tm, D_MODEL), lambda i: (i, 0)),
            _mod_spec(0, 3, cond_of_tile),
            _mod_spec(0, 4, cond_of_tile),
            pl.BlockSpec((D_MODEL, n_in), lambda i: (0, 0), pipeline_mode=pl.Buffered(1)),
            pl.BlockSpec((FOURIER_GROUP_DIM, 2 * FOURIER_GROUP_DIM), lambda i: (0, 0)),
        ],
        out_specs=(pl.BlockSpec((tm, CONV_DIM), lambda i: (i, 0)),
                   pl.BlockSpec((tm, CONV_DIM), lambda i: (i, 0)),
                   pl.BlockSpec((2, tm, FOURIER_DIM), lambda i: (0, i, 0))),
        compiler_params=_cparams(("parallel",), 56),
        name=f"mix_in_{rows}",
    )(x, mod, mod, w_in, cs)


def _fft1_kernel(m_ref, v_ref, o_ref):
    n1 = v_ref.shape[1]
    m = m_ref[...]
    for q in range(v_ref.shape[2]):
        for j in range(SUBLANES):
            v = jnp.concatenate([v_ref[0, :, q, j, :], v_ref[1, :, q, j, :]], axis=0).astype(BF16)
            t = _dot(m, v)
            o_ref[0, :, q, j, :] = t[:n1]
            o_ref[1, :, q, j, :] = t[n1:]


def _fft1(v, m1, batch, n1, n2):
    groups = n2 // SUBLANES
    shape = (2, batch, n1, groups, SUBLANES, FOURIER_DIM)
    spec = pl.BlockSpec((2, None, n1, FFT1_GROUPS, SUBLANES, FOURIER_DIM), lambda b, j: (0, b, 0, j, 0, 0))
    return pl.pallas_call(
        _fft1_kernel,
        out_shape=jax.ShapeDtypeStruct(shape, F32),
        grid=(batch, groups // FFT1_GROUPS),
        in_specs=[pl.BlockSpec((2 * n1, 2 * n1), lambda b, j: (0, 0)), spec],
        out_specs=spec,
        compiler_params=_cparams(("parallel", "parallel"), 48),
        name="fft_stage1",
    )(m1, v.reshape(shape))


def _fft2_one(tr, ti, twc, tws, f, scale):
    reps = FOURIER_DIM // LANES
    c = jnp.concatenate([twc] * reps, axis=1)
    s = jnp.concatenate([tws] * reps, axis=1)
    pr = (tr * c + ti * s).astype(BF16)
    pi = (ti * c - tr * s).astype(BF16)
    return _dot(f, jnp.concatenate([pr, pi], axis=0)) * scale


def _fft2_kernel(t_ref, twc_ref, tws_ref, f_ref, o_ref, *, scale):
    f = f_ref[...]
    for j in range(SUBLANES):
        o_ref[:, j, :] = _fft2_one(t_ref[0, j], t_ref[1, j], twc_ref[j], tws_ref[j], f, scale)


def _fft2_single_kernel(t_ref, twc_ref, tws_ref, f_ref, o_ref, *, scale):
    o_ref[...] = _fft2_one(t_ref[0], t_ref[1], twc_ref[...], tws_ref[...], f_ref[...], scale)


def _fft2(t, twc, tws, f2, batch, n1, n2):
    t5 = t.reshape(2, batch, n1, n2, FOURIER_DIM)
    scale = 1.0 / math.sqrt(n1 * n2 * FOURIER_GROUP_DIM)
    f_spec = pl.BlockSpec((n2, 2 * n2), lambda b, k: (0, 0))
    if n1 == 1:
        out = pl.pallas_call(
            functools.partial(_fft2_single_kernel, scale=scale),
            out_shape=jax.ShapeDtypeStruct((batch, n2, FOURIER_DIM), F32),
            grid=(batch, 1),
            in_specs=[
                pl.BlockSpec((2, None, None, n2, FOURIER_DIM), lambda b, k: (0, b, 0, 0, 0)),
                pl.BlockSpec((None, n2, LANES), lambda b, k: (0, 0, 0)),
                pl.BlockSpec((None, n2, LANES), lambda b, k: (0, 0, 0)),
                f_spec,
            ],
            out_specs=pl.BlockSpec((None, n2, FOURIER_DIM), lambda b, k: (b, 0, 0)),
            compiler_params=_cparams(("parallel", "parallel"), 32),
            name=f"fft_stage2_{n1}x{n2}",
        )(t5, twc, tws, f2)
    else:
        out = pl.pallas_call(
            functools.partial(_fft2_kernel, scale=scale),
            out_shape=jax.ShapeDtypeStruct((batch, n2, n1 // SUBLANES, SUBLANES, FOURIER_DIM), F32),
            grid=(batch, n1 // SUBLANES),
            in_specs=[
                pl.BlockSpec((2, None, SUBLANES, n2, FOURIER_DIM), lambda b, k: (0, b, k, 0, 0)),
                pl.BlockSpec((SUBLANES, n2, LANES), lambda b, k: (k, 0, 0)),
                pl.BlockSpec((SUBLANES, n2, LANES), lambda b, k: (k, 0, 0)),
                f_spec,
            ],
            out_specs=pl.BlockSpec((None, n2, None, SUBLANES, FOURIER_DIM), lambda b, k: (b, 0, k, 0, 0)),
            compiler_params=_cparams(("parallel", "parallel"), 48),
            name=f"fft_stage2_{n1}x{n2}",
        )(t5, twc, tws, f2)
    return out.reshape(batch * n2 * n1, FOURIER_DIM)


def _mix_out_kernel(x_ref, gt_ref, gb_ref, v_ref, vp_ref, vn_ref, yb_ref, wc_ref, wo_ref,
                    lg_ref, lb_ref, o_ref, *, tiles_per_seq):
    i = pl.program_id(0)
    tm = v_ref.shape[0]
    pos = lax.rem(i, tiles_per_seq)
    v = v_ref[...]
    prev_row = jnp.where(pos == 0, 0.0, vp_ref[7:8, :])
    next_row = jnp.where(pos == tiles_per_seq - 1, 0.0, vn_ref[0:1, :])
    row = lax.broadcasted_iota(jnp.int32, v.shape, 0)
    v_prev = jnp.where(row == 0, prev_row, pltpu.roll(v, 1, 0))
    v_next = jnp.where(row == tm - 1, next_row, pltpu.roll(v, tm - 1, 0))
    conv = wc_ref[0:1, :] * v_prev + wc_ref[1:2, :] * v + wc_ref[2:3, :] * v_next
    y_a = (gb_ref[...] * conv).astype(BF16)
    for r in range(tm // PROJ_SLAB):
        rows = pl.ds(r * PROJ_SLAB, PROJ_SLAB)
        y = (_dot(y_a[r * PROJ_SLAB:(r + 1) * PROJ_SLAB], wo_ref[0:CONV_DIM, :])
             + _dot(yb_ref[rows, :].astype(BF16), wo_ref[CONV_DIM:, :]))
        z = ALPHA * x_ref[rows, :] + gt_ref[...] * y
        o_ref[rows, :] = _layer_norm_rows(z, lg_ref[...], lb_ref[...])


def _mix_out(x, mod, cond_of_tile, gb, v, yb, w_conv, w_out, ln_g, ln_b, tm, seq_len):
    rows = x.shape[0]
    sub = 8
    v3 = v.reshape(rows // sub, sub, CONV_DIM)
    step = tm // sub
    last = rows // sub - 1
    return pl.pallas_call(
        functools.partial(_mix_out_kernel, tiles_per_seq=seq_len // tm),
        out_shape=jax.ShapeDtypeStruct((rows, D_MODEL), F32),
        grid=(rows // tm,),
        in_specs=[
            pl.BlockSpec((tm, D_MODEL), lambda i: (i, 0)),
            _mod_spec(0, 5, cond_of_tile),
            pl.BlockSpec((tm, CONV_DIM), lambda i: (i, 0)),
            pl.BlockSpec((tm, CONV_DIM), lambda i: (i, 0)),
            pl.BlockSpec((None, sub, CONV_DIM), lambda i: (jnp.maximum(i * step - 1, 0), 0, 0)),
            pl.BlockSpec((None, sub, CONV_DIM), lambda i: (jnp.minimum((i + 1) * step, last), 0, 0)),
            pl.BlockSpec((tm, FOURIER_DIM), lambda i: (i, 0)),
            pl.BlockSpec((None, 3, CONV_DIM), lambda i: (0, 0, 0)),
            pl.BlockSpec((D_MODEL, D_MODEL), lambda i: (0, 0), pipeline_mode=pl.Buffered(1)),
            _ln_spec(0, 1),
            _ln_spec(0, 1),
        ],
        out_specs=pl.BlockSpec((tm, D_MODEL), lambda i: (i, 0)),
        compiler_params=_cparams(("parallel",), 56),
        name=f"mix_out_{rows}",
    )(x, mod, gb, v, v3, v3, yb, w_conv, w_out, ln_g, ln_b)


def _rope_block(blk, cos, sin, first_half):
    rot = jnp.where(first_half, pltpu.roll(blk, LANES - 16, 1), pltpu.roll(blk, 16, 1))
    return blk * cos + rot * sin


def _dup_heads(blk, low_half):
    swapped = pltpu.roll(blk, HEAD_DIM, 1)
    return jnp.where(low_half, blk, swapped), jnp.where(low_half, swapped, blk)


def _qkv_kernel(x_ref, sh_ref, sc_ref, w_ref, wvt_ref, cos_ref, sin_ref, q_ref, k_ref, vt_ref, *, rope):
    tm = x_ref.shape[0]
    xm = (x_ref[...] * (1.0 + sc_ref[...]) + sh_ref[...]).astype(BF16)
    lane = lax.broadcasted_iota(jnp.int32, (tm, LANES), 1)
    first_half = jnp.bitwise_and(lane, 31) < 16
    low_half = lane < HEAD_DIM
    n_q = N_Q_HEADS * HEAD_DIM
    n_kv = N_KV_HEADS * HEAD_DIM
    if rope:
        cos = cos_ref[...]
        sin = sin_ref[...]
    if q_ref is not None:
        q = _dot(xm, w_ref[:, 0:n_q])
        for j in range(n_q // LANES):
            lanes = slice(j * LANES, (j + 1) * LANES)
            blk = q[:, lanes]
            if rope:
                blk = _rope_block(blk, cos, sin, first_half)
            q_ref[:, lanes] = (blk * (LOG2_E * HEAD_DIM ** -0.5)).astype(BF16)
    k = _dot(xm, w_ref[:, n_q:n_q + n_kv])
    v_t = lax.dot_general(wvt_ref[...], xm, (((1,), (1,)), ((), ())),
                          preferred_element_type=F32).astype(BF16)
    for j in range(n_kv // LANES):
        kb = k[:, j * LANES:(j + 1) * LANES]
        if rope:
            kb = _rope_block(kb, cos, sin, first_half)
        k0, k1 = _dup_heads(kb, low_half)
        k_ref[:, 2 * j * LANES:(2 * j + 1) * LANES] = k0.astype(BF16)
        k_ref[:, (2 * j + 1) * LANES:(2 * j + 2) * LANES] = k1.astype(BF16)
    for g in range(N_KV_HEADS):
        head = v_t[g * HEAD_DIM:(g + 1) * HEAD_DIM, :]
        vt_ref[g * LANES:g * LANES + HEAD_DIM, :] = head
        vt_ref[g * LANES + HEAD_DIM:(g + 1) * LANES, :] = head


def _qkv_latent_kernel(x_ref, sh_ref, sc_ref, w_ref, wvt_ref, cos_ref, sin_ref, q_ref, k_ref, vt_ref):
    _qkv_kernel(x_ref, sh_ref, sc_ref, w_ref, wvt_ref, cos_ref, sin_ref, q_ref, k_ref, vt_ref, rope=True)


def _kv_context_kernel(x_ref, sh_ref, sc_ref, w_ref, wvt_ref, k_ref, vt_ref):
    _qkv_kernel(x_ref, sh_ref, sc_ref, w_ref, wvt_ref, None, None, None, k_ref, vt_ref, rope=False)


def _qkv_latent(x, mod, cond_of_tile, w_in, wv_t, cos_t, sin_t, tm, batch, seq_len):
    rows = x.shape[0]
    n_in = (N_Q_HEADS + 2 * N_KV_HEADS) * HEAD_DIM
    tps = seq_len // tm
    dup = N_KV_HEADS * LANES
    return pl.pallas_call(
        _qkv_latent_kernel,
        out_shape=(jax.ShapeDtypeStruct((rows, N_Q_HEADS * HEAD_DIM), BF16),
                   jax.ShapeDtypeStruct((rows, dup), BF16),
                   jax.ShapeDtypeStruct((batch, dup, seq_len), BF16)),
        grid=(rows // tm,),
        in_specs=[
            pl.BlockSpec((tm, D_MODEL), lambda i: (i, 0)),
            _mod_spec(1, 3, cond_of_tile),
            _mod_spec(1, 4, cond_of_tile),
            pl.BlockSpec((D_MODEL, n_in), lambda i: (0, 0), pipeline_mode=pl.Buffered(1)),
            pl.BlockSpec((N_KV_HEADS * HEAD_DIM, D_MODEL), lambda i: (0, 0)),
            pl.BlockSpec((tm, LANES), lambda i: (lax.rem(i, tps), 0)),
            pl.BlockSpec((tm, LANES), lambda i: (lax.rem(i, tps), 0)),
        ],
        out_specs=(pl.BlockSpec((tm, N_Q_HEADS * HEAD_DIM), lambda i: (i, 0)),
                   pl.BlockSpec((tm, dup), lambda i: (i, 0)),
                   pl.BlockSpec((None, dup, tm), lambda i: (i // tps, 0, lax.rem(i, tps)))),
        compiler_params=_cparams(("parallel",), 48),
        name="qkv_latent",
    )(x, mod, mod, w_in, wv_t, cos_t, sin_t)


def _kv_context(x, mod, cond_of_tile, w_in, wv_t, tm, batch):
    rows = x.shape[0]
    n_in = (N_Q_HEADS + 2 * N_KV_HEADS) * HEAD_DIM
    dup = N_KV_HEADS * LANES
    return pl.pallas_call(
        _kv_context_kernel,
        out_shape=(jax.ShapeDtypeStruct((rows, dup), BF16),
                   jax.ShapeDtypeStruct((batch, dup, tm), BF16)),
        grid=(rows // tm,),
        in_specs=[
            pl.BlockSpec((tm, D_MODEL), lambda i: (i, 0)),
            _mod_spec(1, 3, cond_of_tile),
            _mod_spec(1, 4, cond_of_tile),
            pl.BlockSpec((D_MODEL, n_in), lambda i: (0, 0), pipeline_mode=pl.Buffered(1)),
            pl.BlockSpec((N_KV_HEADS * HEAD_DIM, D_MODEL), lambda i: (0, 0)),
        ],
        out_specs=(pl.BlockSpec((tm, dup), lambda i: (i, 0)),
                   pl.BlockSpec((None, dup, tm), lambda i: (i, 0, 0))),
        compiler_params=_cparams(("parallel",), 40),
        name="kv_context",
    )(x, mod, mod, w_in, wv_t)


def _band_start(i, seq_len):
    return jnp.clip((i - 1) * BLOCK, 0, seq_len - 3 * BLOCK)


def _attn_kernel(sink_ref, q_ref, kl_ref, vlt_ref, kc_ref, vct_ref, bias_ref, o_ref, *, seq_len):
    i = pl.program_id(1)
    band = 3 * BLOCK
    n_ctx = kc_ref.shape[0]
    pairs = KV_REP // 2
    start = pl.multiple_of(_band_start(i, seq_len), BLOCK)
    low_half = lax.broadcasted_iota(jnp.int32, (BLOCK, LANES), 1) < HEAD_DIM
    top_half = lax.broadcasted_iota(jnp.int32, (LANES, BLOCK), 0) < HEAD_DIM

    def scores_t(g):
        kv = slice(g * LANES, (g + 1) * LANES)
        q_parts = []
        for p in range(pairs):
            blk = q_ref[:, (g * pairs + p) * LANES:(g * pairs + p + 1) * LANES]
            q_parts.append(jnp.where(low_half, blk, jnp.zeros_like(blk)))
            q_parts.append(jnp.where(low_half, jnp.zeros_like(blk), blk))
        q_g = jnp.concatenate(q_parts, axis=0)
        k_all = jnp.concatenate([kc_ref[:, kv], kl_ref[pl.ds(start, band), kv]], axis=0)
        return lax.dot_general(k_all, q_g, (((1,), (1,)), ((), ())), preferred_element_type=F32)

    pending = [scores_t(g) for g in range(ATTN_AHEAD)]
    for g in range(N_KV_HEADS):
        if g + ATTN_AHEAD < N_KV_HEADS:
            pending.append(scores_t(g + ATTN_AHEAD))
        kv = slice(g * LANES, (g + 1) * LANES)
        s = pending[g]
        sink = jnp.concatenate([jnp.full((1, BLOCK), sink_ref[g * KV_REP + h] * LOG2_E, F32)
                                for h in range(KV_REP)], axis=1)
        s_ctx = s[:n_ctx]
        s_loc = s[n_ctx:] + bias_ref[...]
        m = jnp.maximum(jnp.maximum(jnp.max(s_ctx, axis=0, keepdims=True),
                                    jnp.max(s_loc, axis=0, keepdims=True)), sink)
        p_ctx = jnp.exp2(s_ctx - m)
        p_loc = jnp.exp2(s_loc - m)
        denom = (jnp.sum(p_ctx, axis=0, keepdims=True) + jnp.sum(p_loc, axis=0, keepdims=True)
                 + jnp.exp2(sink - m))
        probs_t = jnp.concatenate([p_ctx, p_loc], axis=0).astype(BF16)
        vt_all = jnp.concatenate([vct_ref[kv, :], vlt_ref[kv, pl.ds(start, band)]], axis=1)
        o_t = _dot(vt_all, probs_t) / denom
        for p in range(pairs):
            lo = o_t[:, 2 * p * BLOCK:(2 * p + 1) * BLOCK]
            hi = o_t[:, (2 * p + 1) * BLOCK:(2 * p + 2) * BLOCK]
            pair_t = jnp.where(top_half, lo, hi)
            o_ref[:, (g * pairs + p) * LANES:(g * pairs + p + 1) * LANES] = pair_t.T.astype(BF16)


def _window_bias(seq_len):
    band = 3 * BLOCK
    cols = KV_REP * BLOCK
    n_blk = seq_len // BLOCK
    blocks = jnp.array([0, 1, n_blk - 1], jnp.int32)
    delta = (_band_start(blocks, seq_len) - blocks * BLOCK)[:, None, None]
    j = jnp.arange(band, dtype=jnp.int32)[None, :, None]
    a = jnp.bitwise_and(jnp.arange(cols, dtype=jnp.int32), BLOCK - 1)[None, None, :]
    return jnp.where(jnp.abs(j + delta - a) <= WINDOW, 0.0, NEG_INF).astype(F32)


def _attention(sink, q, k_l, vt_l, k_c, vt_c, batch, seq_len, n_ctx):
    n_blk = seq_len // BLOCK
    assert n_blk >= 3
    dup = N_KV_HEADS * LANES
    n_q = N_Q_HEADS * HEAD_DIM
    band = 3 * BLOCK

    def bias_case(b, i):
        return (jnp.where(i == 0, 0, jnp.where(i == n_blk - 1, 2, 1)), 0, 0)

    return pl.pallas_call(
        functools.partial(_attn_kernel, seq_len=seq_len),
        out_shape=jax.ShapeDtypeStruct((batch * seq_len, n_q), BF16),
        grid=(batch, n_blk),
        in_specs=[
            pl.BlockSpec(memory_space=pltpu.SMEM),
            pl.BlockSpec((BLOCK, n_q), lambda b, i: (b * n_blk + i, 0)),
            pl.BlockSpec((None, seq_len, dup), lambda b, i: (b, 0, 0)),
            pl.BlockSpec((None, dup, seq_len), lambda b, i: (b, 0, 0)),
            pl.BlockSpec((None, n_ctx, dup), lambda b, i: (b, 0, 0)),
            pl.BlockSpec((None, dup, n_ctx), lambda b, i: (b, 0, 0)),
            pl.BlockSpec((None, band, KV_REP * BLOCK), bias_case),
        ],
        out_specs=pl.BlockSpec((BLOCK, n_q), lambda b, i: (b * n_blk + i, 0)),
        compiler_params=_cparams(("parallel", "arbitrary"), 56),
        name="window_attention",
    )(sink, q, k_l.reshape(batch, seq_len, dup), vt_l, k_c.reshape(batch, n_ctx, dup), vt_c,
      _window_bias(seq_len))


def _attn_out_kernel(x_ref, gt_ref, a_ref, wo_ref, lg_ref, lb_ref, o_ref):
    for r in range(x_ref.shape[0] // PROJ_SLAB):
        rows = pl.ds(r * PROJ_SLAB, PROJ_SLAB)
        y = _dot(a_ref[rows, :], wo_ref[...])
        z = ALPHA * x_ref[rows, :] + gt_ref[...] * y
        o_ref[rows, :] = _layer_norm_rows(z, lg_ref[...], lb_ref[...])


def _attn_out(x, mod, cond_of_tile, a, w_out, ln_g, ln_b, tm):
    rows = x.shape[0]
    return pl.pallas_call(
        _attn_out_kernel,
        out_shape=jax.ShapeDtypeStruct((rows, D_MODEL), F32),
        grid=(rows // tm,),
        in_specs=[
            pl.BlockSpec((tm, D_MODEL), lambda i: (i, 0)),
            _mod_spec(1, 5, cond_of_tile),
            pl.BlockSpec((tm, D_MODEL), lambda i: (i, 0)),
            pl.BlockSpec((D_MODEL, D_MODEL), lambda i: (0, 0), pipeline_mode=pl.Buffered(1)),
            _ln_spec(1, 1),
            _ln_spec(1, 1),
        ],
        out_specs=pl.BlockSpec((tm, D_MODEL), lambda i: (i, 0)),
        compiler_params=_cparams(("parallel",), 48),
        name="attn_out",
    )(x, mod, a, w_out, ln_g, ln_b)


def _dft_cos_sin(n):
    k = jnp.arange(n, dtype=jnp.int32)
    ang = ((k[:, None] * k[None, :]) % n).astype(F32) * (2.0 * math.pi / n)
    return jnp.cos(ang), jnp.sin(ang)


def _twiddles(n1, n2):
    k1 = jnp.arange(n1, dtype=jnp.int32)[:, None]
    m2 = jnp.arange(n2, dtype=jnp.int32)[None, :]
    n = n1 * n2
    ang = ((k1 * m2) % n).astype(F32) * (2.0 * math.pi / n)
    shape = (n1, n2, LANES)
    return (jnp.broadcast_to(jnp.cos(ang)[:, :, None], shape),
            jnp.broadcast_to(jnp.sin(ang)[:, :, None], shape))


def _rope_lane_tables(seq_len):
    rows = seq_len // GRID_W
    row = jnp.repeat(jnp.arange(rows, dtype=F32), GRID_W)
    col = jnp.tile(jnp.arange(GRID_W, dtype=F32), rows)
    n_freq = HEAD_DIM // 4
    inv_freq = jnp.power(ROPE_BASE, -jnp.arange(n_freq, dtype=F32) / n_freq)
    ang_r = row[:, None] * inv_freq
    ang_c = col[:, None] * inv_freq
    cos_h = jnp.concatenate([jnp.cos(ang_r), jnp.cos(ang_r), jnp.cos(ang_c), jnp.cos(ang_c)], axis=-1)
    sin_h = jnp.concatenate([-jnp.sin(ang_r), jnp.sin(ang_r), -jnp.sin(ang_c), jnp.sin(ang_c)], axis=-1)
    reps = LANES // HEAD_DIM
    return jnp.tile(cos_h, (1, reps)), jnp.tile(sin_h, (1, reps))


def kernel(x, c, ctx, c_ctx, w_mod, b_mod, ln_g, ln_b, ffn_w_gate, ffn_w_up, ffn_w_down,
           ab_w_in, ab_conv, ab_w_out, attn_w_in, attn_sink, attn_w_out):
    batch, seq_len, _ = x.shape
    n_ctx = ctx.shape[1]
    assert batch == 2 and seq_len % FFN_TM == 0 and seq_len % (FFT_N1 * 8) == 0

    xl = x.reshape(batch * seq_len, D_MODEL)
    xc = ctx.reshape(batch * n_ctx, D_MODEL)
    ctx_rows = batch * n_ctx

    cond = jnp.concatenate([c, c_ctx[None, :], jnp.zeros((N_COND - batch - 1, D_MODEL), F32)], axis=0)
    mod = _modulation(cond, w_mod, b_mod)
    ln_g4 = ln_g.reshape(DEPTH, 3, 1, D_MODEL)
    ln_b4 = ln_b.reshape(DEPTH, 3, 1, D_MODEL)

    ffn_f32 = (ffn_w_gate, ffn_w_up, ffn_w_down)
    first_weights = tuple(w[0, 0].astype(BF16) for w in ffn_f32)
    mixer_f32 = (ab_w_in, ab_w_out, attn_w_in, attn_w_out)

    def lat_cond(tm):
        tiles = seq_len // tm
        return lambda i: i // tiles

    ctx_cond = lambda i: CTX_COND

    def ffn_both(xl, xc, layer, sub, weights, with_ctx, extra_jobs=()):
        nxt = (layer, sub + 1) if sub == 0 else (layer + 1, 0)
        jobs = tuple((w, nxt) for w in ffn_f32) if nxt[0] < DEPTH else ()
        xl, copies = _ffn(xl, mod, layer, sub, lat_cond(FFN_TM), weights, ln_g4, ln_b4, FFN_TM,
                          jobs + tuple(extra_jobs))
        if with_ctx:
            xc, _ = _ffn(xc, mod, layer, sub, ctx_cond, weights, ln_g4, ln_b4, ctx_rows)
        return xl, xc, copies[:len(jobs)], copies[len(jobs):]

    xl, xc, weights, (ab_in, ab_out, at_in, at_out) = ffn_both(
        xl, xc, 0, 0, first_weights, True, [(w, (0,)) for w in mixer_f32])

    cc, sc_ = _dft_cos_sin(FOURIER_GROUP_DIM)
    cs_chan = jnp.concatenate([cc, -sc_], axis=1).astype(BF16)
    n1, n2 = FFT_N1, seq_len // FFT_N1
    c1, s1 = _dft_cos_sin(n1)
    m1 = jnp.concatenate([jnp.concatenate([c1, s1], axis=1),
                          jnp.concatenate([-s1, c1], axis=1)], axis=0).astype(BF16)
    c2, s2 = _dft_cos_sin(n2)
    f2 = jnp.concatenate([c2, s2], axis=1).astype(BF16)
    twc, tws = _twiddles(n1, n2)
    cn, sn = _dft_cos_sin(n_ctx)
    f2_ctx = jnp.concatenate([cn, sn], axis=1).astype(BF16)
    twc_ctx, tws_ctx = _twiddles(1, n_ctx)

    gb, v, vf = _mix_in(xl, mod, lat_cond(PROJ_TM), ab_in, cs_chan, PROJ_TM)
    t = _fft1(vf, m1, batch, n1, n2)
    yb = _fft2(t, twc, tws, f2, batch, n1, n2)
    xl = _mix_out(xl, mod, lat_cond(PROJ_TM), gb, v, yb, ab_conv, ab_out, ln_g4, ln_b4, PROJ_TM, seq_len)

    gb_c, v_c, vf_c = _mix_in(xc, mod, ctx_cond, ab_in, cs_chan, n_ctx)
    yb_c = _fft2(vf_c, twc_ctx, tws_ctx, f2_ctx, batch, 1, n_ctx)
    xc = _mix_out(xc, mod, ctx_cond, gb_c, v_c, yb_c, ab_conv, ab_out, ln_g4, ln_b4, n_ctx, n_ctx)

    xl, xc, weights, _ = ffn_both(xl, xc, 0, 1, weights, True)

    xl, xc, weights, _ = ffn_both(xl, xc, 1, 0, weights, True)

    cos_t, sin_t = _rope_lane_tables(seq_len)
    n_qk = (N_Q_HEADS + N_KV_HEADS) * HEAD_DIM
    wv_t = jnp.transpose(attn_w_in[0, :, n_qk:]).astype(BF16)
    q, k_l, vt_l = _qkv_latent(xl, mod, lat_cond(PROJ_TM), at_in, wv_t, cos_t, sin_t, PROJ_TM, batch, seq_len)
    k_c, vt_c = _kv_context(xc, mod, ctx_cond, at_in, wv_t, n_ctx, batch)
    att = _attention(attn_sink.reshape(N_Q_HEADS), q, k_l, vt_l, k_c, vt_c, batch, seq_len, n_ctx)
    xl = _attn_out(xl, mod, lat_cond(PROJ_TM), att, at_out, ln_g4, ln_b4, PROJ_TM)

    xl, _, _, _ = ffn_both(xl, xc, 1, 1, weights, False)
    return xl.reshape(batch, seq_len, D_MODEL)
```

```python
import functools
import math

import jax
import jax.numpy as jnp
from jax import lax
from jax.experimental import pallas as pl
from jax.experimental.pallas import tpu as pltpu

F32 = jnp.float32
BF16 = jnp.bfloat16

D_MODEL = 2048
GRID_W = 64
N_MOD = 9
D_FF = 5632
CONV_DIM = 1024
FOURIER_DIM = 1024
FOURIER_GROUPS = 8
FOURIER_GROUP_DIM = 128
HEAD_DIM = 64
N_Q_HEADS = 32
N_KV_HEADS = 4
KV_REP = 8
WINDOW = 128
BLOCK = 128
ROPE_BASE = 10000.0
LN_EPS = 1e-5
NEG_INF = -1e30
DEPTH = 2
ALPHA = (2 * DEPTH) ** 0.25
LOG2_E = math.log2(math.e)

LANES = 128
SUBLANES = 8
BF16_SUBLANES = 16
V7X_VMEM_BYTES = 64 * 1024 * 1024
MIB = 1024 * 1024

N_COND = 8
CTX_COND = 2
FFN_TM = 1024
FFN_TF = 512
FFN_SLAB = 256
PROJ_TM = 512
PROJ_SLAB = 256
ATTN_AHEAD = 2
FFT_N1 = 64
FFT1_GROUPS = 2


def _cparams(semantics, vmem_mib):
    return pltpu.CompilerParams(dimension_semantics=semantics,
                                vmem_limit_bytes=min(vmem_mib * MIB, V7X_VMEM_BYTES))


def _dot(a, b):
    return jnp.dot(a, b, preferred_element_type=F32)


def _layer_norm_rows(z, g, b):
    mu = jnp.mean(z, axis=-1, keepdims=True)
    zc = z - mu
    var = jnp.mean(zc * zc, axis=-1, keepdims=True)
    return zc * lax.rsqrt(var + LN_EPS) * g + b


def _mod_kernel(c_ref, w_ref, b_ref, o_ref):
    c = c_ref[...]
    a = (c * jax.nn.sigmoid(c)).astype(BF16)
    o_ref[:, 0, :] = _dot(a, w_ref[...].astype(BF16)) + b_ref[...]


def _modulation(cond, w_mod, b_mod):
    n = N_MOD * D_MODEL
    return pl.pallas_call(
        _mod_kernel,
        out_shape=jax.ShapeDtypeStruct((DEPTH, N_COND, N_MOD, 1, D_MODEL), F32),
        grid=(DEPTH, N_MOD),
        in_specs=[
            pl.BlockSpec((N_COND, D_MODEL), lambda l, j: (0, 0)),
            pl.BlockSpec((None, D_MODEL, D_MODEL), lambda l, j: (l, 0, j)),
            pl.BlockSpec((None, 1, D_MODEL), lambda l, j: (l, 0, j)),
        ],
        out_specs=pl.BlockSpec((None, N_COND, None, 1, D_MODEL), lambda l, j: (l, 0, j, 0, 0)),
        compiler_params=_cparams(("parallel", "parallel"), 48),
        name="modulation",
    )(cond, w_mod, b_mod.reshape(DEPTH, 1, n))


def _mod_spec(layer, k, cond_of_tile):
    return pl.BlockSpec((None, None, None, 1, D_MODEL),
                        lambda i, *_: (layer, cond_of_tile(i), k, 0, 0))


def _ln_spec(layer, k):
    return pl.BlockSpec((None, None, 1, D_MODEL), lambda i, *_: (layer, k, 0, 0))


def _ffn_kernel(x_ref, sh_ref, sc_ref, gt_ref, wg_ref, wu_ref, wd_ref, lg_ref, lb_ref, *rest):
    n_cast = (len(rest) - 2) // 2
    o_ref, xm_ref = rest[n_cast], rest[-1]
    cast_pairs = tuple(zip(rest[:n_cast], rest[n_cast + 1:-1]))
    f = pl.program_id(1)
    last = pl.num_programs(1) - 1
    slabs = [pl.ds(r * FFN_SLAB, FFN_SLAB) for r in range(x_ref.shape[0] // FFN_SLAB)]
    half_gate = 0.5 * gt_ref[...]

    def cast_next_slice():
        for src, dst in cast_pairs:
            dst[...] = src[...].astype(BF16)

    def swiglu_chunk(rows):
        xm = xm_ref[rows, :]
        h_gate = _dot(xm, wg_ref[...])
        h_up = _dot(xm, wu_ref[...])
        act = (h_gate * jax.nn.sigmoid(h_gate) * h_up).astype(BF16)
        return half_gate * _dot(act, wd_ref[...])

    @pl.when(f == 0)
    def _():
        cast_next_slice()
        scale1 = 1.0 + sc_ref[...]
        shift = sh_ref[...]
        for rows in slabs:
            x = x_ref[rows, :]
            xm_ref[rows, :] = (x * scale1 + shift).astype(BF16)
            o_ref[rows, :] = ALPHA * x + swiglu_chunk(rows)

    @pl.when(jnp.logical_and(f > 0, f < last))
    def _():
        cast_next_slice()
        o_ref[...] += swiglu_chunk(slice(None))

    @pl.when(f == last)
    def _():
        cast_next_slice()
        g = lg_ref[...]
        b = lb_ref[...]
        for rows in slabs:
            o_ref[rows, :] = _layer_norm_rows(o_ref[rows, :] + swiglu_chunk(rows), g, b)


def _cast_specs(src, lead, n_f, n_steps):
    n_rows, n_cols = src.shape[-2:]
    rows = BF16_SUBLANES
    while n_rows % rows or n_rows // rows > n_steps:
        rows += BF16_SUBLANES
    last_block = n_rows // rows - 1
    step = lambda i, f: jnp.minimum(i * n_f + f, last_block)
    in_spec = pl.BlockSpec((None,) * len(lead) + (rows, n_cols), lambda i, f: (*lead, step(i, f), 0))
    out_spec = pl.BlockSpec((rows, n_cols), lambda i, f: (step(i, f), 0))
    return in_spec, out_spec, jax.ShapeDtypeStruct((n_rows, n_cols), BF16)


def _ffn(x, mod, layer, sub, cond_of_tile, weights, ln_g, ln_b, tm, cast_jobs=()):
    rows = x.shape[0]
    k0 = 6 * sub
    n_f = D_FF // FFN_TF
    n_steps = (rows // tm) * n_f
    w_in_spec = pl.BlockSpec((D_MODEL, FFN_TF), lambda i, f: (0, f))
    in_specs = [
        pl.BlockSpec((tm, D_MODEL), lambda i, f: (i, 0)),
        _mod_spec(layer, k0, cond_of_tile),
        _mod_spec(layer, k0 + 1, cond_of_tile),
        _mod_spec(layer, k0 + 2, cond_of_tile),
        w_in_spec,
        w_in_spec,
        pl.BlockSpec((FFN_TF, D_MODEL), lambda i, f: (f, 0)),
        _ln_spec(layer, 2 * sub),
        _ln_spec(layer, 2 * sub),
    ]
    out_shape = jax.ShapeDtypeStruct((rows, D_MODEL), F32)
    out_specs = pl.BlockSpec((tm, D_MODEL), lambda i, f: (i, 0))
    args = (x, mod, mod, mod, *weights, ln_g, ln_b)
    semantics = ("parallel", "arbitrary")
    if cast_jobs:
        specs = [_cast_specs(src, lead, n_f, n_steps) for src, lead in cast_jobs]
        in_specs += [s[0] for s in specs]
        out_specs = (out_specs, *[s[1] for s in specs])
        out_shape = (out_shape, *[s[2] for s in specs])
        args += tuple(src for src, _ in cast_jobs)
        semantics = ("arbitrary", "arbitrary")
    out = pl.pallas_call(
        _ffn_kernel,
        out_shape=out_shape,
        grid=(rows // tm, n_f),
        in_specs=in_specs,
        out_specs=out_specs,
        scratch_shapes=[pltpu.VMEM((tm, D_MODEL), BF16)],
        compiler_params=_cparams(semantics, 60),
        name=f"ffn_l{layer}_s{sub}_{rows}",
    )(*args)
    if not cast_jobs:
        return out, ()
    return out[0], tuple(out[1:])


def _mix_in_kernel(x_ref, sh_ref, sc_ref, w_ref, cs_ref, gb_ref, v_ref, f_ref):
    xm = (x_ref[...] * (1.0 + sc_ref[...]) + sh_ref[...]).astype(BF16)
    gb_ref[...] = _dot(xm, w_ref[:, 0:CONV_DIM])
    g_c = _dot(xm, w_ref[:, CONV_DIM:2 * CONV_DIM])
    x_in = _dot(xm, w_ref[:, 2 * CONV_DIM:3 * CONV_DIM])
    v_ref[...] = g_c * x_in
    u_f = _dot(xm, w_ref[:, 3 * CONV_DIM:]).astype(BF16)
    cs = cs_ref[...]
    for g in range(FOURIER_GROUPS):
        lanes = slice(g * FOURIER_GROUP_DIM, (g + 1) * FOURIER_GROUP_DIM)
        res = _dot(u_f[:, lanes], cs)
        f_ref[0, :, lanes] = res[:, :FOURIER_GROUP_DIM]
        f_ref[1, :, lanes] = res[:, FOURIER_GROUP_DIM:]


def _mix_in(x, mod, cond_of_tile, w_in, cs, tm):
    rows = x.shape[0]
    n_in = 3 * CONV_DIM + FOURIER_DIM
    return pl.pallas_call(
        _mix_in_kernel,
        out_shape=(jax.ShapeDtypeStruct((rows, CONV_DIM), F32),
                   jax.ShapeDtypeStruct((rows, CONV_DIM), F32),
                   jax.ShapeDtypeStruct((2, rows, FOURIER_DIM), F32)),
        grid=(rows // tm,),
        in_specs=[
            pl.BlockSpec((tm, D_MODEL), lambda i: (i, 0)),
            _mod_spec(0, 3, cond_of_tile),
            _mod_spec(0, 4, cond_of_tile),
            pl.BlockSpec((D_MODEL, n_in), lambda i: (0, 0), pipeline_mode=pl.Buffered(1)),
            pl.BlockSpec((FOURIER_GROUP_DIM, 2 * FOURIER_GROUP_DIM), lambda i: (0, 0)),
        ],
        out_specs=(pl.BlockSpec((tm, CONV_DIM), lambda i: (i, 0)),
                   pl.BlockSpec((tm, CONV_DIM), lambda i: (i, 0)),
                   pl.BlockSpec((2, tm, FOURIER_DIM), lambda i: (0, i, 0))),
        compiler_params=_cparams(("parallel",), 56),
        name=f"mix_in_{rows}",
    )(x, mod, mod, w_in, cs)


def _fft1_kernel(m_ref, v_ref, o_ref):
    n1 = v_ref.shape[1]
    m = m_ref[...]
    for q in range(v_ref.shape[2]):
        for j in range(SUBLANES):
            v = jnp.concatenate([v_ref[0, :, q, j, :], v_ref[1, :, q, j, :]], axis=0).astype(BF16)
            t = _dot(m, v)
            o_ref[0, :, q, j, :] = t[:n1]
            o_ref[1, :, q, j, :] = t[n1:]


def _fft1(v, m1, batch, n1, n2):
    groups = n2 // SUBLANES
    shape = (2, batch, n1, groups, SUBLANES, FOURIER_DIM)
    spec = pl.BlockSpec((2, None, n1, FFT1_GROUPS, SUBLANES, FOURIER_DIM), lambda b, j: (0, b, 0, j, 0, 0))
    return pl.pallas_call(
        _fft1_kernel,
        out_shape=jax.ShapeDtypeStruct(shape, F32),
        grid=(batch, groups // FFT1_GROUPS),
        in_specs=[pl.BlockSpec((2 * n1, 2 * n1), lambda b, j: (0, 0)), spec],
        out_specs=spec,
        compiler_params=_cparams(("parallel", "parallel"), 48),
        name="fft_stage1",
    )(m1, v.reshape(shape))


def _fft2_one(tr, ti, twc, tws, f, scale):
    reps = FOURIER_DIM // LANES
    c = jnp.concatenate([twc] * reps, axis=1)
    s = jnp.concatenate([tws] * reps, axis=1)
    pr = (tr * c + ti * s).astype(BF16)
    pi = (ti * c - tr * s).astype(BF16)
    return _dot(f, jnp.concatenate([pr, pi], axis=0)) * scale


def _fft2_kernel(t_ref, twc_ref, tws_ref, f_ref, o_ref, *, scale):
    f = f_ref[...]
    for j in range(SUBLANES):
        o_ref[:, j, :] = _fft2_one(t_ref[0, j], t_ref[1, j], twc_ref[j], tws_ref[j], f, scale)


def _fft2_single_kernel(t_ref, twc_ref, tws_ref, f_ref, o_ref, *, scale):
    o_ref[...] = _fft2_one(t_ref[0], t_ref[1], twc_ref[...], tws_ref[...], f_ref[...], scale)


def _fft2(t, twc, tws, f2, batch, n1, n2):
    t5 = t.reshape(2, batch, n1, n2, FOURIER_DIM)
    scale = 1.0 / math.sqrt(n1 * n2 * FOURIER_GROUP_DIM)
    f_spec = pl.BlockSpec((n2, 2 * n2), lambda b, k: (0, 0))
    if n1 == 1:
        out = pl.pallas_call(
            functools.partial(_fft2_single_kernel, scale=scale),
            out_shape=jax.ShapeDtypeStruct((batch, n2, FOURIER_DIM), F32),
            grid=(batch, 1),
            in_specs=[
                pl.BlockSpec((2, None, None, n2, FOURIER_DIM), lambda b, k: (0, b, 0, 0, 0)),
                pl.BlockSpec((None, n2, LANES), lambda b, k: (0, 0, 0)),
                pl.BlockSpec((None, n2, LANES), lambda b, k: (0, 0, 0)),
                f_spec,
            ],
            out_specs=pl.BlockSpec((None, n2, FOURIER_DIM), lambda b, k: (b, 0, 0)),
            compiler_params=_cparams(("parallel", "parallel"), 32),
            name=f"fft_stage2_{n1}x{n2}",
        )(t5, twc, tws, f2)
    else:
        out = pl.pallas_call(
            functools.partial(_fft2_kernel, scale=scale),
            out_shape=jax.ShapeDtypeStruct((batch, n2, n1 // SUBLANES, SUBLANES, FOURIER_DIM), F32),
            grid=(batch, n1 // SUBLANES),
            in_specs=[
                pl.BlockSpec((2, None, SUBLANES, n2, FOURIER_DIM), lambda b, k: (0, b, k, 0, 0)),
                pl.BlockSpec((SUBLANES, n2, LANES), lambda b, k: (k, 0, 0)),
                pl.BlockSpec((SUBLANES, n2, LANES), lambda b, k: (k, 0, 0)),
                f_spec,
            ],
            out_specs=pl.BlockSpec((None, n2, None, SUBLANES, FOURIER_DIM), lambda b, k: (b, 0, k, 0, 0)),
            compiler_params=_cparams(("parallel", "parallel"), 48),
            name=f"fft_stage2_{n1}x{n2}",
        )(t5, twc, tws, f2)
    return out.reshape(batch * n2 * n1, FOURIER_DIM)


def _mix_out_kernel(x_ref, gt_ref, gb_ref, v_ref, vp_ref, vn_ref, yb_ref, wc_ref, wo_ref,
                    lg_ref, lb_ref, o_ref, *, tiles_per_seq):
    i = pl.program_id(0)
    tm = v_ref.shape[0]
    pos = lax.rem(i, tiles_per_seq)
    v = v_ref[...]
    prev_row = jnp.where(pos == 0, 0.0, vp_ref[7:8, :])
    next_row = jnp.where(pos == tiles_per_seq - 1, 0.0, vn_ref[0:1, :])
    row = lax.broadcasted_iota(jnp.int32, v.shape, 0)
    v_prev = jnp.where(row == 0, prev_row, pltpu.roll(v, 1, 0))
    v_next = jnp.where(row == tm - 1, next_row, pltpu.roll(v, tm - 1, 0))
    conv = wc_ref[0:1, :] * v_prev + wc_ref[1:2, :] * v + wc_ref[2:3, :] * v_next
    y_a = (gb_ref[...] * conv).astype(BF16)
    for r in range(tm // PROJ_SLAB):
        rows = pl.ds(r * PROJ_SLAB, PROJ_SLAB)
        y = (_dot(y_a[r * PROJ_SLAB:(r + 1) * PROJ_SLAB], wo_ref[0:CONV_DIM, :])
             + _dot(yb_ref[rows, :].astype(BF16), wo_ref[CONV_DIM:, :]))
        z = ALPHA * x_ref[rows, :] + gt_ref[...] * y
        o_ref[rows, :] = _layer_norm_rows(z, lg_ref[...], lb_ref[...])


def _mix_out(x, mod, cond_of_tile, gb, v, yb, w_conv, w_out, ln_g, ln_b, tm, seq_len):
    rows = x.shape[0]
    sub = 8
    v3 = v.reshape(rows // sub, sub, CONV_DIM)
    step = tm // sub
    last = rows // sub - 1
    return pl.pallas_call(
        functools.partial(_mix_out_kernel, tiles_per_seq=seq_len // tm),
        out_shape=jax.ShapeDtypeStruct((rows, D_MODEL), F32),
        grid=(rows // tm,),
        in_specs=[
            pl.BlockSpec((tm, D_MODEL), lambda i: (i, 0)),
            _mod_spec(0, 5, cond_of_tile),
            pl.BlockSpec((tm, CONV_DIM), lambda i: (i, 0)),
            pl.BlockSpec((tm, CONV_DIM), lambda i: (i, 0)),
            pl.BlockSpec((None, sub, CONV_DIM), lambda i: (jnp.maximum(i * step - 1, 0), 0, 0)),
            pl.BlockSpec((None, sub, CONV_DIM), lambda i: (jnp.minimum((i + 1) * step, last), 0, 0)),
            pl.BlockSpec((tm, FOURIER_DIM), lambda i: (i, 0)),
            pl.BlockSpec((None, 3, CONV_DIM), lambda i: (0, 0, 0)),
            pl.BlockSpec((D_MODEL, D_MODEL), lambda i: (0, 0), pipeline_mode=pl.Buffered(1)),
            _ln_spec(0, 1),
            _ln_spec(0, 1),
        ],
        out_specs=pl.BlockSpec((tm, D_MODEL), lambda i: (i, 0)),
        compiler_params=_cparams(("parallel",), 56),
        name=f"mix_out_{rows}",
    )(x, mod, gb, v, v3, v3, yb, w_conv, w_out, ln_g, ln_b)


def _rope_block(blk, cos, sin, first_half):
    rot = jnp.where(first_half, pltpu.roll(blk, LANES - 16, 1), pltpu.roll(blk, 16, 1))
    return blk * cos + rot * sin


def _dup_heads(blk, low_half):
    swapped = pltpu.roll(blk, HEAD_DIM, 1)
    return jnp.where(low_half, blk, swapped), jnp.where(low_half, swapped, blk)


def _qkv_kernel(x_ref, sh_ref, sc_ref, w_ref, wvt_ref, cos_ref, sin_ref, q_ref, k_ref, vt_ref, *, rope):
    tm = x_ref.shape[0]
    xm = (x_ref[...] * (1.0 + sc_ref[...]) + sh_ref[...]).astype(BF16)
    lane = lax.broadcasted_iota(jnp.int32, (tm, LANES), 1)
    first_half = jnp.bitwise_and(lane, 31) < 16
    low_half = lane < HEAD_DIM
    n_q = N_Q_HEADS * HEAD_DIM
    n_kv = N_KV_HEADS * HEAD_DIM
    if rope:
        cos = cos_ref[...]
        sin = sin_ref[...]
    if q_ref is not None:
        q = _dot(xm, w_ref[:, 0:n_q])
        for j in range(n_q // LANES):
            lanes = slice(j * LANES, (j + 1) * LANES)
            blk = q[:, lanes]
            if rope:
                blk = _rope_block(blk, cos, sin, first_half)
            q_ref[:, lanes] = (blk * (LOG2_E * HEAD_DIM ** -0.5)).astype(BF16)
    k = _dot(xm, w_ref[:, n_q:n_q + n_kv])
    v_t = lax.dot_general(wvt_ref[...], xm, (((1,), (1,)), ((), ())),
                          preferred_element_type=F32).astype(BF16)
    for j in range(n_kv // LANES):
        kb = k[:, j * LANES:(j + 1) * LANES]
        if rope:
            kb = _rope_block(kb, cos, sin, first_half)
        k0, k1 = _dup_heads(kb, low_half)
        k_ref[:, 2 * j * LANES:(2 * j + 1) * LANES] = k0.astype(BF16)
        k_ref[:, (2 * j + 1) * LANES:(2 * j + 2) * LANES] = k1.astype(BF16)
    for g in range(N_KV_HEADS):
        head = v_t[g * HEAD_DIM:(g + 1) * HEAD_DIM, :]
        vt_ref[g * LANES:g * LANES + HEAD_DIM, :] = head
        vt_ref[g * LANES + HEAD_DIM:(g + 1) * LANES, :] = head


def _qkv_latent_kernel(x_ref, sh_ref, sc_ref, w_ref, wvt_ref, cos_ref, sin_ref, q_ref, k_ref, vt_ref):
    _qkv_kernel(x_ref, sh_ref, sc_ref, w_ref, wvt_ref, cos_ref, sin_ref, q_ref, k_ref, vt_ref, rope=True)


def _kv_context_kernel(x_ref, sh_ref, sc_ref, w_ref, wvt_ref, k_ref, vt_ref):
    _qkv_kernel(x_ref, sh_ref, sc_ref, w_ref, wvt_ref, None, None, None, k_ref, vt_ref, rope=False)


def _qkv_latent(x, mod, cond_of_tile, w_in, wv_t, cos_t, sin_t, tm, batch, seq_len):
    rows = x.shape[0]
    n_in = (N_Q_HEADS + 2 * N_KV_HEADS) * HEAD_DIM
    tps = seq_len // tm
    dup = N_KV_HEADS * LANES
    return pl.pallas_call(
        _qkv_latent_kernel,
        out_shape=(jax.ShapeDtypeStruct((rows, N_Q_HEADS * HEAD_DIM), BF16),
                   jax.ShapeDtypeStruct((rows, dup), BF16),
                   jax.ShapeDtypeStruct((batch, dup, seq_len), BF16)),
        grid=(rows // tm,),
        in_specs=[
            pl.BlockSpec((tm, D_MODEL), lambda i: (i, 0)),
            _mod_spec(1, 3, cond_of_tile),
            _mod_spec(1, 4, cond_of_tile),
            pl.BlockSpec((D_MODEL, n_in), lambda i: (0, 0), pipeline_mode=pl.Buffered(1)),
            pl.BlockSpec((N_KV_HEADS * HEAD_DIM, D_MODEL), lambda i: (0, 0)),
            pl.BlockSpec((tm, LANES), lambda i: (lax.rem(i, tps), 0)),
            pl.BlockSpec((tm, LANES), lambda i: (lax.rem(i, tps), 0)),
        ],
        out_specs=(pl.BlockSpec((tm, N_Q_HEADS * HEAD_DIM), lambda i: (i, 0)),
                   pl.BlockSpec((tm, dup), lambda i: (i, 0)),
                   pl.BlockSpec((None, dup, tm), lambda i: (i // tps, 0, lax.rem(i, tps)))),
        compiler_params=_cparams(("parallel",), 48),
        name="qkv_latent",
    )(x, mod, mod, w_in, wv_t, cos_t, sin_t)


def _kv_context(x, mod, cond_of_tile, w_in, wv_t, tm, batch):
    rows = x.shape[0]
    n_in = (N_Q_HEADS + 2 * N_KV_HEADS) * HEAD_DIM
    dup = N_KV_HEADS * LANES
    return pl.pallas_call(
        _kv_context_kernel,
        out_shape=(jax.ShapeDtypeStruct((rows, dup), BF16),
                   jax.ShapeDtypeStruct((batch, dup, tm), BF16)),
        grid=(rows // tm,),
        in_specs=[
            pl.BlockSpec((tm, D_MODEL), lambda i: (i, 0)),
            _mod_spec(1, 3, cond_of_tile),
            _mod_spec(1, 4, cond_of_tile),
            pl.BlockSpec((D_MODEL, n_in), lambda i: (0, 0), pipeline_mode=pl.Buffered(1)),
            pl.BlockSpec((N_KV_HEADS * HEAD_DIM, D_MODEL), lambda i: (0, 0)),
        ],
        out_specs=(pl.BlockSpec((tm, dup), lambda i: (i, 0)),
                   pl.BlockSpec((None, dup, tm), lambda i: (i, 0, 0))),
        compiler_params=_cparams(("parallel",), 40),
        name="kv_context",
    )(x, mod, mod, w_in, wv_t)


def _band_start(i, seq_len):
    return jnp.clip((i - 1) * BLOCK, 0, seq_len - 3 * BLOCK)


def _attn_kernel(sink_ref, q_ref, kl_ref, vlt_ref, kc_ref, vct_ref, bias_ref, o_ref, *, seq_len):
    i = pl.program_id(1)
    band = 3 * BLOCK
    n_ctx = kc_ref.shape[0]
    pairs = KV_REP // 2
    start = pl.multiple_of(_band_start(i, seq_len), BLOCK)
    low_half = lax.broadcasted_iota(jnp.int32, (BLOCK, LANES), 1) < HEAD_DIM
    top_half = lax.broadcasted_iota(jnp.int32, (LANES, BLOCK), 0) < HEAD_DIM

    def scores_t(g):
        kv = slice(g * LANES, (g + 1) * LANES)
        q_parts = []
        for p in range(pairs):
            blk = q_ref[:, (g * pairs + p) * LANES:(g * pairs + p + 1) * LANES]
            q_parts.append(jnp.where(low_half, blk, jnp.zeros_like(blk)))
            q_parts.append(jnp.where(low_half, jnp.zeros_like(blk), blk))
        q_g = jnp.concatenate(q_parts, axis=0)
        k_all = jnp.concatenate([kc_ref[:, kv], kl_ref[pl.ds(start, band), kv]], axis=0)
        return lax.dot_general(k_all, q_g, (((1,), (1,)), ((), ())), preferred_element_type=F32)

    pending = [scores_t(g) for g in range(ATTN_AHEAD)]
    for g in range(N_KV_HEADS):
        if g + ATTN_AHEAD < N_KV_HEADS:
            pending.append(scores_t(g + ATTN_AHEAD))
        kv = slice(g * LANES, (g + 1) * LANES)
        s = pending[g]
        sink = jnp.concatenate([jnp.full((1, BLOCK), sink_ref[g * KV_REP + h] * LOG2_E, F32)
                                for h in range(KV_REP)], axis=1)
        s_ctx = s[:n_ctx]
        s_loc = s[n_ctx:] + bias_ref[...]
        m = jnp.maximum(jnp.maximum(jnp.max(s_ctx, axis=0, keepdims=True),
                                    jnp.max(s_loc, axis=0, keepdims=True)), sink)
        p_ctx = jnp.exp2(s_ctx - m)
        p_loc = jnp.exp2(s_loc - m)
        probs_t = jnp.concatenate([p_ctx, p_loc], axis=0).astype(BF16)
        vt_all = jnp.concatenate([vct_ref[kv, :], vlt_ref[kv, pl.ds(start, band)]], axis=1)
        vt_aug = jnp.concatenate([vt_all, jnp.ones((BF16_SUBLANES, vt_all.shape[1]), BF16)], axis=0)
        o_aug = _dot(vt_aug, probs_t)
        denom = o_aug[LANES:LANES + 1] + jnp.exp2(sink - m)
        o_t = o_aug[:LANES] / denom
        for p in range(pairs):
            lo = o_t[:, 2 * p * BLOCK:(2 * p + 1) * BLOCK]
            hi = o_t[:, (2 * p + 1) * BLOCK:(2 * p + 2) * BLOCK]
            pair_t = jnp.where(top_half, lo, hi)
            o_ref[:, (g * pairs + p) * LANES:(g * pairs + p + 1) * LANES] = pair_t.T.astype(BF16)


def _window_bias(seq_len):
    band = 3 * BLOCK
    cols = KV_REP * BLOCK
    n_blk = seq_len // BLOCK
    blocks = jnp.array([0, 1, n_blk - 1], jnp.int32)
    delta = (_band_start(blocks, seq_len) - blocks * BLOCK)[:, None, None]
    j = jnp.arange(band, dtype=jnp.int32)[None, :, None]
    a = jnp.bitwise_and(jnp.arange(cols, dtype=jnp.int32), BLOCK - 1)[None, None, :]
    return jnp.where(jnp.abs(j + delta - a) <= WINDOW, 0.0, NEG_INF).astype(F32)


def _attention(sink, q, k_l, vt_l, k_c, vt_c, batch, seq_len, n_ctx):
    n_blk = seq_len // BLOCK
    assert n_blk >= 3
    dup = N_KV_HEADS * LANES
    n_q = N_Q_HEADS * HEAD_DIM
    band = 3 * BLOCK

    def bias_case(b, i):
        return (jnp.where(i == 0, 0, jnp.where(i == n_blk - 1, 2, 1)), 0, 0)

    return pl.pallas_call(
        functools.partial(_attn_kernel, seq_len=seq_len),
        out_shape=jax.ShapeDtypeStruct((batch * seq_len, n_q), BF16),
        grid=(batch, n_blk),
        in_specs=[
            pl.BlockSpec(memory_space=pltpu.SMEM),
            pl.BlockSpec((BLOCK, n_q), lambda b, i: (b * n_blk + i, 0)),
            pl.BlockSpec((None, seq_len, dup), lambda b, i: (b, 0, 0)),
            pl.BlockSpec((None, dup, seq_len), lambda b, i: (b, 0, 0)),
            pl.BlockSpec((None, n_ctx, dup), lambda b, i: (b, 0, 0)),
            pl.BlockSpec((None, dup, n_ctx), lambda b, i: (b, 0, 0)),
            pl.BlockSpec((None, band, KV_REP * BLOCK), bias_case),
        ],
        out_specs=pl.BlockSpec((BLOCK, n_q), lambda b, i: (b * n_blk + i, 0)),
        compiler_params=_cparams(("parallel", "arbitrary"), 56),
        name="window_attention",
    )(sink, q, k_l.reshape(batch, seq_len, dup), vt_l, k_c.reshape(batch, n_ctx, dup), vt_c,
      _window_bias(seq_len))


def _attn_out_kernel(x_ref, gt_ref, a_ref, wo_ref, lg_ref, lb_ref, o_ref):
    for r in range(x_ref.shape[0] // PROJ_SLAB):
        rows = pl.ds(r * PROJ_SLAB, PROJ_SLAB)
        y = _dot(a_ref[rows, :], wo_ref[...])
        z = ALPHA * x_ref[rows, :] + gt_ref[...] * y
        o_ref[rows, :] = _layer_norm_rows(z, lg_ref[...], lb_ref[...])


def _attn_out(x, mod, cond_of_tile, a, w_out, ln_g, ln_b, tm):
    rows = x.shape[0]
    return pl.pallas_call(
        _attn_out_kernel,
        out_shape=jax.ShapeDtypeStruct((rows, D_MODEL), F32),
        grid=(rows // tm,),
        in_specs=[
            pl.BlockSpec((tm, D_MODEL), lambda i: (i, 0)),
            _mod_spec(1, 5, cond_of_tile),
            pl.BlockSpec((tm, D_MODEL), lambda i: (i, 0)),
            pl.BlockSpec((D_MODEL, D_MODEL), lambda i: (0, 0), pipeline_mode=pl.Buffered(1)),
            _ln_spec(1, 1),
            _ln_spec(1, 1),
        ],
        out_specs=pl.BlockSpec((tm, D_MODEL), lambda i: (i, 0)),
        compiler_params=_cparams(("parallel",), 48),
        name="attn_out",
    )(x, mod, a, w_out, ln_g, ln_b)


def _dft_cos_sin(n):
    k = jnp.arange(n, dtype=jnp.int32)
    ang = ((k[:, None] * k[None, :]) % n).astype(F32) * (2.0 * math.pi / n)
    return jnp.cos(ang), jnp.sin(ang)


def _twiddles(n1, n2):
    k1 = jnp.arange(n1, dtype=jnp.int32)[:, None]
    m2 = jnp.arange(n2, dtype=jnp.int32)[None, :]
    n = n1 * n2
    ang = ((k1 * m2) % n).astype(F32) * (2.0 * math.pi / n)
    shape = (n1, n2, LANES)
    return (jnp.broadcast_to(jnp.cos(ang)[:, :, None], shape),
            jnp.broadcast_to(jnp.sin(ang)[:, :, None], shape))


def _rope_lane_tables(seq_len):
    rows = seq_len // GRID_W
    row = jnp.repeat(jnp.arange(rows, dtype=F32), GRID_W)
    col = jnp.tile(jnp.arange(GRID_W, dtype=F32), rows)
    n_freq = HEAD_DIM // 4
    inv_freq = jnp.power(ROPE_BASE, -jnp.arange(n_freq, dtype=F32) / n_freq)
    ang_r = row[:, None] * inv_freq
    ang_c = col[:, None] * inv_freq
    cos_h = jnp.concatenate([jnp.cos(ang_r), jnp.cos(ang_r), jnp.cos(ang_c), jnp.cos(ang_c)], axis=-1)
    sin_h = jnp.concatenate([-jnp.sin(ang_r), jnp.sin(ang_r), -jnp.sin(ang_c), jnp.sin(ang_c)], axis=-1)
    reps = LANES // HEAD_DIM
    return jnp.tile(cos_h, (1, reps)), jnp.tile(sin_h, (1, reps))


def kernel(x, c, ctx, c_ctx, w_mod, b_mod, ln_g, ln_b, ffn_w_gate, ffn_w_up, ffn_w_down,
           ab_w_in, ab_conv, ab_w_out, attn_w_in, attn_sink, attn_w_out):
    batch, seq_len, _ = x.shape
    n_ctx = ctx.shape[1]
    assert batch == 2 and seq_len % FFN_TM == 0 and seq_len % (FFT_N1 * 8) == 0

    xl = x.reshape(batch * seq_len, D_MODEL)
    xc = ctx.reshape(batch * n_ctx, D_MODEL)
    ctx_rows = batch * n_ctx

    cond = jnp.concatenate([c, c_ctx[None, :], jnp.zeros((N_COND - batch - 1, D_MODEL), F32)], axis=0)
    mod = _modulation(cond, w_mod, b_mod)
    ln_g4 = ln_g.reshape(DEPTH, 3, 1, D_MODEL)
    ln_b4 = ln_b.reshape(DEPTH, 3, 1, D_MODEL)

    ffn_f32 = (ffn_w_gate, ffn_w_up, ffn_w_down)
    first_weights = tuple(w[0, 0].astype(BF16) for w in ffn_f32)
    mixer_f32 = (ab_w_in, ab_w_out, attn_w_in, attn_w_out)

    def lat_cond(tm):
        tiles = seq_len // tm
        return lambda i: i // tiles

    ctx_cond = lambda i: CTX_COND

    def ffn_both(xl, xc, layer, sub, weights, with_ctx, extra_jobs=()):
        nxt = (layer, sub + 1) if sub == 0 else (layer + 1, 0)
        jobs = tuple((w, nxt) for w in ffn_f32) if nxt[0] < DEPTH else ()
        xl, copies = _ffn(xl, mod, layer, sub, lat_cond(FFN_TM), weights, ln_g4, ln_b4, FFN_TM,
                          jobs + tuple(extra_jobs))
        if with_ctx:
            xc, _ = _ffn(xc, mod, layer, sub, ctx_cond, weights, ln_g4, ln_b4, ctx_rows)
        return xl, xc, copies[:len(jobs)], copies[len(jobs):]

    xl, xc, weights, (ab_in, ab_out, at_in, at_out) = ffn_both(
        xl, xc, 0, 0, first_weights, True, [(w, (0,)) for w in mixer_f32])

    cc, sc_ = _dft_cos_sin(FOURIER_GROUP_DIM)
    cs_chan = jnp.concatenate([cc, -sc_], axis=1).astype(BF16)
    n1, n2 = FFT_N1, seq_len // FFT_N1
    c1, s1 = _dft_cos_sin(n1)
    m1 = jnp.concatenate([jnp.concatenate([c1, s1], axis=1),
                          jnp.concatenate([-s1, c1], axis=1)], axis=0).astype(BF16)
    c2, s2 = _dft_cos_sin(n2)
    f2 = jnp.concatenate([c2, s2], axis=1).astype(BF16)
    twc, tws = _twiddles(n1, n2)
    cn, sn = _dft_cos_sin(n_ctx)
    f2_ctx = jnp.concatenate([cn, sn], axis=1).astype(BF16)
    twc_ctx, tws_ctx = _twiddles(1, n_ctx)

    gb, v, vf = _mix_in(xl, mod, lat_cond(PROJ_TM), ab_in, cs_chan, PROJ_TM)
    t = _fft1(vf, m1, batch, n1, n2)
    yb = _fft2(t, twc, tws, f2, batch, n1, n2)
    xl = _mix_out(xl, mod, lat_cond(PROJ_TM), gb, v, yb, ab_conv, ab_out, ln_g4, ln_b4, PROJ_TM, seq_len)

    gb_c, v_c, vf_c = _mix_in(xc, mod, ctx_cond, ab_in, cs_chan, n_ctx)
    yb_c = _fft2(vf_c, twc_ctx, tws_ctx, f2_ctx, batch, 1, n_ctx)
    xc = _mix_out(xc, mod, ctx_cond, gb_c, v_c, yb_c, ab_conv, ab_out, ln_g4, ln_b4, n_ctx, n_ctx)

    xl, xc, weights, _ = ffn_both(xl, xc, 0, 1, weights, True)

    xl, xc, weights, _ = ffn_both(xl, xc, 1, 0, weights, True)

    cos_t, sin_t = _rope_lane_tables(seq_len)
    n_qk = (N_Q_HEADS + N_KV_HEADS) * HEAD_DIM
    wv_t = jnp.transpose(attn_w_in[0, :, n_qk:]).astype(BF16)
    q, k_l, vt_l = _qkv_latent(xl, mod, lat_cond(PROJ_TM), at_in, wv_t, cos_t, sin_t, PROJ_TM, batch, seq_len)
    k_c, vt_c = _kv_context(xc, mod, ctx_cond, at_in, wv_t, n_ctx, batch)
    att = _attention(attn_sink.reshape(N_Q_HEADS), q, k_l, vt_l, k_c, vt_c, batch, seq_len, n_ctx)
    xl = _attn_out(xl, mod, lat_cond(PROJ_TM), att, at_out, ln_g4, ln_b4, PROJ_TM)

    xl, _, _, _ = ffn_both(xl, xc, 1, 1, weights, False)
    return xl.reshape(batch, seq_len, D_MODEL)
```

```python
import functools
import math

import jax
import jax.numpy as jnp
from jax import lax
from jax.experimental import pallas as pl
from jax.experimental.pallas import tpu as pltpu

F32 = jnp.float32
BF16 = jnp.bfloat16

D_MODEL = 2048
GRID_W = 64
N_MOD = 9
D_FF = 5632
CONV_DIM = 1024
FOURIER_DIM = 1024
FOURIER_GROUPS = 8
FOURIER_GROUP_DIM = 128
HEAD_DIM = 64
N_Q_HEADS = 32
N_KV_HEADS = 4
KV_REP = 8
WINDOW = 128
BLOCK = 128
ROPE_BASE = 10000.0
LN_EPS = 1e-5
NEG_INF = -1e30
F32_MAX = float(jnp.finfo(jnp.float32).max)
DEPTH = 2
ALPHA = (2 * DEPTH) ** 0.25
LOG2_E = math.log2(math.e)

LANES = 128
SUBLANES = 8
BF16_SUBLANES = 16
V7X_VMEM_BYTES = 64 * 1024 * 1024
MIB = 1024 * 1024

N_COND = 8
CTX_COND = 2
FFN_TM = 1024
FFN_TF = 512
FFN_SLAB = 256
PROJ_TM = 512
PROJ_SLAB = 256
ATTN_AHEAD = 2
FFT_N1 = 64
FFT1_GROUPS = 2


def _cparams(semantics, vmem_mib):
    return pltpu.CompilerParams(dimension_semantics=semantics,
                                vmem_limit_bytes=min(vmem_mib * MIB, V7X_VMEM_BYTES))


def _dot(a, b):
    return jnp.dot(a, b, preferred_element_type=F32)


def _layer_norm_rows(z, g, b):
    mu = jnp.mean(z, axis=-1, keepdims=True)
    zc = z - mu
    var = jnp.mean(zc * zc, axis=-1, keepdims=True)
    return zc * lax.rsqrt(var + LN_EPS) * g + b


def _mod_kernel(c_ref, w_ref, b_ref, o_ref):
    c = c_ref[...]
    a = (c * jax.nn.sigmoid(c)).astype(BF16)
    o_ref[:, 0, :] = _dot(a, w_ref[...].astype(BF16)) + b_ref[...]


def _modulation(cond, w_mod, b_mod):
    n = N_MOD * D_MODEL
    return pl.pallas_call(
        _mod_kernel,
        out_shape=jax.ShapeDtypeStruct((DEPTH, N_COND, N_MOD, 1, D_MODEL), F32),
        grid=(DEPTH, N_MOD),
        in_specs=[
            pl.BlockSpec((N_COND, D_MODEL), lambda l, j: (0, 0)),
            pl.BlockSpec((None, D_MODEL, D_MODEL), lambda l, j: (l, 0, j)),
            pl.BlockSpec((None, 1, D_MODEL), lambda l, j: (l, 0, j)),
        ],
        out_specs=pl.BlockSpec((None, N_COND, None, 1, D_MODEL), lambda l, j: (l, 0, j, 0, 0)),
        compiler_params=_cparams(("parallel", "parallel"), 48),
        name="modulation",
    )(cond, w_mod, b_mod.reshape(DEPTH, 1, n))


def _mod_spec(layer, k, cond_of_tile):
    return pl.BlockSpec((None, None, None, 1, D_MODEL),
                        lambda i, *_: (layer, cond_of_tile(i), k, 0, 0))


def _ln_spec(layer, k):
    return pl.BlockSpec((None, None, 1, D_MODEL), lambda i, *_: (layer, k, 0, 0))


def _ffn_kernel(x_ref, sh_ref, sc_ref, gt_ref, wg_ref, wu_ref, wd_ref, lg_ref, lb_ref, *rest):
    n_cast = (len(rest) - 2) // 2
    o_ref, xm_ref = rest[n_cast], rest[-1]
    cast_pairs = tuple(zip(rest[:n_cast], rest[n_cast + 1:-1]))
    f = pl.program_id(1)
    last = pl.num_programs(1) - 1
    slabs = [pl.ds(r * FFN_SLAB, FFN_SLAB) for r in range(x_ref.shape[0] // FFN_SLAB)]
    half_gate = 0.5 * gt_ref[...]

    def cast_next_slice():
        for src, dst in cast_pairs:
            dst[...] = src[...].astype(BF16)

    def swiglu_chunk(rows):
        xm = xm_ref[rows, :]
        h_gate = _dot(xm, wg_ref[...])
        h_up = _dot(xm, wu_ref[...])
        act = (h_gate * jax.nn.sigmoid(h_gate) * h_up).astype(BF16)
        return half_gate * _dot(act, wd_ref[...])

    @pl.when(f == 0)
    def _():
        cast_next_slice()
        scale1 = 1.0 + sc_ref[...]
        shift = sh_ref[...]
        for rows in slabs:
            x = x_ref[rows, :]
            xm_ref[rows, :] = (x * scale1 + shift).astype(BF16)
            o_ref[rows, :] = ALPHA * x + swiglu_chunk(rows)

    @pl.when(jnp.logical_and(f > 0, f < last))
    def _():
        cast_next_slice()
        o_ref[...] += swiglu_chunk(slice(None))

    @pl.when(f == last)
    def _():
        cast_next_slice()
        g = lg_ref[...]
        b = lb_ref[...]
        for rows in slabs:
            o_ref[rows, :] = _layer_norm_rows(o_ref[rows, :] + swiglu_chunk(rows), g, b)


def _cast_specs(src, lead, n_f, n_steps):
    n_rows, n_cols = src.shape[-2:]
    rows = BF16_SUBLANES
    while n_rows % rows or n_rows // rows > n_steps:
        rows += BF16_SUBLANES
    last_block = n_rows // rows - 1
    step = lambda i, f: jnp.minimum(i * n_f + f, last_block)
    in_spec = pl.BlockSpec((None,) * len(lead) + (rows, n_cols), lambda i, f: (*lead, step(i, f), 0))
    out_spec = pl.BlockSpec((rows, n_cols), lambda i, f: (step(i, f), 0))
    return in_spec, out_spec, jax.ShapeDtypeStruct((n_rows, n_cols), BF16)


def _ffn(x, mod, layer, sub, cond_of_tile, weights, ln_g, ln_b, tm, cast_jobs=()):
    rows = x.shape[0]
    k0 = 6 * sub
    n_f = D_FF // FFN_TF
    n_steps = (rows // tm) * n_f
    w_in_spec = pl.BlockSpec((D_MODEL, FFN_TF), lambda i, f: (0, f))
    in_specs = [
        pl.BlockSpec((tm, D_MODEL), lambda i, f: (i, 0)),
        _mod_spec(layer, k0, cond_of_tile),
        _mod_spec(layer, k0 + 1, cond_of_tile),
        _mod_spec(layer, k0 + 2, cond_of_tile),
        w_in_spec,
        w_in_spec,
        pl.BlockSpec((FFN_TF, D_MODEL), lambda i, f: (f, 0)),
        _ln_spec(layer, 2 * sub),
        _ln_spec(layer, 2 * sub),
    ]
    out_shape = jax.ShapeDtypeStruct((rows, D_MODEL), F32)
    out_specs = pl.BlockSpec((tm, D_MODEL), lambda i, f: (i, 0))
    args = (x, mod, mod, mod, *weights, ln_g, ln_b)
    semantics = ("parallel", "arbitrary")
    if cast_jobs:
        specs = [_cast_specs(src, lead, n_f, n_steps) for src, lead in cast_jobs]
        in_specs += [s[0] for s in specs]
        out_specs = (out_specs, *[s[1] for s in specs])
        out_shape = (out_shape, *[s[2] for s in specs])
        args += tuple(src for src, _ in cast_jobs)
        semantics = ("arbitrary", "arbitrary")
    out = pl.pallas_call(
        _ffn_kernel,
        out_shape=out_shape,
        grid=(rows // tm, n_f),
        in_specs=in_specs,
        out_specs=out_specs,
        scratch_shapes=[pltpu.VMEM((tm, D_MODEL), BF16)],
        compiler_params=_cparams(semantics, 60),
        name=f"ffn_l{layer}_s{sub}_{rows}",
    )(*args)
    if not cast_jobs:
        return out, ()
    return out[0], tuple(out[1:])


def _mix_in_kernel(x_ref, sh_ref, sc_ref, w_ref, cs_ref, gb_ref, v_ref, f_ref):
    xm = (x_ref[...] * (1.0 + sc_ref[...]) + sh_ref[...]).astype(BF16)
    gb_ref[...] = _dot(xm, w_ref[:, 0:CONV_DIM])
    g_c = _dot(xm, w_ref[:, CONV_DIM:2 * CONV_DIM])
    x_in = _dot(xm, w_ref[:, 2 * CONV_DIM:3 * CONV_DIM])
    v_ref[...] = g_c * x_in
    u_f = _dot(xm, w_ref[:, 3 * CONV_DIM:]).astype(BF16)
    cs = cs_ref[...]
    for g in range(FOURIER_GROUPS):
        lanes = slice(g * FOURIER_GROUP_DIM, (g + 1) * FOURIER_GROUP_DIM)
        res = _dot(u_f[:, lanes], cs)
        f_ref[0, :, lanes] = res[:, :FOURIER_GROUP_DIM]
        f_ref[1, :, lanes] = res[:, FOURIER_GROUP_DIM:]


def _mix_in(x, mod, cond_of_tile, w_in, cs, tm):
    rows = x.shape[0]
    n_in = 3 * CONV_DIM + FOURIER_DIM
    return pl.pallas_call(
        _mix_in_kernel,
        out_shape=(jax.ShapeDtypeStruct((rows, CONV_DIM), F32),
                   jax.ShapeDtypeStruct((rows, CONV_DIM), F32),
                   jax.ShapeDtypeStruct((2, rows, FOURIER_DIM), F32)),
        grid=(rows // tm,),
        in_specs=[
            pl.BlockSpec((tm, D_MODEL), lambda i: (i, 0)),
            _mod_spec(0, 3, cond_of_tile),
            _mod_spec(0, 4, cond_of_tile),
            pl.BlockSpec((D_MODEL, n_in), lambda i: (0, 0), pipeline_mode=pl.Buffered(1)),
            pl.BlockSpec((FOURIER_GROUP_DIM, 2 * FOURIER_GROUP_DIM), lambda i: (0, 0)),
        ],
        out_specs=(pl.BlockSpec((tm, CONV_DIM), lambda i: (i, 0)),
                   pl.BlockSpec((tm, CONV_DIM), lambda i: (i, 0)),
                   pl.BlockSpec((2, tm, FOURIER_DIM), lambda i: (0, i, 0))),
        compiler_params=_cparams(("parallel",), 56),
        name=f"mix_in_{rows}",
    )(x, mod, mod, w_in, cs)


def _fft1_kernel(m_ref, v_ref, o_ref):
    n1 = v_ref.shape[1]
    m = m_ref[...]
    for q in range(v_ref.shape[2]):
        for j in range(SUBLANES):
            v = jnp.concatenate([v_ref[0, :, q, j, :], v_ref[1, :, q, j, :]], axis=0).astype(BF16)
            t = _dot(m, v)
            o_ref[0, :, q, j, :] = t[:n1]
            o_ref[1, :, q, j, :] = t[n1:]


def _fft1(v, m1, batch, n1, n2):
    groups = n2 // SUBLANES
    shape = (2, batch, n1, groups, SUBLANES, FOURIER_DIM)
    spec = pl.BlockSpec((2, None, n1, FFT1_GROUPS, SUBLANES, FOURIER_DIM), lambda b, j: (0, b, 0, j, 0, 0))
    return pl.pallas_call(
        _fft1_kernel,
        out_shape=jax.ShapeDtypeStruct(shape, F32),
        grid=(batch, groups // FFT1_GROUPS),
        in_specs=[pl.BlockSpec((2 * n1, 2 * n1), lambda b, j: (0, 0)), spec],
        out_specs=spec,
        compiler_params=_cparams(("parallel", "parallel"), 48),
        name="fft_stage1",
    )(m1, v.reshape(shape))


def _fft2_one(tr, ti, twc, tws, f, scale):
    reps = FOURIER_DIM // LANES
    c = jnp.concatenate([twc] * reps, axis=1)
    s = jnp.concatenate([tws] * reps, axis=1)
    pr = (tr * c + ti * s).astype(BF16)
    pi = (ti * c - tr * s).astype(BF16)
    return _dot(f, jnp.concatenate([pr, pi], axis=0)) * scale


def _fft2_kernel(t_ref, twc_ref, tws_ref, f_ref, o_ref, *, scale):
    f = f_ref[...]
    for j in range(SUBLANES):
        o_ref[:, j, :] = _fft2_one(t_ref[0, j], t_ref[1, j], twc_ref[j], tws_ref[j], f, scale)


def _fft2_single_kernel(t_ref, twc_ref, tws_ref, f_ref, o_ref, *, scale):
    o_ref[...] = _fft2_one(t_ref[0], t_ref[1], twc_ref[...], tws_ref[...], f_ref[...], scale)


def _fft2(t, twc, tws, f2, batch, n1, n2):
    t5 = t.reshape(2, batch, n1, n2, FOURIER_DIM)
    scale = 1.0 / math.sqrt(n1 * n2 * FOURIER_GROUP_DIM)
    f_spec = pl.BlockSpec((n2, 2 * n2), lambda b, k: (0, 0))
    if n1 == 1:
        out = pl.pallas_call(
            functools.partial(_fft2_single_kernel, scale=scale),
            out_shape=jax.ShapeDtypeStruct((batch, n2, FOURIER_DIM), F32),
            grid=(batch, 1),
            in_specs=[
                pl.BlockSpec((2, None, None, n2, FOURIER_DIM), lambda b, k: (0, b, 0, 0, 0)),
                pl.BlockSpec((None, n2, LANES), lambda b, k: (0, 0, 0)),
                pl.BlockSpec((None, n2, LANES), lambda b, k: (0, 0, 0)),
                f_spec,
            ],
            out_specs=pl.BlockSpec((None, n2, FOURIER_DIM), lambda b, k: (b, 0, 0)),
            compiler_params=_cparams(("parallel", "parallel"), 32),
            name=f"fft_stage2_{n1}x{n2}",
        )(t5, twc, tws, f2)
    else:
        out = pl.pallas_call(
            functools.partial(_fft2_kernel, scale=scale),
            out_shape=jax.ShapeDtypeStruct((batch, n2, n1 // SUBLANES, SUBLANES, FOURIER_DIM), F32),
            grid=(batch, n1 // SUBLANES),
            in_specs=[
                pl.BlockSpec((2, None, SUBLANES, n2, FOURIER_DIM), lambda b, k: (0, b, k, 0, 0)),
                pl.BlockSpec((SUBLANES, n2, LANES), lambda b, k: (k, 0, 0)),
                pl.BlockSpec((SUBLANES, n2, LANES), lambda b, k: (k, 0, 0)),
                f_spec,
            ],
            out_specs=pl.BlockSpec((None, n2, None, SUBLANES, FOURIER_DIM), lambda b, k: (b, 0, k, 0, 0)),
            compiler_params=_cparams(("parallel", "parallel"), 48),
            name=f"fft_stage2_{n1}x{n2}",
        )(t5, twc, tws, f2)
    return out.reshape(batch * n2 * n1, FOURIER_DIM)


def _mix_out_kernel(x_ref, gt_ref, gb_ref, v_ref, vp_ref, vn_ref, yb_ref, wc_ref, wo_ref,
                    lg_ref, lb_ref, o_ref, *, tiles_per_seq):
    i = pl.program_id(0)
    tm = v_ref.shape[0]
    pos = lax.rem(i, tiles_per_seq)
    v = v_ref[...]
    prev_row = jnp.where(pos == 0, 0.0, vp_ref[7:8, :])
    next_row = jnp.where(pos == tiles_per_seq - 1, 0.0, vn_ref[0:1, :])
    row = lax.broadcasted_iota(jnp.int32, v.shape, 0)
    v_prev = jnp.where(row == 0, prev_row, pltpu.roll(v, 1, 0))
    v_next = jnp.where(row == tm - 1, next_row, pltpu.roll(v, tm - 1, 0))
    conv = wc_ref[0:1, :] * v_prev + wc_ref[1:2, :] * v + wc_ref[2:3, :] * v_next
    y_a = (gb_ref[...] * conv).astype(BF16)
    for r in range(tm // PROJ_SLAB):
        rows = pl.ds(r * PROJ_SLAB, PROJ_SLAB)
        y = (_dot(y_a[r * PROJ_SLAB:(r + 1) * PROJ_SLAB], wo_ref[0:CONV_DIM, :])
             + _dot(yb_ref[rows, :].astype(BF16), wo_ref[CONV_DIM:, :]))
        z = ALPHA * x_ref[rows, :] + gt_ref[...] * y
        o_ref[rows, :] = _layer_norm_rows(z, lg_ref[...], lb_ref[...])


def _mix_out(x, mod, cond_of_tile, gb, v, yb, w_conv, w_out, ln_g, ln_b, tm, seq_len):
    rows = x.shape[0]
    sub = 8
    v3 = v.reshape(rows // sub, sub, CONV_DIM)
    step = tm // sub
    last = rows // sub - 1
    return pl.pallas_call(
        functools.partial(_mix_out_kernel, tiles_per_seq=seq_len // tm),
        out_shape=jax.ShapeDtypeStruct((rows, D_MODEL), F32),
        grid=(rows // tm,),
        in_specs=[
            pl.BlockSpec((tm, D_MODEL), lambda i: (i, 0)),
            _mod_spec(0, 5, cond_of_tile),
            pl.BlockSpec((tm, CONV_DIM), lambda i: (i, 0)),
            pl.BlockSpec((tm, CONV_DIM), lambda i: (i, 0)),
            pl.BlockSpec((None, sub, CONV_DIM), lambda i: (jnp.maximum(i * step - 1, 0), 0, 0)),
            pl.BlockSpec((None, sub, CONV_DIM), lambda i: (jnp.minimum((i + 1) * step, last), 0, 0)),
            pl.BlockSpec((tm, FOURIER_DIM), lambda i: (i, 0)),
            pl.BlockSpec((None, 3, CONV_DIM), lambda i: (0, 0, 0)),
            pl.BlockSpec((D_MODEL, D_MODEL), lambda i: (0, 0), pipeline_mode=pl.Buffered(1)),
            _ln_spec(0, 1),
            _ln_spec(0, 1),
        ],
        out_specs=pl.BlockSpec((tm, D_MODEL), lambda i: (i, 0)),
        compiler_params=_cparams(("parallel",), 56),
        name=f"mix_out_{rows}",
    )(x, mod, gb, v, v3, v3, yb, w_conv, w_out, ln_g, ln_b)


def _rope_block(blk, cos, sin, first_half):
    rot = jnp.where(first_half, pltpu.roll(blk, LANES - 16, 1), pltpu.roll(blk, 16, 1))
    return blk * cos + rot * sin


def _dup_heads(blk, low_half):
    swapped = pltpu.roll(blk, HEAD_DIM, 1)
    return jnp.where(low_half, blk, swapped), jnp.where(low_half, swapped, blk)


def _qkv_kernel(x_ref, sh_ref, sc_ref, w_ref, wvt_ref, cos_ref, sin_ref, q_ref, k_ref, vt_ref, *, rope):
    tm = x_ref.shape[0]
    xm = (x_ref[...] * (1.0 + sc_ref[...]) + sh_ref[...]).astype(BF16)
    lane = lax.broadcasted_iota(jnp.int32, (tm, LANES), 1)
    first_half = jnp.bitwise_and(lane, 31) < 16
    low_half = lane < HEAD_DIM
    n_q = N_Q_HEADS * HEAD_DIM
    n_kv = N_KV_HEADS * HEAD_DIM
    if rope:
        cos = cos_ref[...]
        sin = sin_ref[...]
    if q_ref is not None:
        q = _dot(xm, w_ref[:, 0:n_q])
        for j in range(n_q // LANES):
            lanes = slice(j * LANES, (j + 1) * LANES)
            blk = q[:, lanes]
            if rope:
                blk = _rope_block(blk, cos, sin, first_half)
            q_ref[:, lanes] = (blk * (LOG2_E * HEAD_DIM ** -0.5)).astype(BF16)
    k = _dot(xm, w_ref[:, n_q:n_q + n_kv])
    v_t = lax.dot_general(wvt_ref[...], xm, (((1,), (1,)), ((), ())),
                          preferred_element_type=F32).astype(BF16)
    for j in range(n_kv // LANES):
        kb = k[:, j * LANES:(j + 1) * LANES]
        if rope:
            kb = _rope_block(kb, cos, sin, first_half)
        k0, k1 = _dup_heads(kb, low_half)
        k_ref[:, 2 * j * LANES:(2 * j + 1) * LANES] = k0.astype(BF16)
        k_ref[:, (2 * j + 1) * LANES:(2 * j + 2) * LANES] = k1.astype(BF16)
    for g in range(N_KV_HEADS):
        head = v_t[g * HEAD_DIM:(g + 1) * HEAD_DIM, :]
        vt_ref[g * LANES:g * LANES + HEAD_DIM, :] = head
        vt_ref[g * LANES + HEAD_DIM:(g + 1) * LANES, :] = head


def _qkv_latent_kernel(x_ref, sh_ref, sc_ref, w_ref, wvt_ref, cos_ref, sin_ref, q_ref, k_ref, vt_ref):
    _qkv_kernel(x_ref, sh_ref, sc_ref, w_ref, wvt_ref, cos_ref, sin_ref, q_ref, k_ref, vt_ref, rope=True)


def _kv_context_kernel(x_ref, sh_ref, sc_ref, w_ref, wvt_ref, k_ref, vt_ref):
    _qkv_kernel(x_ref, sh_ref, sc_ref, w_ref, wvt_ref, None, None, None, k_ref, vt_ref, rope=False)


def _qkv_latent(x, mod, cond_of_tile, w_in, wv_t, cos_t, sin_t, tm, batch, seq_len):
    rows = x.shape[0]
    n_in = (N_Q_HEADS + 2 * N_KV_HEADS) * HEAD_DIM
    tps = seq_len // tm
    dup = N_KV_HEADS * LANES
    return pl.pallas_call(
        _qkv_latent_kernel,
        out_shape=(jax.ShapeDtypeStruct((rows, N_Q_HEADS * HEAD_DIM), BF16),
                   jax.ShapeDtypeStruct((rows, dup), BF16),
                   jax.ShapeDtypeStruct((batch, dup, seq_len), BF16)),
        grid=(rows // tm,),
        in_specs=[
            pl.BlockSpec((tm, D_MODEL), lambda i: (i, 0)),
            _mod_spec(1, 3, cond_of_tile),
            _mod_spec(1, 4, cond_of_tile),
            pl.BlockSpec((D_MODEL, n_in), lambda i: (0, 0), pipeline_mode=pl.Buffered(1)),
            pl.BlockSpec((N_KV_HEADS * HEAD_DIM, D_MODEL), lambda i: (0, 0)),
            pl.BlockSpec((tm, LANES), lambda i: (lax.rem(i, tps), 0)),
            pl.BlockSpec((tm, LANES), lambda i: (lax.rem(i, tps), 0)),
        ],
        out_specs=(pl.BlockSpec((tm, N_Q_HEADS * HEAD_DIM), lambda i: (i, 0)),
                   pl.BlockSpec((tm, dup), lambda i: (i, 0)),
                   pl.BlockSpec((None, dup, tm), lambda i: (i // tps, 0, lax.rem(i, tps)))),
        compiler_params=_cparams(("parallel",), 48),
        name="qkv_latent",
    )(x, mod, mod, w_in, wv_t, cos_t, sin_t)


def _kv_context(x, mod, cond_of_tile, w_in, wv_t, tm, batch):
    rows = x.shape[0]
    n_in = (N_Q_HEADS + 2 * N_KV_HEADS) * HEAD_DIM
    dup = N_KV_HEADS * LANES
    return pl.pallas_call(
        _kv_context_kernel,
        out_shape=(jax.ShapeDtypeStruct((rows, dup), BF16),
                   jax.ShapeDtypeStruct((batch, dup, tm), BF16)),
        grid=(rows // tm,),
        in_specs=[
            pl.BlockSpec((tm, D_MODEL), lambda i: (i, 0)),
            _mod_spec(1, 3, cond_of_tile),
            _mod_spec(1, 4, cond_of_tile),
            pl.BlockSpec((D_MODEL, n_in), lambda i: (0, 0), pipeline_mode=pl.Buffered(1)),
            pl.BlockSpec((N_KV_HEADS * HEAD_DIM, D_MODEL), lambda i: (0, 0)),
        ],
        out_specs=(pl.BlockSpec((tm, dup), lambda i: (i, 0)),
                   pl.BlockSpec((None, dup, tm), lambda i: (i, 0, 0))),
        compiler_params=_cparams(("parallel",), 40),
        name="kv_context",
    )(x, mod, mod, w_in, wv_t)


def _band_start(i, seq_len):
    return jnp.clip((i - 1) * BLOCK, 0, seq_len - 3 * BLOCK)


def _attn_kernel(sink_ref, q_ref, kl_ref, vlt_ref, kc_ref, vct_ref, cap_ref, o_ref, *, seq_len):
    i = pl.program_id(1)
    band = 3 * BLOCK
    n_ctx = kc_ref.shape[0]
    pairs = KV_REP // 2
    start = pl.multiple_of(_band_start(i, seq_len), BLOCK)
    low_half = lax.broadcasted_iota(jnp.int32, (BLOCK, LANES), 1) < HEAD_DIM
    top_half = lax.broadcasted_iota(jnp.int32, (LANES, BLOCK), 0) < HEAD_DIM

    def scores_t(g):
        kv = slice(g * LANES, (g + 1) * LANES)
        q_parts = []
        for p in range(pairs):
            blk = q_ref[:, (g * pairs + p) * LANES:(g * pairs + p + 1) * LANES]
            q_parts.append(jnp.where(low_half, blk, jnp.zeros_like(blk)))
            q_parts.append(jnp.where(low_half, jnp.zeros_like(blk), blk))
        q_g = jnp.concatenate(q_parts, axis=0)
        k_all = jnp.concatenate([kc_ref[:, kv], kl_ref[pl.ds(start, band), kv]], axis=0)
        return lax.dot_general(k_all, q_g, (((1,), (1,)), ((), ())), preferred_element_type=F32)

    pending = [scores_t(g) for g in range(ATTN_AHEAD)]
    for g in range(N_KV_HEADS):
        if g + ATTN_AHEAD < N_KV_HEADS:
            pending.append(scores_t(g + ATTN_AHEAD))
        kv = slice(g * LANES, (g + 1) * LANES)
        s = pending[g]
        sink = jnp.concatenate([jnp.full((1, BLOCK), sink_ref[g * KV_REP + h] * LOG2_E, F32)
                                for h in range(KV_REP)], axis=1)
        s_ctx = s[:n_ctx]
        s_loc = jnp.minimum(s[n_ctx:], cap_ref[...])
        m = jnp.maximum(jnp.maximum(jnp.max(s_ctx, axis=0, keepdims=True),
                                    jnp.max(s_loc, axis=0, keepdims=True)), sink)
        p_ctx = jnp.exp2(s_ctx - m)
        p_loc = jnp.exp2(s_loc - m)
        probs_t = jnp.concatenate([p_ctx, p_loc], axis=0).astype(BF16)
        vt_all = jnp.concatenate([vct_ref[kv, :], vlt_ref[kv, pl.ds(start, band)]], axis=1)
        vt_aug = jnp.concatenate([vt_all, jnp.ones((BF16_SUBLANES, vt_all.shape[1]), BF16)], axis=0)
        o_aug = _dot(vt_aug, probs_t)
        denom = o_aug[LANES:LANES + 1] + jnp.exp2(sink - m)
        o_t = o_aug[:LANES] / denom
        for p in range(pairs):
            lo = o_t[:, 2 * p * BLOCK:(2 * p + 1) * BLOCK]
            hi = o_t[:, (2 * p + 1) * BLOCK:(2 * p + 2) * BLOCK]
            pair_t = jnp.where(top_half, lo, hi)
            o_ref[:, (g * pairs + p) * LANES:(g * pairs + p + 1) * LANES] = pair_t.T.astype(BF16)


def _window_cap(seq_len):
    band = 3 * BLOCK
    cols = KV_REP * BLOCK
    n_blk = seq_len // BLOCK
    blocks = jnp.array([0, 1, n_blk - 1], jnp.int32)
    delta = (_band_start(blocks, seq_len) - blocks * BLOCK)[:, None, None]
    j = jnp.arange(band, dtype=jnp.int32)[None, :, None]
    a = jnp.bitwise_and(jnp.arange(cols, dtype=jnp.int32), BLOCK - 1)[None, None, :]
    return jnp.where(jnp.abs(j + delta - a) <= WINDOW, F32_MAX, NEG_INF).astype(F32)


def _attention(sink, q, k_l, vt_l, k_c, vt_c, batch, seq_len, n_ctx):
    n_blk = seq_len // BLOCK
    assert n_blk >= 3
    dup = N_KV_HEADS * LANES
    n_q = N_Q_HEADS * HEAD_DIM
    band = 3 * BLOCK

    def cap_case(b, i):
        return (jnp.where(i == 0, 0, jnp.where(i == n_blk - 1, 2, 1)), 0, 0)

    return pl.pallas_call(
        functools.partial(_attn_kernel, seq_len=seq_len),
        out_shape=jax.ShapeDtypeStruct((batch * seq_len, n_q), BF16),
        grid=(batch, n_blk),
        in_specs=[
            pl.BlockSpec(memory_space=pltpu.SMEM),
            pl.BlockSpec((BLOCK, n_q), lambda b, i: (b * n_blk + i, 0)),
            pl.BlockSpec((None, seq_len, dup), lambda b, i: (b, 0, 0)),
            pl.BlockSpec((None, dup, seq_len), lambda b, i: (b, 0, 0)),
            pl.BlockSpec((None, n_ctx, dup), lambda b, i: (b, 0, 0)),
            pl.BlockSpec((None, dup, n_ctx), lambda b, i: (b, 0, 0)),
            pl.BlockSpec((None, band, KV_REP * BLOCK), cap_case),
        ],
        out_specs=pl.BlockSpec((BLOCK, n_q), lambda b, i: (b * n_blk + i, 0)),
        compiler_params=_cparams(("parallel", "arbitrary"), 56),
        name="window_attention",
    )(sink, q, k_l.reshape(batch, seq_len, dup), vt_l, k_c.reshape(batch, n_ctx, dup), vt_c,
      _window_cap(seq_len))


def _attn_out_kernel(x_ref, gt_ref, a_ref, wo_ref, lg_ref, lb_ref, o_ref):
    for r in range(x_ref.shape[0] // PROJ_SLAB):
        rows = pl.ds(r * PROJ_SLAB, PROJ_SLAB)
        y = _dot(a_ref[rows, :], wo_ref[...])
        z = ALPHA * x_ref[rows, :] + gt_ref[...] * y
        o_ref[rows, :] = _layer_norm_rows(z, lg_ref[...], lb_ref[...])


def _attn_out(x, mod, cond_of_tile, a, w_out, ln_g, ln_b, tm):
    rows = x.shape[0]
    return pl.pallas_call(
        _attn_out_kernel,
        out_shape=jax.ShapeDtypeStruct((rows, D_MODEL), F32),
        grid=(rows // tm,),
        in_specs=[
            pl.BlockSpec((tm, D_MODEL), lambda i: (i, 0)),
            _mod_spec(1, 5, cond_of_tile),
            pl.BlockSpec((tm, D_MODEL), lambda i: (i, 0)),
            pl.BlockSpec((D_MODEL, D_MODEL), lambda i: (0, 0), pipeline_mode=pl.Buffered(1)),
            _ln_spec(1, 1),
            _ln_spec(1, 1),
        ],
        out_specs=pl.BlockSpec((tm, D_MODEL), lambda i: (i, 0)),
        compiler_params=_cparams(("parallel",), 48),
        name="attn_out",
    )(x, mod, a, w_out, ln_g, ln_b)


def _dft_cos_sin(n):
    k = jnp.arange(n, dtype=jnp.int32)
    ang = ((k[:, None] * k[None, :]) % n).astype(F32) * (2.0 * math.pi / n)
    return jnp.cos(ang), jnp.sin(ang)


def _twiddles(n1, n2):
    k1 = jnp.arange(n1, dtype=jnp.int32)[:, None]
    m2 = jnp.arange(n2, dtype=jnp.int32)[None, :]
    n = n1 * n2
    ang = ((k1 * m2) % n).astype(F32) * (2.0 * math.pi / n)
    shape = (n1, n2, LANES)
    return (jnp.broadcast_to(jnp.cos(ang)[:, :, None], shape),
            jnp.broadcast_to(jnp.sin(ang)[:, :, None], shape))


def _rope_lane_tables(seq_len):
    rows = seq_len // GRID_W
    row = jnp.repeat(jnp.arange(rows, dtype=F32), GRID_W)
    col = jnp.tile(jnp.arange(GRID_W, dtype=F32), rows)
    n_freq = HEAD_DIM // 4
    inv_freq = jnp.power(ROPE_BASE, -jnp.arange(n_freq, dtype=F32) / n_freq)
    ang_r = row[:, None] * inv_freq
    ang_c = col[:, None] * inv_freq
    cos_h = jnp.concatenate([jnp.cos(ang_r), jnp.cos(ang_r), jnp.cos(ang_c), jnp.cos(ang_c)], axis=-1)
    sin_h = jnp.concatenate([-jnp.sin(ang_r), jnp.sin(ang_r), -jnp.sin(ang_c), jnp.sin(ang_c)], axis=-1)
    reps = LANES // HEAD_DIM
    return jnp.tile(cos_h, (1, reps)), jnp.tile(sin_h, (1, reps))


def kernel(x, c, ctx, c_ctx, w_mod, b_mod, ln_g, ln_b, ffn_w_gate, ffn_w_up, ffn_w_down,
           ab_w_in, ab_conv, ab_w_out, attn_w_in, attn_sink, attn_w_out):
    batch, seq_len, _ = x.shape
    n_ctx = ctx.shape[1]
    assert batch == 2 and seq_len % FFN_TM == 0 and seq_len % (FFT_N1 * 8) == 0

    xl = x.reshape(batch * seq_len, D_MODEL)
    xc = ctx.reshape(batch * n_ctx, D_MODEL)
    ctx_rows = batch * n_ctx

    cond = jnp.concatenate([c, c_ctx[None, :], jnp.zeros((N_COND - batch - 1, D_MODEL), F32)], axis=0)
    mod = _modulation(cond, w_mod, b_mod)
    ln_g4 = ln_g.reshape(DEPTH, 3, 1, D_MODEL)
    ln_b4 = ln_b.reshape(DEPTH, 3, 1, D_MODEL)

    ffn_f32 = (ffn_w_gate, ffn_w_up, ffn_w_down)
    first_weights = tuple(w[0, 0].astype(BF16) for w in ffn_f32)
    mixer_f32 = (ab_w_in, ab_w_out, attn_w_in, attn_w_out)

    def lat_cond(tm):
        tiles = seq_len // tm
        return lambda i: i // tiles

    ctx_cond = lambda i: CTX_COND

    def ffn_both(xl, xc, layer, sub, weights, with_ctx, extra_jobs=()):
        nxt = (layer, sub + 1) if sub == 0 else (layer + 1, 0)
        jobs = tuple((w, nxt) for w in ffn_f32) if nxt[0] < DEPTH else ()
        xl, copies = _ffn(xl, mod, layer, sub, lat_cond(FFN_TM), weights, ln_g4, ln_b4, FFN_TM,
                          jobs + tuple(extra_jobs))
        if with_ctx:
            xc, _ = _ffn(xc, mod, layer, sub, ctx_cond, weights, ln_g4, ln_b4, ctx_rows)
        return xl, xc, copies[:len(jobs)], copies[len(jobs):]

    xl, xc, weights, (ab_in, ab_out, at_in, at_out) = ffn_both(
        xl, xc, 0, 0, first_weights, True, [(w, (0,)) for w in mixer_f32])

    cc, sc_ = _dft_cos_sin(FOURIER_GROUP_DIM)
    cs_chan = jnp.concatenate([cc, -sc_], axis=1).astype(BF16)
    n1, n2 = FFT_N1, seq_len // FFT_N1
    c1, s1 = _dft_cos_sin(n1)
    m1 = jnp.concatenate([jnp.concatenate([c1, s1], axis=1),
                          jnp.concatenate([-s1, c1], axis=1)], axis=0).astype(BF16)
    c2, s2 = _dft_cos_sin(n2)
    f2 = jnp.concatenate([c2, s2], axis=1).astype(BF16)
    twc, tws = _twiddles(n1, n2)
    cn, sn = _dft_cos_sin(n_ctx)
    f2_ctx = jnp.concatenate([cn, sn], axis=1).astype(BF16)
    twc_ctx, tws_ctx = _twiddles(1, n_ctx)

    gb, v, vf = _mix_in(xl, mod, lat_cond(PROJ_TM), ab_in, cs_chan, PROJ_TM)
    t = _fft1(vf, m1, batch, n1, n2)
    yb = _fft2(t, twc, tws, f2, batch, n1, n2)
    xl = _mix_out(xl, mod, lat_cond(PROJ_TM), gb, v, yb, ab_conv, ab_out, ln_g4, ln_b4, PROJ_TM, seq_len)

    gb_c, v_c, vf_c = _mix_in(xc, mod, ctx_cond, ab_in, cs_chan, n_ctx)
    yb_c = _fft2(vf_c, twc_ctx, tws_ctx, f2_ctx, batch, 1, n_ctx)
    xc = _mix_out(xc, mod, ctx_cond, gb_c, v_c, yb_c, ab_conv, ab_out, ln_g4, ln_b4, n_ctx, n_ctx)

    xl, xc, weights, _ = ffn_both(xl, xc, 0, 1, weights, True)

    xl, xc, weights, _ = ffn_both(xl, xc, 1, 0, weights, True)

    cos_t, sin_t = _rope_lane_tables(seq_len)
    n_qk = (N_Q_HEADS + N_KV_HEADS) * HEAD_DIM
    wv_t = jnp.transpose(attn_w_in[0, :, n_qk:]).astype(BF16)
    q, k_l, vt_l = _qkv_latent(xl, mod, lat_cond(PROJ_TM), at_in, wv_t, cos_t, sin_t, PROJ_TM, batch, seq_len)
    k_c, vt_c = _kv_context(xc, mod, ctx_cond, at_in, wv_t, n_ctx, batch)
    att = _attention(attn_sink.reshape(N_Q_HEADS), q, k_l, vt_l, k_c, vt_c, batch, seq_len, n_ctx)
    xl = _attn_out(xl, mod, lat_cond(PROJ_TM), att, at_out, ln_g4, ln_b4, PROJ_TM)

    xl, _, _, _ = ffn_both(xl, xc, 1, 1, weights, False)
    return xl.reshape(batch, seq_len, D_MODEL)
```

```python
import functools
import math

import jax
import jax.numpy as jnp
from jax import lax
from jax.experimental import pallas as pl
from jax.experimental.pallas import tpu as pltpu

F32 = jnp.float32
BF16 = jnp.bfloat16

D_MODEL = 2048
GRID_W = 64
N_MOD = 9
D_FF = 5632
CONV_DIM = 1024
FOURIER_DIM = 1024
FOURIER_GROUPS = 8
FOURIER_GROUP_DIM = 128
HEAD_DIM = 64
ROPE_FREQS = HEAD_DIM // 4
N_Q_HEADS = 32
N_KV_HEADS = 4
KV_REP = 8
WINDOW = 128
BLOCK = 128
ROPE_BASE = 10000.0
LN_EPS = 1e-5
NEG_INF = -1e30
F32_MAX = float(jnp.finfo(jnp.float32).max)
DEPTH = 2
ALPHA = (2 * DEPTH) ** 0.25
LOG2_E = math.log2(math.e)

LANES = 128
SUBLANES = 8
BF16_SUBLANES = 16
V7X_VMEM_BYTES = 64 * 1024 * 1024
MIB = 1024 * 1024

N_COND = 8
CTX_COND = 2
FFN_TM = 1024
FFN_TF = 512
FFN_SLAB = 256
PROJ_TM = 512
PROJ_SLAB = 256
ATTN_AHEAD = 2
FFT_N1 = 64
FFT1_GROUPS = 2


def _cparams(semantics, vmem_mib):
    return pltpu.CompilerParams(dimension_semantics=semantics,
                                vmem_limit_bytes=min(vmem_mib * MIB, V7X_VMEM_BYTES))


def _dot(a, b):
    return jnp.dot(a, b, preferred_element_type=F32)


def _layer_norm_rows(z, g, b):
    mu = jnp.mean(z, axis=-1, keepdims=True)
    zc = z - mu
    var = jnp.mean(zc * zc, axis=-1, keepdims=True)
    return zc * lax.rsqrt(var + LN_EPS) * g + b


def _mod_kernel(c_ref, w_ref, b_ref, o_ref):
    c = c_ref[...]
    a = (c * jax.nn.sigmoid(c)).astype(BF16)
    o_ref[:, 0, :] = _dot(a, w_ref[...].astype(BF16)) + b_ref[...]


def _modulation(cond, w_mod, b_mod):
    n = N_MOD * D_MODEL
    return pl.pallas_call(
        _mod_kernel,
        out_shape=jax.ShapeDtypeStruct((DEPTH, N_COND, N_MOD, 1, D_MODEL), F32),
        grid=(DEPTH, N_MOD),
        in_specs=[
            pl.BlockSpec((N_COND, D_MODEL), lambda l, j: (0, 0)),
            pl.BlockSpec((None, D_MODEL, D_MODEL), lambda l, j: (l, 0, j)),
            pl.BlockSpec((None, 1, D_MODEL), lambda l, j: (l, 0, j)),
        ],
        out_specs=pl.BlockSpec((None, N_COND, None, 1, D_MODEL), lambda l, j: (l, 0, j, 0, 0)),
        compiler_params=_cparams(("parallel", "parallel"), 48),
        name="modulation",
    )(cond, w_mod, b_mod.reshape(DEPTH, 1, n))


def _mod_spec(layer, k, cond_of_tile):
    return pl.BlockSpec((None, None, None, 1, D_MODEL),
                        lambda i, *_: (layer, cond_of_tile(i), k, 0, 0))


def _ln_spec(layer, k):
    return pl.BlockSpec((None, None, 1, D_MODEL), lambda i, *_: (layer, k, 0, 0))


def _ffn_kernel(x_ref, sh_ref, sc_ref, gt_ref, wg_ref, wu_ref, wd_ref, lg_ref, lb_ref, *rest):
    n_cast = (len(rest) - 2) // 2
    o_ref, xm_ref = rest[n_cast], rest[-1]
    cast_pairs = tuple(zip(rest[:n_cast], rest[n_cast + 1:-1]))
    f = pl.program_id(1)
    last = pl.num_programs(1) - 1
    slabs = [pl.ds(r * FFN_SLAB, FFN_SLAB) for r in range(x_ref.shape[0] // FFN_SLAB)]
    half_gate = 0.5 * gt_ref[...]

    def cast_next_slice():
        for src, dst in cast_pairs:
            dst[...] = src[...].astype(BF16)

    def swiglu_chunk(rows):
        xm = xm_ref[rows, :]
        h_gate = _dot(xm, wg_ref[...])
        h_up = _dot(xm, wu_ref[...])
        act = (h_gate * jax.nn.sigmoid(h_gate) * h_up).astype(BF16)
        return half_gate * _dot(act, wd_ref[...])

    @pl.when(f == 0)
    def _():
        cast_next_slice()
        scale1 = 1.0 + sc_ref[...]
        shift = sh_ref[...]
        for rows in slabs:
            x = x_ref[rows, :]
            xm_ref[rows, :] = (x * scale1 + shift).astype(BF16)
            o_ref[rows, :] = ALPHA * x + swiglu_chunk(rows)

    @pl.when(jnp.logical_and(f > 0, f < last))
    def _():
        cast_next_slice()
        o_ref[...] += swiglu_chunk(slice(None))

    @pl.when(f == last)
    def _():
        cast_next_slice()
        g = lg_ref[...]
        b = lb_ref[...]
        for rows in slabs:
            o_ref[rows, :] = _layer_norm_rows(o_ref[rows, :] + swiglu_chunk(rows), g, b)


def _cast_specs(src, lead, n_f, n_steps):
    n_rows, n_cols = src.shape[-2:]
    rows = BF16_SUBLANES
    while n_rows % rows or n_rows // rows > n_steps:
        rows += BF16_SUBLANES
    last_block = n_rows // rows - 1
    step = lambda i, f: jnp.minimum(i * n_f + f, last_block)
    in_spec = pl.BlockSpec((None,) * len(lead) + (rows, n_cols), lambda i, f: (*lead, step(i, f), 0))
    out_spec = pl.BlockSpec((rows, n_cols), lambda i, f: (step(i, f), 0))
    return in_spec, out_spec, jax.ShapeDtypeStruct((n_rows, n_cols), BF16)


def _ffn(x, mod, layer, sub, cond_of_tile, weights, ln_g, ln_b, tm, cast_jobs=()):
    rows = x.shape[0]
    k0 = 6 * sub
    n_f = D_FF // FFN_TF
    n_steps = (rows // tm) * n_f
    w_in_spec = pl.BlockSpec((D_MODEL, FFN_TF), lambda i, f: (0, f))
    in_specs = [
        pl.BlockSpec((tm, D_MODEL), lambda i, f: (i, 0)),
        _mod_spec(layer, k0, cond_of_tile),
        _mod_spec(layer, k0 + 1, cond_of_tile),
        _mod_spec(layer, k0 + 2, cond_of_tile),
        w_in_spec,
        w_in_spec,
        pl.BlockSpec((FFN_TF, D_MODEL), lambda i, f: (f, 0)),
        _ln_spec(layer, 2 * sub),
        _ln_spec(layer, 2 * sub),
    ]
    out_shape = jax.ShapeDtypeStruct((rows, D_MODEL), F32)
    out_specs = pl.BlockSpec((tm, D_MODEL), lambda i, f: (i, 0))
    args = (x, mod, mod, mod, *weights, ln_g, ln_b)
    semantics = ("parallel", "arbitrary")
    if cast_jobs:
        specs = [_cast_specs(src, lead, n_f, n_steps) for src, lead in cast_jobs]
        in_specs += [s[0] for s in specs]
        out_specs = (out_specs, *[s[1] for s in specs])
        out_shape = (out_shape, *[s[2] for s in specs])
        args += tuple(src for src, _ in cast_jobs)
        semantics = ("arbitrary", "arbitrary")
    out = pl.pallas_call(
        _ffn_kernel,
        out_shape=out_shape,
        grid=(rows // tm, n_f),
        in_specs=in_specs,
        out_specs=out_specs,
        scratch_shapes=[pltpu.VMEM((tm, D_MODEL), BF16)],
        compiler_params=_cparams(semantics, 60),
        name=f"ffn_l{layer}_s{sub}_{rows}",
    )(*args)
    if not cast_jobs:
        return out, ()
    return out[0], tuple(out[1:])


def _mix_in_kernel(x_ref, sh_ref, sc_ref, w_ref, cs_ref, gb_ref, v_ref, f_ref):
    xm = (x_ref[...] * (1.0 + sc_ref[...]) + sh_ref[...]).astype(BF16)
    gb_ref[...] = _dot(xm, w_ref[:, 0:CONV_DIM])
    g_c = _dot(xm, w_ref[:, CONV_DIM:2 * CONV_DIM])
    x_in = _dot(xm, w_ref[:, 2 * CONV_DIM:3 * CONV_DIM])
    v_ref[...] = g_c * x_in
    u_f = _dot(xm, w_ref[:, 3 * CONV_DIM:]).astype(BF16)
    cs = cs_ref[...]
    for g in range(FOURIER_GROUPS):
        lanes = slice(g * FOURIER_GROUP_DIM, (g + 1) * FOURIER_GROUP_DIM)
        res = _dot(u_f[:, lanes], cs)
        f_ref[0, :, lanes] = res[:, :FOURIER_GROUP_DIM]
        f_ref[1, :, lanes] = res[:, FOURIER_GROUP_DIM:]


def _mix_in(x, mod, cond_of_tile, w_in, cs, tm):
    rows = x.shape[0]
    n_in = 3 * CONV_DIM + FOURIER_DIM
    return pl.pallas_call(
        _mix_in_kernel,
        out_shape=(jax.ShapeDtypeStruct((rows, CONV_DIM), F32),
                   jax.ShapeDtypeStruct((rows, CONV_DIM), F32),
                   jax.ShapeDtypeStruct((2, rows, FOURIER_DIM), F32)),
        grid=(rows // tm,),
        in_specs=[
            pl.BlockSpec((tm, D_MODEL), lambda i: (i, 0)),
            _mod_spec(0, 3, cond_of_tile),
            _mod_spec(0, 4, cond_of_tile),
            pl.BlockSpec((D_MODEL, n_in), lambda i: (0, 0), pipeline_mode=pl.Buffered(1)),
            pl.BlockSpec((FOURIER_GROUP_DIM, 2 * FOURIER_GROUP_DIM), lambda i: (0, 0)),
        ],
        out_specs=(pl.BlockSpec((tm, CONV_DIM), lambda i: (i, 0)),
                   pl.BlockSpec((tm, CONV_DIM), lambda i: (i, 0)),
                   pl.BlockSpec((2, tm, FOURIER_DIM), lambda i: (0, i, 0))),
        compiler_params=_cparams(("parallel",), 56),
        name=f"mix_in_{rows}",
    )(x, mod, mod, w_in, cs)


def _fft1_kernel(m_ref, v_ref, o_ref):
    n1 = v_ref.shape[1]
    m = m_ref[...]
    for q in range(v_ref.shape[2]):
        for j in range(SUBLANES):
            v = jnp.concatenate([v_ref[0, :, q, j, :], v_ref[1, :, q, j, :]], axis=0).astype(BF16)
            t = _dot(m, v)
            o_ref[0, :, q, j, :] = t[:n1]
            o_ref[1, :, q, j, :] = t[n1:]


def _fft1(v, m1, batch, n1, n2):
    groups = n2 // SUBLANES
    shape = (2, batch, n1, groups, SUBLANES, FOURIER_DIM)
    spec = pl.BlockSpec((2, None, n1, FFT1_GROUPS, SUBLANES, FOURIER_DIM), lambda b, j: (0, b, 0, j, 0, 0))
    return pl.pallas_call(
        _fft1_kernel,
        out_shape=jax.ShapeDtypeStruct(shape, F32),
        grid=(batch, groups // FFT1_GROUPS),
        in_specs=[pl.BlockSpec((2 * n1, 2 * n1), lambda b, j: (0, 0)), spec],
        out_specs=spec,
        compiler_params=_cparams(("parallel", "parallel"), 48),
        name="fft_stage1",
    )(m1, v.reshape(shape))


def _fft2_one(tr, ti, twc, tws, f, scale):
    reps = FOURIER_DIM // LANES
    c = jnp.concatenate([twc] * reps, axis=1)
    s = jnp.concatenate([tws] * reps, axis=1)
    pr = (tr * c + ti * s).astype(BF16)
    pi = (ti * c - tr * s).astype(BF16)
    return _dot(f, jnp.concatenate([pr, pi], axis=0)) * scale


def _fft2_kernel(t_ref, twc_ref, tws_ref, f_ref, o_ref, *, scale):
    f = f_ref[...]
    for j in range(SUBLANES):
        o_ref[:, j, :] = _fft2_one(t_ref[0, j], t_ref[1, j], twc_ref[j], tws_ref[j], f, scale)


def _fft2_single_kernel(t_ref, twc_ref, tws_ref, f_ref, o_ref, *, scale):
    o_ref[...] = _fft2_one(t_ref[0], t_ref[1], twc_ref[...], tws_ref[...], f_ref[...], scale)


def _fft2(t, twc, tws, f2, batch, n1, n2):
    t5 = t.reshape(2, batch, n1, n2, FOURIER_DIM)
    scale = 1.0 / math.sqrt(n1 * n2 * FOURIER_GROUP_DIM)
    f_spec = pl.BlockSpec((n2, 2 * n2), lambda b, k: (0, 0))
    if n1 == 1:
        out = pl.pallas_call(
            functools.partial(_fft2_single_kernel, scale=scale),
            out_shape=jax.ShapeDtypeStruct((batch, n2, FOURIER_DIM), F32),
            grid=(batch, 1),
            in_specs=[
                pl.BlockSpec((2, None, None, n2, FOURIER_DIM), lambda b, k: (0, b, 0, 0, 0)),
                pl.BlockSpec((None, n2, LANES), lambda b, k: (0, 0, 0)),
                pl.BlockSpec((None, n2, LANES), lambda b, k: (0, 0, 0)),
                f_spec,
            ],
            out_specs=pl.BlockSpec((None, n2, FOURIER_DIM), lambda b, k: (b, 0, 0)),
            compiler_params=_cparams(("parallel", "parallel"), 32),
            name=f"fft_stage2_{n1}x{n2}",
        )(t5, twc, tws, f2)
    else:
        out = pl.pallas_call(
            functools.partial(_fft2_kernel, scale=scale),
            out_shape=jax.ShapeDtypeStruct((batch, n2, n1 // SUBLANES, SUBLANES, FOURIER_DIM), F32),
            grid=(batch, n1 // SUBLANES),
            in_specs=[
                pl.BlockSpec((2, None, SUBLANES, n2, FOURIER_DIM), lambda b, k: (0, b, k, 0, 0)),
                pl.BlockSpec((SUBLANES, n2, LANES), lambda b, k: (k, 0, 0)),
                pl.BlockSpec((SUBLANES, n2, LANES), lambda b, k: (k, 0, 0)),
                f_spec,
            ],
            out_specs=pl.BlockSpec((None, n2, None, SUBLANES, FOURIER_DIM), lambda b, k: (b, 0, k, 0, 0)),
            compiler_params=_cparams(("parallel", "parallel"), 48),
            name=f"fft_stage2_{n1}x{n2}",
        )(t5, twc, tws, f2)
    return out.reshape(batch * n2 * n1, FOURIER_DIM)


def _mix_out_kernel(x_ref, gt_ref, gb_ref, v_ref, vp_ref, vn_ref, yb_ref, wc_ref, wo_ref,
                    lg_ref, lb_ref, o_ref, *, tiles_per_seq):
    i = pl.program_id(0)
    tm = v_ref.shape[0]
    pos = lax.rem(i, tiles_per_seq)
    v = v_ref[...]
    prev_row = jnp.where(pos == 0, 0.0, vp_ref[SUBLANES - 1:SUBLANES, :])
    next_row = jnp.where(pos == tiles_per_seq - 1, 0.0, vn_ref[0:1, :])
    row = lax.broadcasted_iota(jnp.int32, v.shape, 0)
    v_prev = jnp.where(row == 0, prev_row, pltpu.roll(v, 1, 0))
    v_next = jnp.where(row == tm - 1, next_row, pltpu.roll(v, tm - 1, 0))
    conv = wc_ref[0:1, :] * v_prev + wc_ref[1:2, :] * v + wc_ref[2:3, :] * v_next
    y_a = (gb_ref[...] * conv).astype(BF16)
    for r in range(tm // PROJ_SLAB):
        rows = pl.ds(r * PROJ_SLAB, PROJ_SLAB)
        y = (_dot(y_a[r * PROJ_SLAB:(r + 1) * PROJ_SLAB], wo_ref[0:CONV_DIM, :])
             + _dot(yb_ref[rows, :].astype(BF16), wo_ref[CONV_DIM:, :]))
        z = ALPHA * x_ref[rows, :] + gt_ref[...] * y
        o_ref[rows, :] = _layer_norm_rows(z, lg_ref[...], lb_ref[...])


def _mix_out(x, mod, cond_of_tile, gb, v, yb, w_conv, w_out, ln_g, ln_b, tm, seq_len):
    rows = x.shape[0]
    sub = SUBLANES
    v3 = v.reshape(rows // sub, sub, CONV_DIM)
    step = tm // sub
    last = rows // sub - 1
    return pl.pallas_call(
        functools.partial(_mix_out_kernel, tiles_per_seq=seq_len // tm),
        out_shape=jax.ShapeDtypeStruct((rows, D_MODEL), F32),
        grid=(rows // tm,),
        in_specs=[
            pl.BlockSpec((tm, D_MODEL), lambda i: (i, 0)),
            _mod_spec(0, 5, cond_of_tile),
            pl.BlockSpec((tm, CONV_DIM), lambda i: (i, 0)),
            pl.BlockSpec((tm, CONV_DIM), lambda i: (i, 0)),
            pl.BlockSpec((None, sub, CONV_DIM), lambda i: (jnp.maximum(i * step - 1, 0), 0, 0)),
            pl.BlockSpec((None, sub, CONV_DIM), lambda i: (jnp.minimum((i + 1) * step, last), 0, 0)),
            pl.BlockSpec((tm, FOURIER_DIM), lambda i: (i, 0)),
            pl.BlockSpec((None, 3, CONV_DIM), lambda i: (0, 0, 0)),
            pl.BlockSpec((D_MODEL, D_MODEL), lambda i: (0, 0), pipeline_mode=pl.Buffered(1)),
            _ln_spec(0, 1),
            _ln_spec(0, 1),
        ],
        out_specs=pl.BlockSpec((tm, D_MODEL), lambda i: (i, 0)),
        compiler_params=_cparams(("parallel",), 56),
        name=f"mix_out_{rows}",
    )(x, mod, gb, v, v3, v3, yb, w_conv, w_out, ln_g, ln_b)


def _rope_block(blk, cos, sin, first_half):
    rot = jnp.where(first_half, pltpu.roll(blk, LANES - ROPE_FREQS, 1), pltpu.roll(blk, ROPE_FREQS, 1))
    return blk * cos + rot * sin


def _dup_heads(blk, low_half):
    swapped = pltpu.roll(blk, HEAD_DIM, 1)
    return jnp.where(low_half, blk, swapped), jnp.where(low_half, swapped, blk)


def _qkv_kernel(x_ref, sh_ref, sc_ref, w_ref, wvt_ref, cos_ref, sin_ref, q_ref, k_ref, vt_ref, *, rope):
    tm = x_ref.shape[0]
    xm = (x_ref[...] * (1.0 + sc_ref[...]) + sh_ref[...]).astype(BF16)
    lane = lax.broadcasted_iota(jnp.int32, (tm, LANES), 1)
    first_half = jnp.bitwise_and(lane, 2 * ROPE_FREQS - 1) < ROPE_FREQS
    low_half = lane < HEAD_DIM
    n_q = N_Q_HEADS * HEAD_DIM
    n_kv = N_KV_HEADS * HEAD_DIM
    if rope:
        cos = cos_ref[...]
        sin = sin_ref[...]
    if q_ref is not None:
        q = _dot(xm, w_ref[:, 0:n_q])
        for j in range(n_q // LANES):
            lanes = slice(j * LANES, (j + 1) * LANES)
            blk = q[:, lanes]
            if rope:
                blk = _rope_block(blk, cos, sin, first_half)
            q_ref[:, lanes] = (blk * (LOG2_E * HEAD_DIM ** -0.5)).astype(BF16)
    k = _dot(xm, w_ref[:, n_q:n_q + n_kv])
    v_t = lax.dot_general(wvt_ref[...], xm, (((1,), (1,)), ((), ())),
                          preferred_element_type=F32).astype(BF16)
    for j in range(n_kv // LANES):
        kb = k[:, j * LANES:(j + 1) * LANES]
        if rope:
            kb = _rope_block(kb, cos, sin, first_half)
        k0, k1 = _dup_heads(kb, low_half)
        k_ref[:, 2 * j * LANES:(2 * j + 1) * LANES] = k0.astype(BF16)
        k_ref[:, (2 * j + 1) * LANES:(2 * j + 2) * LANES] = k1.astype(BF16)
    for g in range(N_KV_HEADS):
        head = v_t[g * HEAD_DIM:(g + 1) * HEAD_DIM, :]
        vt_ref[g * LANES:g * LANES + HEAD_DIM, :] = head
        vt_ref[g * LANES + HEAD_DIM:(g + 1) * LANES, :] = head


def _qkv_latent_kernel(x_ref, sh_ref, sc_ref, w_ref, wvt_ref, cos_ref, sin_ref, q_ref, k_ref, vt_ref):
    _qkv_kernel(x_ref, sh_ref, sc_ref, w_ref, wvt_ref, cos_ref, sin_ref, q_ref, k_ref, vt_ref, rope=True)


def _kv_context_kernel(x_ref, sh_ref, sc_ref, w_ref, wvt_ref, k_ref, vt_ref):
    _qkv_kernel(x_ref, sh_ref, sc_ref, w_ref, wvt_ref, None, None, None, k_ref, vt_ref, rope=False)


def _qkv_latent(x, mod, cond_of_tile, w_in, wv_t, cos_t, sin_t, tm, batch, seq_len):
    rows = x.shape[0]
    n_in = (N_Q_HEADS + 2 * N_KV_HEADS) * HEAD_DIM
    tps = seq_len // tm
    dup = N_KV_HEADS * LANES
    return pl.pallas_call(
        _qkv_latent_kernel,
        out_shape=(jax.ShapeDtypeStruct((rows, N_Q_HEADS * HEAD_DIM), BF16),
                   jax.ShapeDtypeStruct((rows, dup), BF16),
                   jax.ShapeDtypeStruct((batch, dup, seq_len), BF16)),
        grid=(rows // tm,),
        in_specs=[
            pl.BlockSpec((tm, D_MODEL), lambda i: (i, 0)),
            _mod_spec(1, 3, cond_of_tile),
            _mod_spec(1, 4, cond_of_tile),
            pl.BlockSpec((D_MODEL, n_in), lambda i: (0, 0), pipeline_mode=pl.Buffered(1)),
            pl.BlockSpec((N_KV_HEADS * HEAD_DIM, D_MODEL), lambda i: (0, 0)),
            pl.BlockSpec((tm, LANES), lambda i: (lax.rem(i, tps), 0)),
            pl.BlockSpec((tm, LANES), lambda i: (lax.rem(i, tps), 0)),
        ],
        out_specs=(pl.BlockSpec((tm, N_Q_HEADS * HEAD_DIM), lambda i: (i, 0)),
                   pl.BlockSpec((tm, dup), lambda i: (i, 0)),
                   pl.BlockSpec((None, dup, tm), lambda i: (i // tps, 0, lax.rem(i, tps)))),
        compiler_params=_cparams(("parallel",), 48),
        name="qkv_latent",
    )(x, mod, mod, w_in, wv_t, cos_t, sin_t)


def _kv_context(x, mod, cond_of_tile, w_in, wv_t, tm, batch):
    rows = x.shape[0]
    n_in = (N_Q_HEADS + 2 * N_KV_HEADS) * HEAD_DIM
    dup = N_KV_HEADS * LANES
    return pl.pallas_call(
        _kv_context_kernel,
        out_shape=(jax.ShapeDtypeStruct((rows, dup), BF16),
                   jax.ShapeDtypeStruct((batch, dup, tm), BF16)),
        grid=(rows // tm,),
        in_specs=[
            pl.BlockSpec((tm, D_MODEL), lambda i: (i, 0)),
            _mod_spec(1, 3, cond_of_tile),
            _mod_spec(1, 4, cond_of_tile),
            pl.BlockSpec((D_MODEL, n_in), lambda i: (0, 0), pipeline_mode=pl.Buffered(1)),
            pl.BlockSpec((N_KV_HEADS * HEAD_DIM, D_MODEL), lambda i: (0, 0)),
        ],
        out_specs=(pl.BlockSpec((tm, dup), lambda i: (i, 0)),
                   pl.BlockSpec((None, dup, tm), lambda i: (i, 0, 0))),
        compiler_params=_cparams(("parallel",), 40),
        name="kv_context",
    )(x, mod, mod, w_in, wv_t)


def _band_start(i, seq_len):
    return jnp.clip((i - 1) * BLOCK, 0, seq_len - 3 * BLOCK)


def _attn_kernel(sink_ref, q_ref, kl_ref, vlt_ref, kc_ref, vct_ref, cap_ref, o_ref, *, seq_len):
    i = pl.program_id(1)
    band = 3 * BLOCK
    n_ctx = kc_ref.shape[0]
    pairs = KV_REP // 2
    start = pl.multiple_of(_band_start(i, seq_len), BLOCK)
    low_half = lax.broadcasted_iota(jnp.int32, (BLOCK, LANES), 1) < HEAD_DIM
    top_half = lax.broadcasted_iota(jnp.int32, (LANES, BLOCK), 0) < HEAD_DIM

    def scores_t(g):
        kv = slice(g * LANES, (g + 1) * LANES)
        q_parts = []
        for p in range(pairs):
            blk = q_ref[:, (g * pairs + p) * LANES:(g * pairs + p + 1) * LANES]
            q_parts.append(jnp.where(low_half, blk, jnp.zeros_like(blk)))
            q_parts.append(jnp.where(low_half, jnp.zeros_like(blk), blk))
        q_g = jnp.concatenate(q_parts, axis=0)
        k_all = jnp.concatenate([kc_ref[:, kv], kl_ref[pl.ds(start, band), kv]], axis=0)
        return lax.dot_general(k_all, q_g, (((1,), (1,)), ((), ())), preferred_element_type=F32)

    pending = [scores_t(g) for g in range(ATTN_AHEAD)]
    for g in range(N_KV_HEADS):
        if g + ATTN_AHEAD < N_KV_HEADS:
            pending.append(scores_t(g + ATTN_AHEAD))
        kv = slice(g * LANES, (g + 1) * LANES)
        s = pending[g]
        sink = jnp.concatenate([jnp.full((1, BLOCK), sink_ref[g * KV_REP + h] * LOG2_E, F32)
                                for h in range(KV_REP)], axis=1)
        s_ctx = s[:n_ctx]
        s_loc = jnp.minimum(s[n_ctx:], cap_ref[...])
        m = jnp.maximum(jnp.maximum(jnp.max(s_ctx, axis=0, keepdims=True),
                                    jnp.max(s_loc, axis=0, keepdims=True)), sink)
        p_ctx = jnp.exp2(s_ctx - m)
        p_loc = jnp.exp2(s_loc - m)
        probs_t = jnp.concatenate([p_ctx, p_loc], axis=0).astype(BF16)
        vt_all = jnp.concatenate([vct_ref[kv, :], vlt_ref[kv, pl.ds(start, band)]], axis=1)
        vt_aug = jnp.concatenate([vt_all, jnp.ones((BF16_SUBLANES, vt_all.shape[1]), BF16)], axis=0)
        o_aug = _dot(vt_aug, probs_t)
        denom = o_aug[LANES:LANES + 1] + jnp.exp2(sink - m)
        o_t = o_aug[:LANES] / denom
        for p in range(pairs):
            lo = o_t[:, 2 * p * BLOCK:(2 * p + 1) * BLOCK]
            hi = o_t[:, (2 * p + 1) * BLOCK:(2 * p + 2) * BLOCK]
            pair_t = jnp.where(top_half, lo, hi)
            o_ref[:, (g * pairs + p) * LANES:(g * pairs + p + 1) * LANES] = pair_t.T.astype(BF16)


def _window_cap(seq_len):
    band = 3 * BLOCK
    cols = KV_REP * BLOCK
    n_blk = seq_len // BLOCK
    blocks = jnp.array([0, 1, n_blk - 1], jnp.int32)
    delta = (_band_start(blocks, seq_len) - blocks * BLOCK)[:, None, None]
    j = jnp.arange(band, dtype=jnp.int32)[None, :, None]
    a = jnp.bitwise_and(jnp.arange(cols, dtype=jnp.int32), BLOCK - 1)[None, None, :]
    return jnp.where(jnp.abs(j + delta - a) <= WINDOW, F32_MAX, NEG_INF).astype(F32)


def _attention(sink, q, k_l, vt_l, k_c, vt_c, batch, seq_len, n_ctx):
    n_blk = seq_len // BLOCK
    assert n_blk >= 3
    dup = N_KV_HEADS * LANES
    n_q = N_Q_HEADS * HEAD_DIM
    band = 3 * BLOCK

    def cap_case(b, i):
        return (jnp.where(i == 0, 0, jnp.where(i == n_blk - 1, 2, 1)), 0, 0)

    return pl.pallas_call(
        functools.partial(_attn_kernel, seq_len=seq_len),
        out_shape=jax.ShapeDtypeStruct((batch * seq_len, n_q), BF16),
        grid=(batch, n_blk),
        in_specs=[
            pl.BlockSpec(memory_space=pltpu.SMEM),
            pl.BlockSpec((BLOCK, n_q), lambda b, i: (b * n_blk + i, 0)),
            pl.BlockSpec((None, seq_len, dup), lambda b, i: (b, 0, 0)),
            pl.BlockSpec((None, dup, seq_len), lambda b, i: (b, 0, 0)),
            pl.BlockSpec((None, n_ctx, dup), lambda b, i: (b, 0, 0)),
            pl.BlockSpec((None, dup, n_ctx), lambda b, i: (b, 0, 0)),
            pl.BlockSpec((None, band, KV_REP * BLOCK), cap_case),
        ],
        out_specs=pl.BlockSpec((BLOCK, n_q), lambda b, i: (b * n_blk + i, 0)),
        compiler_params=_cparams(("parallel", "arbitrary"), 56),
        name="window_attention",
    )(sink, q, k_l.reshape(batch, seq_len, dup), vt_l, k_c.reshape(batch, n_ctx, dup), vt_c,
      _window_cap(seq_len))


def _attn_out_kernel(x_ref, gt_ref, a_ref, wo_ref, lg_ref, lb_ref, o_ref):
    for r in range(x_ref.shape[0] // PROJ_SLAB):
        rows = pl.ds(r * PROJ_SLAB, PROJ_SLAB)
        y = _dot(a_ref[rows, :], wo_ref[...])
        z = ALPHA * x_ref[rows, :] + gt_ref[...] * y
        o_ref[rows, :] = _layer_norm_rows(z, lg_ref[...], lb_ref[...])


def _attn_out(x, mod, cond_of_tile, a, w_out, ln_g, ln_b, tm):
    rows = x.shape[0]
    return pl.pallas_call(
        _attn_out_kernel,
        out_shape=jax.ShapeDtypeStruct((rows, D_MODEL), F32),
        grid=(rows // tm,),
        in_specs=[
            pl.BlockSpec((tm, D_MODEL), lambda i: (i, 0)),
            _mod_spec(1, 5, cond_of_tile),
            pl.BlockSpec((tm, D_MODEL), lambda i: (i, 0)),
            pl.BlockSpec((D_MODEL, D_MODEL), lambda i: (0, 0), pipeline_mode=pl.Buffered(1)),
            _ln_spec(1, 1),
            _ln_spec(1, 1),
        ],
        out_specs=pl.BlockSpec((tm, D_MODEL), lambda i: (i, 0)),
        compiler_params=_cparams(("parallel",), 48),
        name="attn_out",
    )(x, mod, a, w_out, ln_g, ln_b)


def _dft_cos_sin(n):
    k = jnp.arange(n, dtype=jnp.int32)
    ang = ((k[:, None] * k[None, :]) % n).astype(F32) * (2.0 * math.pi / n)
    return jnp.cos(ang), jnp.sin(ang)


def _twiddles(n1, n2):
    k1 = jnp.arange(n1, dtype=jnp.int32)[:, None]
    m2 = jnp.arange(n2, dtype=jnp.int32)[None, :]
    n = n1 * n2
    ang = ((k1 * m2) % n).astype(F32) * (2.0 * math.pi / n)
    shape = (n1, n2, LANES)
    return (jnp.broadcast_to(jnp.cos(ang)[:, :, None], shape),
            jnp.broadcast_to(jnp.sin(ang)[:, :, None], shape))


def _rope_lane_tables(seq_len):
    rows = seq_len // GRID_W
    row = jnp.repeat(jnp.arange(rows, dtype=F32), GRID_W)
    col = jnp.tile(jnp.arange(GRID_W, dtype=F32), rows)
    n_freq = ROPE_FREQS
    inv_freq = jnp.power(ROPE_BASE, -jnp.arange(n_freq, dtype=F32) / n_freq)
    ang_r = row[:, None] * inv_freq
    ang_c = col[:, None] * inv_freq
    cos_h = jnp.concatenate([jnp.cos(ang_r), jnp.cos(ang_r), jnp.cos(ang_c), jnp.cos(ang_c)], axis=-1)
    sin_h = jnp.concatenate([-jnp.sin(ang_r), jnp.sin(ang_r), -jnp.sin(ang_c), jnp.sin(ang_c)], axis=-1)
    reps = LANES // HEAD_DIM
    return jnp.tile(cos_h, (1, reps)), jnp.tile(sin_h, (1, reps))


def kernel(x, c, ctx, c_ctx, w_mod, b_mod, ln_g, ln_b, ffn_w_gate, ffn_w_up, ffn_w_down,
           ab_w_in, ab_conv, ab_w_out, attn_w_in, attn_sink, attn_w_out):
    batch, seq_len, _ = x.shape
    n_ctx = ctx.shape[1]
    assert batch == 2 and seq_len % FFN_TM == 0 and seq_len % (FFT_N1 * SUBLANES * FFT1_GROUPS) == 0

    xl = x.reshape(batch * seq_len, D_MODEL)
    xc = ctx.reshape(batch * n_ctx, D_MODEL)
    ctx_rows = batch * n_ctx

    cond = jnp.concatenate([c, c_ctx[None, :], jnp.zeros((N_COND - batch - 1, D_MODEL), F32)], axis=0)
    mod = _modulation(cond, w_mod, b_mod)
    ln_g4 = ln_g.reshape(DEPTH, 3, 1, D_MODEL)
    ln_b4 = ln_b.reshape(DEPTH, 3, 1, D_MODEL)

    ffn_f32 = (ffn_w_gate, ffn_w_up, ffn_w_down)
    first_weights = tuple(w[0, 0].astype(BF16) for w in ffn_f32)
    mixer_f32 = (ab_w_in, ab_w_out, attn_w_in, attn_w_out)

    def lat_cond(tm):
        tiles = seq_len // tm
        return lambda i: i // tiles

    ctx_cond = lambda i: CTX_COND

    def ffn_both(xl, xc, layer, sub, weights, with_ctx, extra_jobs=()):
        nxt = (layer, sub + 1) if sub == 0 else (layer + 1, 0)
        jobs = tuple((w, nxt) for w in ffn_f32) if nxt[0] < DEPTH else ()
        xl, copies = _ffn(xl, mod, layer, sub, lat_cond(FFN_TM), weights, ln_g4, ln_b4, FFN_TM,
                          jobs + tuple(extra_jobs))
        if with_ctx:
            xc, _ = _ffn(xc, mod, layer, sub, ctx_cond, weights, ln_g4, ln_b4, ctx_rows)
        return xl, xc, copies[:len(jobs)], copies[len(jobs):]

    xl, xc, weights, (ab_in, ab_out, at_in, at_out) = ffn_both(
        xl, xc, 0, 0, first_weights, True, [(w, (0,)) for w in mixer_f32])

    cc, sc_ = _dft_cos_sin(FOURIER_GROUP_DIM)
    cs_chan = jnp.concatenate([cc, -sc_], axis=1).astype(BF16)
    n1, n2 = FFT_N1, seq_len // FFT_N1
    c1, s1 = _dft_cos_sin(n1)
    m1 = jnp.concatenate([jnp.concatenate([c1, s1], axis=1),
                          jnp.concatenate([-s1, c1], axis=1)], axis=0).astype(BF16)
    c2, s2 = _dft_cos_sin(n2)
    f2 = jnp.concatenate([c2, s2], axis=1).astype(BF16)
    twc, tws = _twiddles(n1, n2)
    cn, sn = _dft_cos_sin(n_ctx)
    f2_ctx = jnp.concatenate([cn, sn], axis=1).astype(BF16)
    twc_ctx, tws_ctx = _twiddles(1, n_ctx)

    gb, v, vf = _mix_in(xl, mod, lat_cond(PROJ_TM), ab_in, cs_chan, PROJ_TM)
    t = _fft1(vf, m1, batch, n1, n2)
    yb = _fft2(t, twc, tws, f2, batch, n1, n2)
    xl = _mix_out(xl, mod, lat_cond(PROJ_TM), gb, v, yb, ab_conv, ab_out, ln_g4, ln_b4, PROJ_TM, seq_len)

    gb_c, v_c, vf_c = _mix_in(xc, mod, ctx_cond, ab_in, cs_chan, n_ctx)
    yb_c = _fft2(vf_c, twc_ctx, tws_ctx, f2_ctx, batch, 1, n_ctx)
    xc = _mix_out(xc, mod, ctx_cond, gb_c, v_c, yb_c, ab_conv, ab_out, ln_g4, ln_b4, n_ctx, n_ctx)

    xl, xc, weights, _ = ffn_both(xl, xc, 0, 1, weights, True)

    xl, xc, weights, _ = ffn_both(xl, xc, 1, 0, weights, True)

    cos_t, sin_t = _rope_lane_tables(seq_len)
    n_qk = (N_Q_HEADS + N_KV_HEADS) * HEAD_DIM
    wv_t = jnp.transpose(attn_w_in[0, :, n_qk:]).astype(BF16)
    q, k_l, vt_l = _qkv_latent(xl, mod, lat_cond(PROJ_TM), at_in, wv_t, cos_t, sin_t, PROJ_TM, batch, seq_len)
    k_c, vt_c = _kv_context(xc, mod, ctx_cond, at_in, wv_t, n_ctx, batch)
    att = _attention(attn_sink.reshape(N_Q_HEADS), q, k_l, vt_l, k_c, vt_c, batch, seq_len, n_ctx)
    xl = _attn_out(xl, mod, lat_cond(PROJ_TM), att, at_out, ln_g4, ln_b4, PROJ_TM)

    xl, _, _, _ = ffn_both(xl, xc, 1, 1, weights, False)
    return xl.reshape(batch, seq_len, D_MODEL)
```

```python
import functools
import math

import jax
import jax.numpy as jnp
from jax import lax
from jax.experimental import pallas as pl
from jax.experimental.pallas import tpu as pltpu

F32 = jnp.float32
BF16 = jnp.bfloat16

D_MODEL = 2048
GRID_W = 64
N_MOD = 9
D_FF = 5632
CONV_DIM = 1024
FOURIER_DIM = 1024
FOURIER_GROUPS = 8
FOURIER_GROUP_DIM = 128
HEAD_DIM = 64
ROPE_FREQS = HEAD_DIM // 4
N_Q_HEADS = 32
N_KV_HEADS = 4
KV_REP = 8
WINDOW = 128
BLOCK = 128
ROPE_BASE = 10000.0
LN_EPS = 1e-5
NEG_INF = -1e30
F32_MAX = float(jnp.finfo(jnp.float32).max)
DEPTH = 2
ALPHA = (2 * DEPTH) ** 0.25
LOG2_E = math.log2(math.e)

LANES = 128
SUBLANES = 8
BF16_SUBLANES = 16
V7X_VMEM_BYTES = 64 * 1024 * 1024
MIB = 1024 * 1024

N_COND = 8
CTX_COND = 2
FFN_TM = 1024
FFN_TF = 512
FFN_SLAB = 256
PROJ_TM = 512
PROJ_SLAB = 256
ATTN_AHEAD = 2
FFT_N1 = 64
FFT1_GROUPS = 2


def _cparams(semantics, vmem_mib):
    return pltpu.CompilerParams(dimension_semantics=semantics,
                                vmem_limit_bytes=min(vmem_mib * MIB, V7X_VMEM_BYTES))


def _dot(a, b):
    return jnp.dot(a, b, preferred_element_type=F32)


def _layer_norm_rows(z, g, b):
    mu = jnp.mean(z, axis=-1, keepdims=True)
    zc = z - mu
    var = jnp.mean(zc * zc, axis=-1, keepdims=True)
    return zc * lax.rsqrt(var + LN_EPS) * g + b


def _mod_kernel(c_ref, w_ref, b_ref, o_ref):
    c = c_ref[...]
    a = (c * jax.nn.sigmoid(c)).astype(BF16)
    o_ref[:, 0, :] = _dot(a, w_ref[...].astype(BF16)) + b_ref[...]


def _modulation(cond, w_mod, b_mod):
    n = N_MOD * D_MODEL
    return pl.pallas_call(
        _mod_kernel,
        out_shape=jax.ShapeDtypeStruct((DEPTH, N_COND, N_MOD, 1, D_MODEL), F32),
        grid=(DEPTH, N_MOD),
        in_specs=[
            pl.BlockSpec((N_COND, D_MODEL), lambda l, j: (0, 0)),
            pl.BlockSpec((None, D_MODEL, D_MODEL), lambda l, j: (l, 0, j)),
            pl.BlockSpec((None, 1, D_MODEL), lambda l, j: (l, 0, j)),
        ],
        out_specs=pl.BlockSpec((None, N_COND, None, 1, D_MODEL), lambda l, j: (l, 0, j, 0, 0)),
        compiler_params=_cparams(("parallel", "parallel"), 48),
        name="modulation",
    )(cond, w_mod, b_mod.reshape(DEPTH, 1, n))


def _mod_spec(layer, k, cond_of_tile):
    return pl.BlockSpec((None, None, None, 1, D_MODEL),
                        lambda i, *_: (layer, cond_of_tile(i), k, 0, 0))


def _ln_spec(layer, k):
    return pl.BlockSpec((None, None, 1, D_MODEL), lambda i, *_: (layer, k, 0, 0))


def _ffn_kernel(x_ref, sh_ref, sc_ref, gt_ref, wg_ref, wu_ref, wd_ref, lg_ref, lb_ref, *rest):
    n_cast = (len(rest) - 2) // 2
    o_ref, xm_ref = rest[n_cast], rest[-1]
    cast_pairs = tuple(zip(rest[:n_cast], rest[n_cast + 1:-1]))
    f = pl.program_id(1)
    last = pl.num_programs(1) - 1
    slabs = [pl.ds(r * FFN_SLAB, FFN_SLAB) for r in range(x_ref.shape[0] // FFN_SLAB)]
    half_gate = 0.5 * gt_ref[...]

    def cast_next_slice():
        for src, dst in cast_pairs:
            dst[...] = src[...].astype(BF16)

    def swiglu_chunk(rows):
        xm = xm_ref[rows, :]
        h_gate = _dot(xm, wg_ref[...])
        h_up = _dot(xm, wu_ref[...])
        act = (h_gate * jax.nn.sigmoid(h_gate) * h_up).astype(BF16)
        return half_gate * _dot(act, wd_ref[...])

    @pl.when(f == 0)
    def _():
        cast_next_slice()
        scale1 = 1.0 + sc_ref[...]
        shift = sh_ref[...]
        for rows in slabs:
            x = x_ref[rows, :]
            xm_ref[rows, :] = (x * scale1 + shift).astype(BF16)
            o_ref[rows, :] = ALPHA * x + swiglu_chunk(rows)

    @pl.when(jnp.logical_and(f > 0, f < last))
    def _():
        cast_next_slice()
        o_ref[...] += swiglu_chunk(slice(None))

    @pl.when(f == last)
    def _():
        cast_next_slice()
        g = lg_ref[...]
        b = lb_ref[...]
        for rows in slabs:
            o_ref[rows, :] = _layer_norm_rows(o_ref[rows, :] + swiglu_chunk(rows), g, b)


def _cast_specs(src, lead, n_f, n_steps):
    n_rows, n_cols = src.shape[-2:]
    rows = BF16_SUBLANES
    while n_rows % rows or n_rows // rows > n_steps:
        rows += BF16_SUBLANES
    last_block = n_rows // rows - 1
    step = lambda i, f: jnp.minimum(i * n_f + f, last_block)
    in_spec = pl.BlockSpec((None,) * len(lead) + (rows, n_cols), lambda i, f: (*lead, step(i, f), 0))
    out_spec = pl.BlockSpec((rows, n_cols), lambda i, f: (step(i, f), 0))
    return in_spec, out_spec, jax.ShapeDtypeStruct((n_rows, n_cols), BF16)


def _ffn(x, mod, layer, sub, cond_of_tile, weights, ln_g, ln_b, tm, cast_jobs=()):
    rows = x.shape[0]
    k0 = 6 * sub
    n_f = D_FF // FFN_TF
    n_steps = (rows // tm) * n_f
    w_in_spec = pl.BlockSpec((D_MODEL, FFN_TF), lambda i, f: (0, f))
    in_specs = [
        pl.BlockSpec((tm, D_MODEL), lambda i, f: (i, 0)),
        _mod_spec(layer, k0, cond_of_tile),
        _mod_spec(layer, k0 + 1, cond_of_tile),
        _mod_spec(layer, k0 + 2, cond_of_tile),
        w_in_spec,
        w_in_spec,
        pl.BlockSpec((FFN_TF, D_MODEL), lambda i, f: (f, 0)),
        _ln_spec(layer, 2 * sub),
        _ln_spec(layer, 2 * sub),
    ]
    out_shape = jax.ShapeDtypeStruct((rows, D_MODEL), F32)
    out_specs = pl.BlockSpec((tm, D_MODEL), lambda i, f: (i, 0))
    args = (x, mod, mod, mod, *weights, ln_g, ln_b)
    semantics = ("parallel", "arbitrary")
    if cast_jobs:
        specs = [_cast_specs(src, lead, n_f, n_steps) for src, lead in cast_jobs]
        in_specs += [s[0] for s in specs]
        out_specs = (out_specs, *[s[1] for s in specs])
        out_shape = (out_shape, *[s[2] for s in specs])
        args += tuple(src for src, _ in cast_jobs)
        semantics = ("arbitrary", "arbitrary")
    out = pl.pallas_call(
        _ffn_kernel,
        out_shape=out_shape,
        grid=(rows // tm, n_f),
        in_specs=in_specs,
        out_specs=out_specs,
        scratch_shapes=[pltpu.VMEM((tm, D_MODEL), BF16)],
        compiler_params=_cparams(semantics, 60),
        name=f"ffn_l{layer}_s{sub}_{rows}",
    )(*args)
    if not cast_jobs:
        return out, ()
    return out[0], tuple(out[1:])


def _mix_in_kernel(x_ref, sh_ref, sc_ref, w_ref, cs_ref, gb_ref, v_ref, f_ref):
    xm = (x_ref[...] * (1.0 + sc_ref[...]) + sh_ref[...]).astype(BF16)
    gb_ref[...] = _dot(xm, w_ref[:, 0:CONV_DIM])
    g_c = _dot(xm, w_ref[:, CONV_DIM:2 * CONV_DIM])
    x_in = _dot(xm, w_ref[:, 2 * CONV_DIM:3 * CONV_DIM])
    v_ref[...] = g_c * x_in
    u_f = _dot(xm, w_ref[:, 3 * CONV_DIM:]).astype(BF16)
    cs = cs_ref[...]
    for g in range(FOURIER_GROUPS):
        lanes = slice(g * FOURIER_GROUP_DIM, (g + 1) * FOURIER_GROUP_DIM)
        res = _dot(u_f[:, lanes], cs)
        f_ref[0, :, lanes] = res[:, :FOURIER_GROUP_DIM]
        f_ref[1, :, lanes] = res[:, FOURIER_GROUP_DIM:]


def _mix_in(x, mod, cond_of_tile, w_in, cs, tm):
    rows = x.shape[0]
    n_in = 3 * CONV_DIM + FOURIER_DIM
    return pl.pallas_call(
        _mix_in_kernel,
        out_shape=(jax.ShapeDtypeStruct((rows, CONV_DIM), F32),
                   jax.ShapeDtypeStruct((rows, CONV_DIM), F32),
                   jax.ShapeDtypeStruct((2, rows, FOURIER_DIM), F32)),
        grid=(rows // tm,),
        in_specs=[
            pl.BlockSpec((tm, D_MODEL), lambda i: (i, 0)),
            _mod_spec(0, 3, cond_of_tile),
            _mod_spec(0, 4, cond_of_tile),
            pl.BlockSpec((D_MODEL, n_in), lambda i: (0, 0), pipeline_mode=pl.Buffered(1)),
            pl.BlockSpec((FOURIER_GROUP_DIM, 2 * FOURIER_GROUP_DIM), lambda i: (0, 0)),
        ],
        out_specs=(pl.BlockSpec((tm, CONV_DIM), lambda i: (i, 0)),
                   pl.BlockSpec((tm, CONV_DIM), lambda i: (i, 0)),
                   pl.BlockSpec((2, tm, FOURIER_DIM), lambda i: (0, i, 0))),
        compiler_params=_cparams(("parallel",), 56),
        name=f"mix_in_{rows}",
    )(x, mod, mod, w_in, cs)


def _fft1_kernel(m_ref, v_ref, o_ref):
    n1 = v_ref.shape[1]
    m = m_ref[...]
    for q in range(v_ref.shape[2]):
        for j in range(SUBLANES):
            v = jnp.concatenate([v_ref[0, :, q, j, :], v_ref[1, :, q, j, :]], axis=0).astype(BF16)
            t = _dot(m, v)
            o_ref[:, q, j, :] = pltpu.pack_elementwise([t[:n1], t[n1:]], packed_dtype=BF16)


def _fft1(v, m1, batch, n1, n2):
    groups = n2 // SUBLANES
    shape = (2, batch, n1, groups, SUBLANES, FOURIER_DIM)
    spec = pl.BlockSpec((2, None, n1, FFT1_GROUPS, SUBLANES, FOURIER_DIM), lambda b, j: (0, b, 0, j, 0, 0))
    return pl.pallas_call(
        _fft1_kernel,
        out_shape=jax.ShapeDtypeStruct(shape[1:], jnp.uint32),
        grid=(batch, groups // FFT1_GROUPS),
        in_specs=[pl.BlockSpec((2 * n1, 2 * n1), lambda b, j: (0, 0)), spec],
        out_specs=pl.BlockSpec((None, n1, FFT1_GROUPS, SUBLANES, FOURIER_DIM), lambda b, j: (b, 0, j, 0, 0)),
        compiler_params=_cparams(("parallel", "parallel"), 48),
        name="fft_stage1",
    )(m1, v.reshape(shape))


def _fft2_one(tr, ti, twc, tws, f, scale):
    reps = FOURIER_DIM // LANES
    c = jnp.concatenate([twc] * reps, axis=1)
    s = jnp.concatenate([tws] * reps, axis=1)
    pr = (tr * c + ti * s).astype(BF16)
    pi = (ti * c - tr * s).astype(BF16)
    return _dot(f, jnp.concatenate([pr, pi], axis=0)) * scale


def _fft2_kernel(t_ref, twc_ref, tws_ref, f_ref, o_ref, *, scale):
    f = f_ref[...]
    for j in range(SUBLANES):
        p = t_ref[j]
        tr = pltpu.unpack_elementwise(p, index=0, packed_dtype=BF16, unpacked_dtype=F32)
        ti = pltpu.unpack_elementwise(p, index=1, packed_dtype=BF16, unpacked_dtype=F32)
        o_ref[:, j, :] = _fft2_one(tr, ti, twc_ref[j], tws_ref[j], f, scale)


def _fft2_single_kernel(t_ref, twc_ref, tws_ref, f_ref, o_ref, *, scale):
    o_ref[...] = _fft2_one(t_ref[0], t_ref[1], twc_ref[...], tws_ref[...], f_ref[...], scale)


def _fft2(t, twc, tws, f2, batch, n1, n2):
    t5 = t.reshape((2,) * (n1 == 1) + (batch, n1, n2, FOURIER_DIM))
    scale = 1.0 / math.sqrt(n1 * n2 * FOURIER_GROUP_DIM)
    f_spec = pl.BlockSpec((n2, 2 * n2), lambda b, k: (0, 0))
    if n1 == 1:
        out = pl.pallas_call(
            functools.partial(_fft2_single_kernel, scale=scale),
            out_shape=jax.ShapeDtypeStruct((batch, n2, FOURIER_DIM), F32),
            grid=(batch, 1),
            in_specs=[
                pl.BlockSpec((2, None, None, n2, FOURIER_DIM), lambda b, k: (0, b, 0, 0, 0)),
                pl.BlockSpec((None, n2, LANES), lambda b, k: (0, 0, 0)),
                pl.BlockSpec((None, n2, LANES), lambda b, k: (0, 0, 0)),
                f_spec,
            ],
            out_specs=pl.BlockSpec((None, n2, FOURIER_DIM), lambda b, k: (b, 0, 0)),
            compiler_params=_cparams(("parallel", "parallel"), 32),
            name=f"fft_stage2_{n1}x{n2}",
        )(t5, twc, tws, f2)
    else:
        out = pl.pallas_call(
            functools.partial(_fft2_kernel, scale=scale),
            out_shape=jax.ShapeDtypeStruct((batch, n2, n1 // SUBLANES, SUBLANES, FOURIER_DIM), F32),
            grid=(batch, n1 // SUBLANES),
            in_specs=[
                pl.BlockSpec((None, SUBLANES, n2, FOURIER_DIM), lambda b, k: (b, k, 0, 0)),
                pl.BlockSpec((SUBLANES, n2, LANES), lambda b, k: (k, 0, 0)),
                pl.BlockSpec((SUBLANES, n2, LANES), lambda b, k: (k, 0, 0)),
                f_spec,
            ],
            out_specs=pl.BlockSpec((None, n2, None, SUBLANES, FOURIER_DIM), lambda b, k: (b, 0, k, 0, 0)),
            compiler_params=_cparams(("parallel", "parallel"), 48),
            name=f"fft_stage2_{n1}x{n2}",
        )(t5, twc, tws, f2)
    return out.reshape(batch * n2 * n1, FOURIER_DIM)


def _mix_out_kernel(x_ref, gt_ref, gb_ref, v_ref, vp_ref, vn_ref, yb_ref, wc_ref, wo_ref,
                    lg_ref, lb_ref, o_ref, *, tiles_per_seq):
    i = pl.program_id(0)
    tm = v_ref.shape[0]
    pos = lax.rem(i, tiles_per_seq)
    v = v_ref[...]
    prev_row = jnp.where(pos == 0, 0.0, vp_ref[SUBLANES - 1:SUBLANES, :])
    next_row = jnp.where(pos == tiles_per_seq - 1, 0.0, vn_ref[0:1, :])
    row = lax.broadcasted_iota(jnp.int32, v.shape, 0)
    v_prev = jnp.where(row == 0, prev_row, pltpu.roll(v, 1, 0))
    v_next = jnp.where(row == tm - 1, next_row, pltpu.roll(v, tm - 1, 0))
    conv = wc_ref[0:1, :] * v_prev + wc_ref[1:2, :] * v + wc_ref[2:3, :] * v_next
    y_a = (gb_ref[...] * conv).astype(BF16)
    for r in range(tm // PROJ_SLAB):
        rows = pl.ds(r * PROJ_SLAB, PROJ_SLAB)
        y = (_dot(y_a[r * PROJ_SLAB:(r + 1) * PROJ_SLAB], wo_ref[0:CONV_DIM, :])
             + _dot(yb_ref[rows, :].astype(BF16), wo_ref[CONV_DIM:, :]))
        z = ALPHA * x_ref[rows, :] + gt_ref[...] * y
        o_ref[rows, :] = _layer_norm_rows(z, lg_ref[...], lb_ref[...])


def _mix_out(x, mod, cond_of_tile, gb, v, yb, w_conv, w_out, ln_g, ln_b, tm, seq_len):
    rows = x.shape[0]
    sub = SUBLANES
    v3 = v.reshape(rows // sub, sub, CONV_DIM)
    step = tm // sub
    last = rows // sub - 1
    return pl.pallas_call(
        functools.partial(_mix_out_kernel, tiles_per_seq=seq_len // tm),
        out_shape=jax.ShapeDtypeStruct((rows, D_MODEL), F32),
        grid=(rows // tm,),
        in_specs=[
            pl.BlockSpec((tm, D_MODEL), lambda i: (i, 0)),
            _mod_spec(0, 5, cond_of_tile),
            pl.BlockSpec((tm, CONV_DIM), lambda i: (i, 0)),
            pl.BlockSpec((tm, CONV_DIM), lambda i: (i, 0)),
            pl.BlockSpec((None, sub, CONV_DIM), lambda i: (jnp.maximum(i * step - 1, 0), 0, 0)),
            pl.BlockSpec((None, sub, CONV_DIM), lambda i: (jnp.minimum((i + 1) * step, last), 0, 0)),
            pl.BlockSpec((tm, FOURIER_DIM), lambda i: (i, 0)),
            pl.BlockSpec((None, 3, CONV_DIM), lambda i: (0, 0, 0)),
            pl.BlockSpec((D_MODEL, D_MODEL), lambda i: (0, 0), pipeline_mode=pl.Buffered(1)),
            _ln_spec(0, 1),
            _ln_spec(0, 1),
        ],
        out_specs=pl.BlockSpec((tm, D_MODEL), lambda i: (i, 0)),
        compiler_params=_cparams(("parallel",), 56),
        name=f"mix_out_{rows}",
    )(x, mod, gb, v, v3, v3, yb, w_conv, w_out, ln_g, ln_b)


def _rope_block(blk, cos, sin, first_half):
    rot = jnp.where(first_half, pltpu.roll(blk, LANES - ROPE_FREQS, 1), pltpu.roll(blk, ROPE_FREQS, 1))
    return blk * cos + rot * sin


def _dup_heads(blk, low_half):
    swapped = pltpu.roll(blk, HEAD_DIM, 1)
    return jnp.where(low_half, blk, swapped), jnp.where(low_half, swapped, blk)


def _qkv_kernel(x_ref, sh_ref, sc_ref, w_ref, wvt_ref, cos_ref, sin_ref, q_ref, k_ref, vt_ref, *, rope):
    tm = x_ref.shape[0]
    xm = (x_ref[...] * (1.0 + sc_ref[...]) + sh_ref[...]).astype(BF16)
    lane = lax.broadcasted_iota(jnp.int32, (tm, LANES), 1)
    first_half = jnp.bitwise_and(lane, 2 * ROPE_FREQS - 1) < ROPE_FREQS
    low_half = lane < HEAD_DIM
    n_q = N_Q_HEADS * HEAD_DIM
    n_kv = N_KV_HEADS * HEAD_DIM
    if rope:
        cos = cos_ref[...]
        sin = sin_ref[...]
    if q_ref is not None:
        q = _dot(xm, w_ref[:, 0:n_q])
        for j in range(n_q // LANES):
            lanes = slice(j * LANES, (j + 1) * LANES)
            blk = q[:, lanes]
            if rope:
                blk = _rope_block(blk, cos, sin, first_half)
            q_ref[:, lanes] = (blk * (LOG2_E * HEAD_DIM ** -0.5)).astype(BF16)
    k = _dot(xm, w_ref[:, n_q:n_q + n_kv])
    v_t = lax.dot_general(wvt_ref[...], xm, (((1,), (1,)), ((), ())),
                          preferred_element_type=F32).astype(BF16)
    for j in range(n_kv // LANES):
        kb = k[:, j * LANES:(j + 1) * LANES]
        if rope:
            kb = _rope_block(kb, cos, sin, first_half)
        k0, k1 = _dup_heads(kb, low_half)
        k_ref[:, 2 * j * LANES:(2 * j + 1) * LANES] = k0.astype(BF16)
        k_ref[:, (2 * j + 1) * LANES:(2 * j + 2) * LANES] = k1.astype(BF16)
    for g in range(N_KV_HEADS):
        head = v_t[g * HEAD_DIM:(g + 1) * HEAD_DIM, :]
        vt_ref[g * LANES:g * LANES + HEAD_DIM, :] = head
        vt_ref[g * LANES + HEAD_DIM:(g + 1) * LANES, :] = head


def _qkv_latent_kernel(x_ref, sh_ref, sc_ref, w_ref, wvt_ref, cos_ref, sin_ref, q_ref, k_ref, vt_ref):
    _qkv_kernel(x_ref, sh_ref, sc_ref, w_ref, wvt_ref, cos_ref, sin_ref, q_ref, k_ref, vt_ref, rope=True)


def _kv_context_kernel(x_ref, sh_ref, sc_ref, w_ref, wvt_ref, k_ref, vt_ref):
    _qkv_kernel(x_ref, sh_ref, sc_ref, w_ref, wvt_ref, None, None, None, k_ref, vt_ref, rope=False)


def _qkv_latent(x, mod, cond_of_tile, w_in, wv_t, cos_t, sin_t, tm, batch, seq_len):
    rows = x.shape[0]
    n_in = (N_Q_HEADS + 2 * N_KV_HEADS) * HEAD_DIM
    tps = seq_len // tm
    dup = N_KV_HEADS * LANES
    return pl.pallas_call(
        _qkv_latent_kernel,
        out_shape=(jax.ShapeDtypeStruct((rows, N_Q_HEADS * HEAD_DIM), BF16),
                   jax.ShapeDtypeStruct((rows, dup), BF16),
                   jax.ShapeDtypeStruct((batch, dup, seq_len), BF16)),
        grid=(rows // tm,),
        in_specs=[
            pl.BlockSpec((tm, D_MODEL), lambda i: (i, 0)),
            _mod_spec(1, 3, cond_of_tile),
            _mod_spec(1, 4, cond_of_tile),
            pl.BlockSpec((D_MODEL, n_in), lambda i: (0, 0), pipeline_mode=pl.Buffered(1)),
            pl.BlockSpec((N_KV_HEADS * HEAD_DIM, D_MODEL), lambda i: (0, 0)),
            pl.BlockSpec((tm, LANES), lambda i: (lax.rem(i, tps), 0)),
            pl.BlockSpec((tm, LANES), lambda i: (lax.rem(i, tps), 0)),
        ],
        out_specs=(pl.BlockSpec((tm, N_Q_HEADS * HEAD_DIM), lambda i: (i, 0)),
                   pl.BlockSpec((tm, dup), lambda i: (i, 0)),
                   pl.BlockSpec((None, dup, tm), lambda i: (i // tps, 0, lax.rem(i, tps)))),
        compiler_params=_cparams(("parallel",), 48),
        name="qkv_latent",
    )(x, mod, mod, w_in, wv_t, cos_t, sin_t)


def _kv_context(x, mod, cond_of_tile, w_in, wv_t, tm, batch):
    rows = x.shape[0]
    n_in = (N_Q_HEADS + 2 * N_KV_HEADS) * HEAD_DIM
    dup = N_KV_HEADS * LANES
    return pl.pallas_call(
        _kv_context_kernel,
        out_shape=(jax.ShapeDtypeStruct((rows, dup), BF16),
                   jax.ShapeDtypeStruct((batch, dup, tm), BF16)),
        grid=(rows // tm,),
        in_specs=[
            pl.BlockSpec((tm, D_MODEL), lambda i: (i, 0)),
            _mod_spec(1, 3, cond_of_tile),
            _mod_spec(1, 4, cond_of_tile),
            pl.BlockSpec((D_MODEL, n_in), lambda i: (0, 0), pipeline_mode=pl.Buffered(1)),
            pl.BlockSpec((N_KV_HEADS * HEAD_DIM, D_MODEL), lambda i: (0, 0)),
        ],
        out_specs=(pl.BlockSpec((tm, dup), lambda i: (i, 0)),
                   pl.BlockSpec((None, dup, tm), lambda i: (i, 0, 0))),
        compiler_params=_cparams(("parallel",), 40),
        name="kv_context",
    )(x, mod, mod, w_in, wv_t)


def _band_start(i, seq_len):
    return jnp.clip((i - 1) * BLOCK, 0, seq_len - 3 * BLOCK)


def _attn_kernel(sink_ref, q_ref, kl_ref, vlt_ref, kc_ref, vct_ref, cap_ref, o_ref, *, seq_len):
    i = pl.program_id(1)
    band = 3 * BLOCK
    n_ctx = kc_ref.shape[0]
    pairs = KV_REP // 2
    start = pl.multiple_of(_band_start(i, seq_len), BLOCK)
    low_half = lax.broadcasted_iota(jnp.int32, (BLOCK, LANES), 1) < HEAD_DIM
    top_half = lax.broadcasted_iota(jnp.int32, (LANES, BLOCK), 0) < HEAD_DIM

    def scores_t(g):
        kv = slice(g * LANES, (g + 1) * LANES)
        q_parts = []
        for p in range(pairs):
            blk = q_ref[:, (g * pairs + p) * LANES:(g * pairs + p + 1) * LANES]
            q_parts.append(jnp.where(low_half, blk, jnp.zeros_like(blk)))
            q_parts.append(jnp.where(low_half, jnp.zeros_like(blk), blk))
        q_g = jnp.concatenate(q_parts, axis=0)
        k_all = jnp.concatenate([kc_ref[:, kv], kl_ref[pl.ds(start, band), kv]], axis=0)
        return lax.dot_general(k_all, q_g, (((1,), (1,)), ((), ())), preferred_element_type=F32)

    pending = [scores_t(g) for g in range(ATTN_AHEAD)]
    for g in range(N_KV_HEADS):
        if g + ATTN_AHEAD < N_KV_HEADS:
            pending.append(scores_t(g + ATTN_AHEAD))
        kv = slice(g * LANES, (g + 1) * LANES)
        s = pending[g]
        sink = jnp.concatenate([jnp.full((1, BLOCK), sink_ref[g * KV_REP + h] * LOG2_E, F32)
                                for h in range(KV_REP)], axis=1)
        s_ctx = s[:n_ctx]
        s_loc = jnp.minimum(s[n_ctx:], cap_ref[...])
        m = jnp.maximum(jnp.maximum(jnp.max(s_ctx, axis=0, keepdims=True),
                                    jnp.max(s_loc, axis=0, keepdims=True)), sink)
        p_ctx = jnp.exp2(s_ctx - m)
        p_loc = jnp.exp2(s_loc - m)
        probs_t = jnp.concatenate([p_ctx, p_loc], axis=0).astype(BF16)
        vt_all = jnp.concatenate([vct_ref[kv, :], vlt_ref[kv, pl.ds(start, band)]], axis=1)
        vt_aug = jnp.concatenate([vt_all, jnp.ones((BF16_SUBLANES, vt_all.shape[1]), BF16)], axis=0)
        o_aug = _dot(vt_aug, probs_t)
        denom = o_aug[LANES:LANES + 1] + jnp.exp2(sink - m)
        o_t = o_aug[:LANES] / denom
        for p in range(pairs):
            lo = o_t[:, 2 * p * BLOCK:(2 * p + 1) * BLOCK]
            hi = o_t[:, (2 * p + 1) * BLOCK:(2 * p + 2) * BLOCK]
            pair_t = jnp.where(top_half, lo, hi)
            o_ref[:, (g * pairs + p) * LANES:(g * pairs + p + 1) * LANES] = pair_t.T.astype(BF16)


def _window_cap(seq_len):
    band = 3 * BLOCK
    cols = KV_REP * BLOCK
    n_blk = seq_len // BLOCK
    blocks = jnp.array([0, 1, n_blk - 1], jnp.int32)
    delta = (_band_start(blocks, seq_len) - blocks * BLOCK)[:, None, None]
    j = jnp.arange(band, dtype=jnp.int32)[None, :, None]
    a = jnp.bitwise_and(jnp.arange(cols, dtype=jnp.int32), BLOCK - 1)[None, None, :]
    return jnp.where(jnp.abs(j + delta - a) <= WINDOW, F32_MAX, NEG_INF).astype(F32)


def _attention(sink, q, k_l, vt_l, k_c, vt_c, batch, seq_len, n_ctx):
    n_blk = seq_len // BLOCK
    assert n_blk >= 3
    dup = N_KV_HEADS * LANES
    n_q = N_Q_HEADS * HEAD_DIM
    band = 3 * BLOCK

    def cap_case(b, i):
        return (jnp.where(i == 0, 0, jnp.where(i == n_blk - 1, 2, 1)), 0, 0)

    return pl.pallas_call(
        functools.partial(_attn_kernel, seq_len=seq_len),
        out_shape=jax.ShapeDtypeStruct((batch * seq_len, n_q), BF16),
        grid=(batch, n_blk),
        in_specs=[
            pl.BlockSpec(memory_space=pltpu.SMEM),
            pl.BlockSpec((BLOCK, n_q), lambda b, i: (b * n_blk + i, 0)),
            pl.BlockSpec((None, seq_len, dup), lambda b, i: (b, 0, 0)),
            pl.BlockSpec((None, dup, seq_len), lambda b, i: (b, 0, 0)),
            pl.BlockSpec((None, n_ctx, dup), lambda b, i: (b, 0, 0)),
            pl.BlockSpec((None, dup, n_ctx), lambda b, i: (b, 0, 0)),
            pl.BlockSpec((None, band, KV_REP * BLOCK), cap_case),
        ],
        out_specs=pl.BlockSpec((BLOCK, n_q), lambda b, i: (b * n_blk + i, 0)),
        compiler_params=_cparams(("parallel", "arbitrary"), 56),
        name="window_attention",
    )(sink, q, k_l.reshape(batch, seq_len, dup), vt_l, k_c.reshape(batch, n_ctx, dup), vt_c,
      _window_cap(seq_len))


def _attn_out_kernel(x_ref, gt_ref, a_ref, wo_ref, lg_ref, lb_ref, o_ref):
    for r in range(x_ref.shape[0] // PROJ_SLAB):
        rows = pl.ds(r * PROJ_SLAB, PROJ_SLAB)
        y = _dot(a_ref[rows, :], wo_ref[...])
        z = ALPHA * x_ref[rows, :] + gt_ref[...] * y
        o_ref[rows, :] = _layer_norm_rows(z, lg_ref[...], lb_ref[...])


def _attn_out(x, mod, cond_of_tile, a, w_out, ln_g, ln_b, tm):
    rows = x.shape[0]
    return pl.pallas_call(
        _attn_out_kernel,
        out_shape=jax.ShapeDtypeStruct((rows, D_MODEL), F32),
        grid=(rows // tm,),
        in_specs=[
            pl.BlockSpec((tm, D_MODEL), lambda i: (i, 0)),
            _mod_spec(1, 5, cond_of_tile),
            pl.BlockSpec((tm, D_MODEL), lambda i: (i, 0)),
            pl.BlockSpec((D_MODEL, D_MODEL), lambda i: (0, 0), pipeline_mode=pl.Buffered(1)),
            _ln_spec(1, 1),
            _ln_spec(1, 1),
        ],
        out_specs=pl.BlockSpec((tm, D_MODEL), lambda i: (i, 0)),
        compiler_params=_cparams(("parallel",), 48),
        name="attn_out",
    )(x, mod, a, w_out, ln_g, ln_b)


def _dft_cos_sin(n):
    k = jnp.arange(n, dtype=jnp.int32)
    ang = ((k[:, None] * k[None, :]) % n).astype(F32) * (2.0 * math.pi / n)
    return jnp.cos(ang), jnp.sin(ang)


def _twiddles(n1, n2):
    k1 = jnp.arange(n1, dtype=jnp.int32)[:, None]
    m2 = jnp.arange(n2, dtype=jnp.int32)[None, :]
    n = n1 * n2
    ang = ((k1 * m2) % n).astype(F32) * (2.0 * math.pi / n)
    shape = (n1, n2, LANES)
    return (jnp.broadcast_to(jnp.cos(ang)[:, :, None], shape),
            jnp.broadcast_to(jnp.sin(ang)[:, :, None], shape))


def _rope_lane_tables(seq_len):
    rows = seq_len // GRID_W
    row = jnp.repeat(jnp.arange(rows, dtype=F32), GRID_W)
    col = jnp.tile(jnp.arange(GRID_W, dtype=F32), rows)
    n_freq = ROPE_FREQS
    inv_freq = jnp.power(ROPE_BASE, -jnp.arange(n_freq, dtype=F32) / n_freq)
    ang_r = row[:, None] * inv_freq
    ang_c = col[:, None] * inv_freq
    cos_h = jnp.concatenate([jnp.cos(ang_r), jnp.cos(ang_r), jnp.cos(ang_c), jnp.cos(ang_c)], axis=-1)
    sin_h = jnp.concatenate([-jnp.sin(ang_r), jnp.sin(ang_r), -jnp.sin(ang_c), jnp.sin(ang_c)], axis=-1)
    reps = LANES // HEAD_DIM
    return jnp.tile(cos_h, (1, reps)), jnp.tile(sin_h, (1, reps))


def kernel(x, c, ctx, c_ctx, w_mod, b_mod, ln_g, ln_b, ffn_w_gate, ffn_w_up, ffn_w_down,
           ab_w_in, ab_conv, ab_w_out, attn_w_in, attn_sink, attn_w_out):
    batch, seq_len, _ = x.shape
    n_ctx = ctx.shape[1]
    assert batch == 2 and seq_len % FFN_TM == 0 and seq_len % (FFT_N1 * SUBLANES * FFT1_GROUPS) == 0

    xl = x.reshape(batch * seq_len, D_MODEL)
    xc = ctx.reshape(batch * n_ctx, D_MODEL)
    ctx_rows = batch * n_ctx

    cond = jnp.concatenate([c, c_ctx[None, :], jnp.zeros((N_COND - batch - 1, D_MODEL), F32)], axis=0)
    mod = _modulation(cond, w_mod, b_mod)
    ln_g4 = ln_g.reshape(DEPTH, 3, 1, D_MODEL)
    ln_b4 = ln_b.reshape(DEPTH, 3, 1, D_MODEL)

    ffn_f32 = (ffn_w_gate, ffn_w_up, ffn_w_down)
    first_weights = tuple(w[0, 0].astype(BF16) for w in ffn_f32)
    mixer_f32 = (ab_w_in, ab_w_out, attn_w_in, attn_w_out)

    def lat_cond(tm):
        tiles = seq_len // tm
        return lambda i: i // tiles

    ctx_cond = lambda i: CTX_COND

    def ffn_both(xl, xc, layer, sub, weights, with_ctx, extra_jobs=()):
        nxt = (layer, sub + 1) if sub == 0 else (layer + 1, 0)
        jobs = tuple((w, nxt) for w in ffn_f32) if nxt[0] < DEPTH else ()
        xl, copies = _ffn(xl, mod, layer, sub, lat_cond(FFN_TM), weights, ln_g4, ln_b4, FFN_TM,
                          jobs + tuple(extra_jobs))
        if with_ctx:
            xc, _ = _ffn(xc, mod, layer, sub, ctx_cond, weights, ln_g4, ln_b4, ctx_rows)
        return xl, xc, copies[:len(jobs)], copies[len(jobs):]

    xl, xc, weights, (ab_in, ab_out, at_in, at_out) = ffn_both(
        xl, xc, 0, 0, first_weights, True, [(w, (0,)) for w in mixer_f32])

    cc, sc_ = _dft_cos_sin(FOURIER_GROUP_DIM)
    cs_chan = jnp.concatenate([cc, -sc_], axis=1).astype(BF16)
    n1, n2 = FFT_N1, seq_len // FFT_N1
    c1, s1 = _dft_cos_sin(n1)
    m1 = jnp.concatenate([jnp.concatenate([c1, s1], axis=1),
                          jnp.concatenate([-s1, c1], axis=1)], axis=0).astype(BF16)
    c2, s2 = _dft_cos_sin(n2)
    f2 = jnp.concatenate([c2, s2], axis=1).astype(BF16)
    twc, tws = _twiddles(n1, n2)
    cn, sn = _dft_cos_sin(n_ctx)
    f2_ctx = jnp.concatenate([cn, sn], axis=1).astype(BF16)
    twc_ctx, tws_ctx = _twiddles(1, n_ctx)

    gb, v, vf = _mix_in(xl, mod, lat_cond(PROJ_TM), ab_in, cs_chan, PROJ_TM)
    t = _fft1(vf, m1, batch, n1, n2)
    yb = _fft2(t, twc, tws, f2, batch, n1, n2)
    xl = _mix_out(xl, mod, lat_cond(PROJ_TM), gb, v, yb, ab_conv, ab_out, ln_g4, ln_b4, PROJ_TM, seq_len)

    gb_c, v_c, vf_c = _mix_in(xc, mod, ctx_cond, ab_in, cs_chan, n_ctx)
    yb_c = _fft2(vf_c, twc_ctx, tws_ctx, f2_ctx, batch, 1, n_ctx)
    xc = _mix_out(xc, mod, ctx_cond, gb_c, v_c, yb_c, ab_conv, ab_out, ln_g4, ln_b4, n_ctx, n_ctx)

    xl, xc, weights, _ = ffn_both(xl, xc, 0, 1, weights, True)

    xl, xc, weights, _ = ffn_both(xl, xc, 1, 0, weights, True)

    cos_t, sin_t = _rope_lane_tables(seq_len)
    n_qk = (N_Q_HEADS + N_KV_HEADS) * HEAD_DIM
    wv_t = jnp.transpose(attn_w_in[0, :, n_qk:]).astype(BF16)
    q, k_l, vt_l = _qkv_latent(xl, mod, lat_cond(PROJ_TM), at_in, wv_t, cos_t, sin_t, PROJ_TM, batch, seq_len)
    k_c, vt_c = _kv_context(xc, mod, ctx_cond, at_in, wv_t, n_ctx, batch)
    att = _attention(attn_sink.reshape(N_Q_HEADS), q, k_l, vt_l, k_c, vt_c, batch, seq_len, n_ctx)
    xl = _attn_out(xl, mod, lat_cond(PROJ_TM), att, at_out, ln_g4, ln_b4, PROJ_TM)

    xl, _, _, _ = ffn_both(xl, xc, 1, 1, weights, False)
    return xl.reshape(batch, seq_len, D_MODEL)
```
